```python
import jax, jax.numpy as jnp
from jax import lax
import numpy as np

D_MODEL = 1024
BATCH = 4
SEQ = 8192
DEPTH = 2

D_MIX = D_MODEL
GROUP_W = D_MIX // 4
ROPE_THETA = 500000.0
NORM_EPS = 1e-6
Q_BLOCK = 128
NEG = -1e30
FORCE = 1e4
F32 = jnp.float32

MLA_HEADS = 4
MLA_NOPE = 64
MLA_ROPE = 32
MLA_V = GROUP_W // MLA_HEADS
MLA_Q_RANK = 192
MLA_KV_RANK = 128

NSA_HEADS = 4
NSA_HD = GROUP_W // NSA_HEADS
NSA_ROT = NSA_HD // 4
CMP_STRIDE = 16
CMP_LEN = 2 * CMP_STRIDE
SEL_LEN = 64
SEL_TOPN = 16
WINDOW = 512

LRU_W = GROUP_W
LRU_BLOCKS = 4
LRU_BW = LRU_W // LRU_BLOCKS
LRU_C = 8.0
CONV_W = 4

SSD_HEADS = 4
SSD_HD = GROUP_W // SSD_HEADS
SSD_GROUPS = 2
SSD_STATE = 128
SSD_CHUNK = 128
SSD_DI = SSD_HEADS * SSD_HD
SSD_XBC = SSD_DI + 2 * SSD_GROUPS * SSD_STATE

MLA_COLS = MLA_Q_RANK + MLA_KV_RANK + MLA_ROPE
NSA_COLS = NSA_HEADS * NSA_HD + 6 * NSA_HD + 3 * NSA_HEADS
LRU_COLS = 2 * LRU_W
SSD_COLS = SSD_DI + SSD_XBC + SSD_HEADS
IN_COLS = MLA_COLS + NSA_COLS + LRU_COLS + SSD_COLS

D_FF = 3584
N_EXPERTS = 8
TOP_K = 2
N_DENSE = (DEPTH + 1) // 2
N_MOE = DEPTH // 2

kernel_name = 'hybrid_mla_nsa_rglru_ssd_moe'


def rms_norm(x, g):
    xf = x.astype(F32)
    y = xf * lax.rsqrt(jnp.mean(xf * xf, axis=-1, keepdims=True) + NORM_EPS)
    return (y * g.astype(F32)).astype(x.dtype)


def rope_tables(positions, rot_dim):
    inv = ROPE_THETA ** (-jnp.arange(0, rot_dim, 2, dtype=F32) / rot_dim)
    ang = positions.astype(F32)[:, None] * inv[None, :]
    return jnp.cos(ang), jnp.sin(ang)


def apply_rope(x, cos, sin, rot_dim):
    half = rot_dim // 2
    xf = x.astype(F32)
    x1, x2, xp = xf[..., :half], xf[..., half:rot_dim], xf[..., rot_dim:]
    c, s = cos[None, :, None, :], sin[None, :, None, :]
    return jnp.concatenate([x1 * c - x2 * s, x2 * c + x1 * s, xp], axis=-1).astype(x.dtype)


def causal_dwconv(x, w, b):
    K, C = w.shape
    y = lax.conv_general_dilated(x, w[:, None, :].astype(x.dtype), window_strides=(1,),
                                 padding=[(K - 1, 0)], dimension_numbers=('NWC', 'WIO', 'NWC'),
                                 feature_group_count=C)
    return y + b.astype(x.dtype)


def swiglu(h, wg, wu, wd):
    return (jax.nn.silu(h @ wg) * (h @ wu)) @ wd


def causal_attention_blocks(q, k, v, scale):
    B, S, H, Dk = q.shape
    nb = S // Q_BLOCK
    qb = q.reshape(B, nb, Q_BLOCK, H, Dk).transpose(1, 0, 2, 3, 4)
    kf, vf = k.astype(F32), v.astype(F32)
    kpos = jnp.arange(S)

    def one(args):
        i, qi = args
        s = jnp.einsum('bqhd,bkhd->bhqk', qi.astype(F32), kf) * scale
        qpos = i * Q_BLOCK + jnp.arange(Q_BLOCK)
        s = jnp.where(kpos[None, :] <= qpos[:, None], s, NEG)
        p = jax.nn.softmax(s, axis=-1)
        return jnp.einsum('bhqk,bkhd->bqhd', p, vf)

    o = lax.map(one, (jnp.arange(nb), qb))
    return o.transpose(1, 0, 2, 3, 4).reshape(B, S, H, v.shape[-1])


def mla_mixer(u, cos, sin, q_norm, w_uq, kv_norm, w_ukv):
    B, S, _ = u.shape
    c_q = u[..., :MLA_Q_RANK]
    c_kv = u[..., MLA_Q_RANK:MLA_Q_RANK + MLA_KV_RANK]
    k_pe = u[..., MLA_Q_RANK + MLA_KV_RANK:][:, :, None, :]
    q = (rms_norm(c_q, q_norm) @ w_uq).reshape(B, S, MLA_HEADS, MLA_NOPE + MLA_ROPE)
    kv = (rms_norm(c_kv, kv_norm) @ w_ukv).reshape(B, S, MLA_HEADS, MLA_NOPE + MLA_V)
    q = jnp.concatenate([q[..., :MLA_NOPE], apply_rope(q[..., MLA_NOPE:], cos, sin, MLA_ROPE)], axis=-1)
    k_pe = jnp.broadcast_to(apply_rope(k_pe, cos, sin, MLA_ROPE), (B, S, MLA_HEADS, MLA_ROPE))
    k = jnp.concatenate([kv[..., :MLA_NOPE], k_pe.astype(kv.dtype)], axis=-1)
    v = kv[..., MLA_NOPE:]
    o = causal_attention_blocks(q, k, v, (MLA_NOPE + MLA_ROPE) ** -0.5)
    return o.reshape(B, S, MLA_HEADS * MLA_V)


def nsa_mixer(u, cos, sin, cmp_pe, w_cmp):
    B, S, _ = u.shape
    H, d = NSA_HEADS, NSA_HD
    q = apply_rope(u[..., :H * d].reshape(B, S, H, d), cos, sin, NSA_ROT)
    kvs = u[..., H * d:H * d + 6 * d].reshape(B, S, 6, d)
    gates = jax.nn.sigmoid(u[..., H * d + 6 * d:].astype(F32)).reshape(B, S, H, 3)
    keys = apply_rope(kvs[:, :, 0::2], cos, sin, NSA_ROT)
    vals = kvs[:, :, 1::2]
    scale = d ** -0.5

    nc = S // CMP_STRIDE - 1

    def compress(t, pe, w):
        ch = t.reshape(B, S // CMP_STRIDE, CMP_STRIDE, d)
        blk = jnp.concatenate([ch[:, :-1], ch[:, 1:]], axis=2)
        return ((blk + pe).reshape(B, nc, CMP_LEN * d) @ w).astype(F32)

    kc = compress(keys[:, :, 0], cmp_pe[0], w_cmp[0])
    vc = compress(vals[:, :, 0], cmp_pe[1], w_cmp[1])
    cmp_start = jnp.arange(nc) * CMP_STRIDE
    cmp_last = cmp_start + CMP_LEN - 1

    n_sel = S // SEL_LEN
    n_top = min(SEL_TOPN, n_sel)
    ks_blk = keys[:, :, 1].astype(F32).reshape(B, n_sel, SEL_LEN, d)
    vs_blk = vals[:, :, 1].astype(F32).reshape(B, n_sel, SEL_LEN, d)
    sel_start = jnp.arange(n_sel) * SEL_LEN
    overlap = ((cmp_start[:, None] <= sel_start[None, :] + SEL_LEN - 1)
               & (cmp_last[:, None] >= sel_start[None, :])).astype(F32)

    kw_pad = jnp.pad(keys[:, :, 2].astype(F32), ((0, 0), (WINDOW, 0), (0, 0)))
    vw_pad = jnp.pad(vals[:, :, 2].astype(F32), ((0, 0), (WINDOW, 0), (0, 0)))

    nb = S // Q_BLOCK
    qb = q.reshape(B, nb, Q_BLOCK, H, d).transpose(1, 0, 2, 3, 4)
    gb = gates.reshape(B, nb, Q_BLOCK, H, 3).transpose(1, 0, 2, 3, 4)
    jj = jnp.arange(n_sel)
    woff = jnp.arange(WINDOW + Q_BLOCK)

    def one(args):
        i, qi, gi = args
        qf = qi.astype(F32)
        qpos = i * Q_BLOCK + jnp.arange(Q_BLOCK)
        m_c = cmp_last[None, :] <= qpos[:, None]
        s_c = jnp.einsum('bqhd,bnd->bhqn', qf, kc) * scale
        p_c = jax.nn.softmax(jnp.where(m_c, s_c, NEG), axis=-1) * m_c
        o_c = jnp.einsum('bhqn,bnd->bqhd', p_c, vc)
        imp = jnp.einsum('bhqn,nj->bqj', p_c, overlap)
        cur = qpos // SEL_LEN
        forced = (jj[None, :] == 0) | (jj[None, :] == cur[:, None]) | (jj[None, :] == cur[:, None] - 1)
        imp = jnp.where(jj[None, :] <= cur[:, None], imp + FORCE * forced, NEG)
        top_v, idx = lax.top_k(imp, n_top)
        ok = top_v > 0.5 * NEG
        ksg = jax.vmap(lambda kb, ix: kb[ix])(ks_blk, idx)
        vsg = jax.vmap(lambda vb, ix: vb[ix])(vs_blk, idx)
        tok = idx[..., None] * SEL_LEN + jnp.arange(SEL_LEN)
        m_s = (tok <= qpos[None, :, None, None]) & ok[..., None]
        s_s = jnp.einsum('bqhd,bqnkd->bhqnk', qf, ksg) * scale
        s_s = jnp.where(m_s[:, None], s_s, NEG)
        p_s = jax.nn.softmax(s_s.reshape(B, H, Q_BLOCK, -1), axis=-1).reshape(s_s.shape)
        o_s = jnp.einsum('bhqnk,bqnkd->bqhd', p_s, vsg)
        kwi = lax.dynamic_slice_in_dim(kw_pad, i * Q_BLOCK, WINDOW + Q_BLOCK, axis=1)
        vwi = lax.dynamic_slice_in_dim(vw_pad, i * Q_BLOCK, WINDOW + Q_BLOCK, axis=1)
        kpos = i * Q_BLOCK - WINDOW + woff
        m_w = ((kpos[None, :] <= qpos[:, None]) & (kpos[None, :] > qpos[:, None] - WINDOW)
               & (kpos[None, :] >= 0))
        s_w = jnp.einsum('bqhd,bkd->bhqk', qf, kwi) * scale
        p_w = jax.nn.softmax(jnp.where(m_w, s_w, NEG), axis=-1)
        o_w = jnp.einsum('bhqk,bkd->bqhd', p_w, vwi)
        return gi[..., 0:1] * o_c + gi[..., 1:2] * o_s + gi[..., 2:3] * o_w

    o = lax.map(one, (jnp.arange(nb), qb, gb))
    return o.transpose(1, 0, 2, 3, 4).reshape(B, S, H * d)


def rglru_mixer(u, conv_w, conv_b, w_gate, b_gate, lam):
    B, S, _ = u.shape
    xb = causal_dwconv(u[..., :LRU_W], conv_w, conv_b).astype(F32)
    gbr = u[..., LRU_W:].astype(F32)
    xblk = xb.reshape(B, S, LRU_BLOCKS, LRU_BW)
    g = jnp.einsum('bsnc,gncd->gbsnd', xblk, w_gate.astype(F32)).reshape(2, B, S, LRU_W)
    g = g + b_gate.astype(F32)[:, None, None, :]
    r, i_g = jax.nn.sigmoid(g[0]), jax.nn.sigmoid(g[1])
    log_a = -LRU_C * r * jax.nn.softplus(-lam.astype(F32))
    a = jnp.exp(log_a)
    b = jnp.sqrt(jnp.maximum(-jnp.expm1(2.0 * log_a), 0.0)) * (i_g * xb)

    def comb(e1, e2):
        a1, b1 = e1
        a2, b2 = e2
        return a1 * a2, a2 * b1 + b2

    _, h = lax.associative_scan(comb, (a, b), axis=1)
    return h * jax.nn.gelu(gbr)


def segsum(x):
    T = x.shape[-1]
    xx = jnp.broadcast_to(x[..., None], x.shape + (T,))
    xx = jnp.where(jnp.tril(jnp.ones((T, T), bool), -1), xx, 0.0)
    cs = jnp.cumsum(xx, axis=-2)
    return jnp.where(jnp.tril(jnp.ones((T, T), bool), 0), cs, -jnp.inf)


def ssd_chunked(xdt, dA, Bm, Cm):
    B, S, H, P = xdt.shape
    N = Bm.shape[-1]
    L = SSD_CHUNK
    c = S // L
    x = xdt.reshape(B, c, L, H, P)
    Bc = Bm.reshape(B, c, L, H, N)
    Cc = Cm.reshape(B, c, L, H, N)
    a = dA.reshape(B, c, L, H).transpose(0, 3, 1, 2)
    a_cs = jnp.cumsum(a, axis=-1)
    Lm = jnp.exp(segsum(a))
    scores = jnp.einsum('bclhn,bcshn->bhcls', Cc, Bc) * Lm
    y_diag = jnp.einsum('bhcls,bcshp->bclhp', scores, x)
    decay_states = jnp.exp(a_cs[..., -1:] - a_cs)
    states = jnp.einsum('bclhn,bhcl,bclhp->bchpn', Bc, decay_states, x)
    states = jnp.concatenate([jnp.zeros_like(states[:, :1]), states], axis=1)
    decay_chunk = jnp.exp(segsum(jnp.pad(a_cs[..., -1], ((0, 0), (0, 0), (1, 0)))))
    states = jnp.einsum('bhzc,bchpn->bzhpn', decay_chunk, states)[:, :-1]
    y_off = jnp.einsum('bclhn,bchpn,bhcl->bclhp', Cc, states, jnp.exp(a_cs))
    return (y_diag + y_off).reshape(B, S, H, P)


def ssd_mixer(u, conv_w, conv_b, dt_bias, a_log, d_skip, norm_g):
    B, S, _ = u.shape
    z = u[..., :SSD_DI].astype(F32)
    xbc = jax.nn.silu(causal_dwconv(u[..., SSD_DI:SSD_DI + SSD_XBC], conv_w, conv_b).astype(F32))
    dt = u[..., SSD_DI + SSD_XBC:].astype(F32)
    gn = SSD_GROUPS * SSD_STATE
    xs = xbc[..., :SSD_DI].reshape(B, S, SSD_HEADS, SSD_HD)
    rep = SSD_HEADS // SSD_GROUPS
    Bm = jnp.repeat(xbc[..., SSD_DI:SSD_DI + gn].reshape(B, S, SSD_GROUPS, SSD_STATE), rep, axis=2)
    Cm = jnp.repeat(xbc[..., SSD_DI + gn:].reshape(B, S, SSD_GROUPS, SSD_STATE), rep, axis=2)
    dt = jax.nn.softplus(dt + dt_bias.astype(F32))
    A = -jnp.exp(a_log.astype(F32))
    y = ssd_chunked(xs * dt[..., None], dt * A, Bm, Cm)
    y = y + xs * d_skip.astype(F32)[:, None]
    y = y.reshape(B, S, SSD_DI) * jax.nn.silu(z)
    return rms_norm(y, norm_g)


def moe_ffn(h, router, w_gate, w_up, w_down):
    B, S, D = h.shape
    t = h.reshape(B * S, D)
    logits = (t @ router).astype(F32)
    top_v, idx = lax.top_k(logits, TOP_K)
    wts = jax.nn.softmax(top_v, axis=-1)
    gate = jnp.sum(jax.nn.one_hot(idx, N_EXPERTS, dtype=F32) * wts[..., None], axis=1)
    out = jnp.zeros((B * S, D), F32)
    for e in range(N_EXPERTS):
        out = out + gate[:, e:e + 1] * swiglu(t, w_gate[e], w_up[e], w_down[e]).astype(F32)
    return out.reshape(B, S, D)


def setup_inputs(seed: int = 0) -> dict:
    key = jax.random.key(seed)
    ks = iter(jax.random.split(key, 40))
    nrm = lambda shape, scale: jax.random.normal(next(ks), shape, F32) * scale
    gain = lambda shape: 1.0 + 0.02 * jax.random.normal(next(ks), shape, F32)
    L = DEPTH
    u_a = jax.random.uniform(next(ks), (L, LRU_W), F32, 0.9, 0.999)
    s_a = u_a ** (1.0 / LRU_C)
    dt0 = jnp.exp(jax.random.uniform(next(ks), (L, SSD_HEADS), F32, np.log(1e-3), np.log(1e-1)))
    return {
        'x': jax.random.normal(next(ks), (BATCH, SEQ, D_MODEL), F32),
        'positions': jnp.arange(SEQ, dtype=jnp.int32),
        'norm_mix': gain((L, D_MODEL)),
        'w_in': nrm((L, D_MODEL, IN_COLS), D_MODEL ** -0.5),
        'mla_q_norm': gain((L, MLA_Q_RANK)),
        'mla_w_uq': nrm((L, MLA_Q_RANK, MLA_HEADS * (MLA_NOPE + MLA_ROPE)), MLA_Q_RANK ** -0.5),
        'mla_kv_norm': gain((L, MLA_KV_RANK)),
        'mla_w_ukv': nrm((L, MLA_KV_RANK, MLA_HEADS * (MLA_NOPE + MLA_V)), MLA_KV_RANK ** -0.5),
        'nsa_cmp_pe': nrm((L, 2, CMP_LEN, NSA_HD), 0.02),
        'nsa_w_cmp': nrm((L, 2, CMP_LEN * NSA_HD, NSA_HD), (CMP_LEN * NSA_HD) ** -0.5),
        'lru_conv_w': nrm((L, CONV_W, LRU_W), CONV_W ** -0.5),
        'lru_conv_b': nrm((L, LRU_W), 0.01),
        'lru_w_gate': nrm((L, 2, LRU_BLOCKS, LRU_BW, LRU_BW), LRU_BW ** -0.5),
        'lru_b_gate': nrm((L, 2, LRU_W), 0.01),
        'lru_lambda': jnp.log(s_a / (1.0 - s_a)),
        'ssd_conv_w': nrm((L, CONV_W, SSD_XBC), CONV_W ** -0.5),
        'ssd_conv_b': nrm((L, SSD_XBC), 0.01),
        'ssd_dt_bias': dt0 + jnp.log(-jnp.expm1(-dt0)),
        'ssd_a_log': jnp.log(jax.random.uniform(next(ks), (L, SSD_HEADS), F32, 1.0, 16.0)),
        'ssd_d': gain((L, SSD_HEADS)),
        'group_norm': gain((L, 4, GROUP_W)),
        'w_out': nrm((L, D_MIX, D_MODEL), D_MIX ** -0.5),
        'norm_ffn': gain((L, D_MODEL)),
        'ffn_w_gate': nrm((N_DENSE, D_MODEL, D_FF), D_MODEL ** -0.5),
        'ffn_w_up': nrm((N_DENSE, D_MODEL, D_FF), D_MODEL ** -0.5),
        'ffn_w_down': nrm((N_DENSE, D_FF, D_MODEL), D_FF ** -0.5),
        'moe_router': nrm((N_MOE, D_MODEL, N_EXPERTS), D_MODEL ** -0.5),
        'moe_w_gate': nrm((N_MOE, N_EXPERTS, D_MODEL, D_FF), D_MODEL ** -0.5),
        'moe_w_up': nrm((N_MOE, N_EXPERTS, D_MODEL, D_FF), D_MODEL ** -0.5),
        'moe_w_down': nrm((N_MOE, N_EXPERTS, D_FF, D_MODEL), D_FF ** -0.5),
        'norm_final': gain((D_MODEL,)),
    }


def reference(x, positions, norm_mix, w_in, mla_q_norm, mla_w_uq, mla_kv_norm, mla_w_ukv,
              nsa_cmp_pe, nsa_w_cmp, lru_conv_w, lru_conv_b, lru_w_gate, lru_b_gate, lru_lambda,
              ssd_conv_w, ssd_conv_b, ssd_dt_bias, ssd_a_log, ssd_d, group_norm, w_out, norm_ffn,
              ffn_w_gate, ffn_w_up, ffn_w_down, moe_router, moe_w_gate, moe_w_up, moe_w_down,
              norm_final):
    cos_m, sin_m = rope_tables(positions, MLA_ROPE)
    cos_n, sin_n = rope_tables(positions, NSA_ROT)
    o1 = MLA_COLS
    o2 = o1 + NSA_COLS
    o3 = o2 + LRU_COLS
    h_res = x
    for l in range(DEPTH):
        h = rms_norm(h_res, norm_mix[l])
        u = h @ w_in[l]
        y_a = mla_mixer(u[..., :o1], cos_m, sin_m, mla_q_norm[l], mla_w_uq[l], mla_kv_norm[l], mla_w_ukv[l])
        y_b = nsa_mixer(u[..., o1:o2], cos_n, sin_n, nsa_cmp_pe[l], nsa_w_cmp[l])
        y_c = rglru_mixer(u[..., o2:o3], lru_conv_w[l], lru_conv_b[l], lru_w_gate[l], lru_b_gate[l], lru_lambda[l])
        y_d = ssd_mixer(u[..., o3:], ssd_conv_w[l], ssd_conv_b[l], ssd_dt_bias[l], ssd_a_log[l], ssd_d[l],
                        group_norm[l, 3])
        g = group_norm[l]
        y = jnp.concatenate([rms_norm(y_a, g[0]), rms_norm(y_b, g[1]), rms_norm(y_c, g[2]), y_d], axis=-1)
        h_res = h_res + y @ w_out[l]
        h = rms_norm(h_res, norm_ffn[l])
        if l % 2 == 0:
            f = swiglu(h, ffn_w_gate[l // 2], ffn_w_up[l // 2], ffn_w_down[l // 2])
        else:
            f = moe_ffn(h, moe_router[l // 2], moe_w_gate[l // 2], moe_w_up[l // 2], moe_w_down[l // 2])
        h_res = h_res + f
    return rms_norm(h_res, norm_final).astype(x.dtype)
```

```python
import functools

import numpy as np
import jax
import jax.numpy as jnp
from jax import lax
from jax.experimental import pallas as pl
from jax.experimental.pallas import tpu as pltpu

F32 = jnp.float32
BF16 = jnp.bfloat16

D_MODEL = 1024
GROUP_W = 256
ROPE_THETA = 500000.0
NORM_EPS = 1e-6
NEG = -1e30
FORCE = 1e4
MLA_HEADS, MLA_NOPE, MLA_ROPE, MLA_V = 4, 64, 32, 64
MLA_Q_RANK, MLA_KV_RANK = 192, 128
NSA_HEADS, NSA_HD, NSA_ROT = 4, 64, 16
CMP_STRIDE, CMP_LEN, SEL_LEN, SEL_TOPN, WINDOW = 16, 32, 64, 16, 512
LRU_W, LRU_BLOCKS, LRU_C, CONV_W = 256, 4, 8.0, 4
SSD_HEADS, SSD_HD, SSD_GROUPS, SSD_STATE, SSD_CHUNK = 4, 64, 2, 128, 128
SSD_DI = 256
SSD_XBC = SSD_DI + 2 * SSD_GROUPS * SSD_STATE
D_FF = 3584
N_EXPERTS = 8

ROPE_W = 768
C_ROPE, C_ROT = 0, 768
C_CQ, C_CKV, C_VSW, C_GATE, C_LRU, C_SSD = 1536, 1792, 1920, 2048, 2176, 2688
SSD_IN_W = 1152
N_EXT = C_SSD + SSD_IN_W
SEL_LANES = 128

VMEM_LIMIT = 56 * 1024 * 1024


def _cparams(sem):
    return pltpu.CompilerParams(dimension_semantics=sem, vmem_limit_bytes=VMEM_LIMIT)


def _dot(a, b):
    return jnp.dot(a.astype(BF16), b.astype(BF16), preferred_element_type=F32)


def _dot_nt(a, b):
    return lax.dot_general(a.astype(BF16), b.astype(BF16), (((1,), (1,)), ((), ())),
                           preferred_element_type=F32)


def _dot_f32(a, b):
    return jnp.dot(a, b, precision=lax.Precision.HIGHEST, preferred_element_type=F32)


def _dot_f32_nt(a, b):
    return lax.dot_general(a, b, (((1,), (1,)), ((), ())), precision=lax.Precision.HIGHEST,
                           preferred_element_type=F32)


def _sigmoid(x):
    return 1.0 / (1.0 + jnp.exp(-x))


def _softplus(x):
    return jnp.maximum(x, 0.0) + jnp.log(1.0 + jnp.exp(-jnp.abs(x)))


def _rms(x, width):
    return x * lax.rsqrt(jnp.sum(x * x, axis=-1, keepdims=True) * (1.0 / width) + NORM_EPS)


def _inproj_kernel(x_ref, g_ref, w_ref, cos_ref, sin_ref,
                   ua_ref, qn_ref, kvc_ref, ksel_ref, kwin_ref, vsw_ref, gate_ref, uc_ref, ud_ref,
                   *, tm, seq_blocks):
    x = x_ref[...]
    h = _rms(x, D_MODEL) * g_ref[...]
    y = _dot(h, w_ref[...])
    roped = y[:, C_ROPE:C_ROPE + ROPE_W] * cos_ref[...] + y[:, C_ROT:C_ROT + ROPE_W] * sin_ref[...]
    ua_ref[:, 0:384] = y[:, C_CQ:C_CQ + 384]
    ua_ref[:, 384:512] = roped[:, 0:128]
    qn_ref[...] = roped[:, 128:384]
    kvc_ref[...] = roped[:, 384:512]
    s0 = (pl.program_id(0) % seq_blocks) * tm
    pos = s0 + lax.broadcasted_iota(jnp.int32, (tm, SEL_LANES), 0)
    blk = lax.broadcasted_iota(jnp.int32, (tm, SEL_LANES), 1)
    onehot = jnp.where(jnp.right_shift(pos, 6) == blk, 1.0, 0.0)
    ksel_ref[:, 0:128] = roped[:, 512:640].astype(BF16)
    ksel_ref[:, 128:256] = onehot.astype(BF16)
    kwin_ref[:, 0:128] = roped[:, 640:768].astype(BF16)
    kwin_ref[:, 128:256] = jnp.zeros((tm, 128), BF16)
    vsw_ref[...] = y[:, C_VSW:C_VSW + 128].astype(BF16)
    gate_ref[...] = y[:, C_GATE:C_GATE + 128]
    uc_ref[...] = y[:, C_LRU:C_LRU + 512]
    ud_ref[...] = y[:, C_SSD:C_SSD + SSD_IN_W]


def _inproj(x2d, g, w_ext, cos_t, sin_t, seq):
    T = x2d.shape[0]
    tm = 256
    seq_blocks = seq // tm
    row = lambda w: pl.BlockSpec((tm, w), lambda i: (i, 0))
    full = lambda a: pl.BlockSpec(a.shape, lambda i: (0,) * a.ndim)
    tab = pl.BlockSpec((tm, ROPE_W), lambda i: (i % seq_blocks, 0))
    outs = [(512, F32), (256, F32), (128, F32), (256, BF16), (256, BF16), (128, BF16), (128, F32),
            (512, F32), (SSD_IN_W, F32)]
    return pl.pallas_call(
        functools.partial(_inproj_kernel, tm=tm, seq_blocks=seq_blocks),
        grid=(T // tm,),
        in_specs=[row(D_MODEL), full(g), full(w_ext), tab, tab],
        out_specs=[row(w) for w, _ in outs],
        out_shape=[jax.ShapeDtypeStruct((T, w), dt) for w, dt in outs],
        compiler_params=_cparams(("parallel",)),
        name="inproj",
    )(x2d, g, w_ext, cos_t, sin_t)


def _mla_prep_kernel(ua_ref, qg_ref, w1_ref, w2_ref, kvg_ref, wk_ref, wv_ref, cos_ref, sin_ref,
                     q_ref, k_ref, v_ref):
    ua = ua_ref[...]
    cq = _rms(ua[:, 0:256], MLA_Q_RANK) * qg_ref[...]
    y1 = _dot(cq, w1_ref[...])
    y2 = _dot(cq, w2_ref[...])
    ckv = _rms(ua[:, 256:384], MLA_KV_RANK) * kvg_ref[...]
    kn = _dot(ckv, wk_ref[...])
    vv = _dot(ckv, wv_ref[...])
    kpe = ua[:, 384:512]
    cos = cos_ref[...]
    sin = sin_ref[...]
    scale = (MLA_NOPE + MLA_ROPE) ** -0.5
    for h in range(MLA_HEADS):
        sl = slice(h * 128, (h + 1) * 128)
        q_ref[h] = ((y1[:, sl] * cos + y2[:, sl] * sin) * scale).astype(BF16)
        k_ref[h] = (kn[:, sl] + kpe).astype(BF16)
        v_ref[h] = vv[:, h * MLA_V:(h + 1) * MLA_V].astype(BF16)


def _mla_prep(ua, qg, w1, w2, kvg, wk, wv, cos_m, sin_m, B, S):
    tm = 512
    nb = S // tm
    full = lambda a: pl.BlockSpec(a.shape, lambda b, i: (0,) * a.ndim)
    tab = pl.BlockSpec((tm, 128), lambda b, i: (i, 0))
    hd = lambda w: pl.BlockSpec((None, MLA_HEADS, tm, w), lambda b, i: (b, 0, i, 0))
    return pl.pallas_call(
        _mla_prep_kernel,
        grid=(B, nb),
        in_specs=[pl.BlockSpec((tm, 512), lambda b, i: (b * nb + i, 0)),
                  full(qg), full(w1), full(w2), full(kvg), full(wk), full(wv), tab, tab],
        out_specs=[hd(128), hd(128), hd(MLA_V)],
        out_shape=[jax.ShapeDtypeStruct((B, MLA_HEADS, S, 128), BF16),
                   jax.ShapeDtypeStruct((B, MLA_HEADS, S, 128), BF16),
                   jax.ShapeDtypeStruct((B, MLA_HEADS, S, MLA_V), BF16)],
        compiler_params=_cparams(("parallel", "parallel")),
        name="mla_prep",
    )(ua, qg, w1, w2, kvg, wk, wv, cos_m, sin_m)


def _flash_kernel(qi_ref, ki_ref, first_ref, last_ref, q_ref, k_ref, v_ref, o_ref, m_ref, l_ref, acc_ref,
                  *, heads, kv_heads, t, window, out_lo, out_w):
    p_idx = pl.program_id(1)
    qi = qi_ref[p_idx]
    ki = ki_ref[p_idx]

    @pl.when(first_ref[p_idx] == 1)
    def _init():
        m_ref[...] = jnp.full(m_ref.shape, NEG, F32)
        l_ref[...] = jnp.zeros(l_ref.shape, F32)
        acc_ref[...] = jnp.zeros(acc_ref.shape, F32)

    q = q_ref[...]
    k = k_ref[...]
    v = v_ref[...]
    dk = q.shape[-1]
    dv = v.shape[-1]
    if kv_heads == 1:
        s = lax.dot_general(q.reshape(heads * t, dk), k[0], (((1,), (1,)), ((), ())),
                            preferred_element_type=F32).reshape(heads, t, t)
    else:
        s = jnp.einsum("hqd,hkd->hqk", q, k, preferred_element_type=F32)
    qpos = qi * t + lax.broadcasted_iota(jnp.int32, (t, t), 0)
    kpos = ki * t + lax.broadcasted_iota(jnp.int32, (t, t), 1)
    mask = kpos <= qpos
    if window is not None:
        mask = mask & (kpos > qpos - window)
    s = jnp.where(mask[None], s, NEG)
    m_old = m_ref[...]
    m_new = jnp.maximum(m_old, jnp.max(s, axis=-1, keepdims=True))
    alpha = jnp.exp(m_old - m_new)
    p = jnp.exp(s - m_new)
    l_ref[...] = alpha * l_ref[...] + jnp.sum(p, axis=-1, keepdims=True)
    if kv_heads == 1:
        pv = jnp.dot(p.reshape(heads * t, t).astype(BF16), v[0],
                     preferred_element_type=F32).reshape(heads, t, dv)
    else:
        pv = jnp.einsum("hqk,hkd->hqd", p.astype(BF16), v, preferred_element_type=F32)
    acc_ref[...] = alpha * acc_ref[...] + pv
    m_ref[...] = m_new

    @pl.when(last_ref[p_idx] == 1)
    def _fin():
        for h in range(heads):
            o_ref[:, h * out_w:(h + 1) * out_w] = acc_ref[h][:, out_lo:out_lo + out_w] / l_ref[h]


def _pair_tables(nq, back):
    qi, ki, first, last = [], [], [], []
    for i in range(nq):
        lo = 0 if back is None else max(0, i - back)
        for j in range(lo, i + 1):
            qi.append(i)
            ki.append(j)
            first.append(1 if j == lo else 0)
            last.append(1 if j == i else 0)
    mk = lambda a: jnp.asarray(np.asarray(a, np.int32))
    return mk(qi), mk(ki), mk(first), mk(last)


def _flash(q, k, v, *, window, out_lo, out_w, t=512):
    B, H, S, dk = q.shape
    Hk = k.shape[1]
    dv = v.shape[-1]
    t = min(t, S)
    nq = S // t
    back = None if window is None else -(-window // t)
    tabs = _pair_tables(nq, back)
    npairs = int(tabs[0].shape[0])
    grid_spec = pltpu.PrefetchScalarGridSpec(
        num_scalar_prefetch=4,
        grid=(B, npairs),
        in_specs=[pl.BlockSpec((None, H, t, dk), lambda b, p, qi, ki, fi, la: (b, 0, qi[p], 0)),
                  pl.BlockSpec((None, Hk, t, dk), lambda b, p, qi, ki, fi, la: (b, 0, ki[p], 0)),
                  pl.BlockSpec((None, Hk, t, dv), lambda b, p, qi, ki, fi, la: (b, 0, ki[p], 0))],
        out_specs=pl.BlockSpec((None, t, H * out_w), lambda b, p, qi, ki, fi, la: (b, qi[p], 0)),
        scratch_shapes=[pltpu.VMEM((H, t, 1), F32), pltpu.VMEM((H, t, 1), F32), pltpu.VMEM((H, t, dv), F32)],
    )
    return pl.pallas_call(
        functools.partial(_flash_kernel, heads=H, kv_heads=Hk, t=t, window=window, out_lo=out_lo, out_w=out_w),
        grid_spec=grid_spec,
        out_shape=jax.ShapeDtypeStruct((B, S, H * out_w), F32),
        compiler_params=_cparams(("parallel", "arbitrary")),
        name="flash_attn",
    )(*tabs, q, k, v)


def _compress_kernel(x_ref, wa_ref, wb_ref, pek_ref, pev_ref, wkf_ref, wvf_ref, o_ref, *, nc_pad):
    x = x_ref[...]
    a = _dot_f32(x, wa_ref[...])
    b = _dot_f32(x, wb_ref[...])
    ck = _dot_f32(pek_ref[...], wkf_ref[...])[0:1]
    cv = _dot_f32(pev_ref[...], wvf_ref[...])[0:1]
    const = jnp.concatenate([ck, cv], axis=-1)
    b_next = pltpu.roll(b, nc_pad - 1, axis=0)
    row = lax.broadcasted_iota(jnp.int32, (nc_pad, 128), 0)
    o_ref[...] = jnp.where(row < nc_pad - 1, a + b_next + const, 0.0)


def _compress(x, wa, wb, pek, pev, wkf, wvf):
    B, nc_pad, w = x.shape
    full = lambda a: pl.BlockSpec(a.shape, lambda b: (0,) * a.ndim)
    return pl.pallas_call(
        functools.partial(_compress_kernel, nc_pad=nc_pad),
        grid=(B,),
        in_specs=[pl.BlockSpec((None, nc_pad, w), lambda b: (b, 0, 0)),
                  full(wa), full(wb), full(pek), full(pev), full(wkf), full(wvf)],
        out_specs=pl.BlockSpec((None, nc_pad, 128), lambda b: (b, 0, 0)),
        out_shape=jax.ShapeDtypeStruct((B, nc_pad, 128), F32),
        compiler_params=_cparams(("parallel",)),
        name="nsa_compress",
    )(x, wa, wb, pek, pev, wkf, wvf)


def _cmp_select_kernel(q_ref, kvc_ref, ov_ref, oc_ref, qaug_ref, *, tq, nc_pad, n_top):
    i = pl.program_id(1)
    q = q_ref[...]
    kc = kvc_ref[:, 0:64]
    vc = kvc_ref[:, 64:128]
    scale = NSA_HD ** -0.5
    qpos = i * tq + lax.broadcasted_iota(jnp.int32, (tq, 1), 0)
    n_idx = lax.broadcasted_iota(jnp.int32, (1, nc_pad), 1)
    m_c = (n_idx * CMP_STRIDE + (CMP_LEN - 1) <= qpos) & (n_idx < nc_pad - 1)
    psum = jnp.zeros((tq, nc_pad), F32)
    for h in range(NSA_HEADS):
        qh = q[:, h * NSA_HD:(h + 1) * NSA_HD]
        s = jnp.where(m_c, _dot_f32_nt(qh, kc) * scale, NEG)
        e = jnp.where(m_c, jnp.exp(s - jnp.max(s, axis=-1, keepdims=True)), 0.0)
        den = jnp.sum(e, axis=-1, keepdims=True)
        p = e / jnp.where(den > 0.0, den, 1.0)
        oc_ref[:, h * NSA_HD:(h + 1) * NSA_HD] = _dot(p, vc)
        psum = psum + p
    imp = _dot_f32(psum, ov_ref[...])
    cur = jnp.right_shift(qpos, 6)
    jj = lax.broadcasted_iota(jnp.int32, (1, SEL_LANES), 1)
    forced = (jj == 0) | (jj == cur) | (jj == cur - 1)
    imp = jnp.where(jj <= cur, imp + FORCE * jnp.where(forced, 1.0, 0.0), NEG)
    jf = jj.astype(F32)
    bias = jnp.full((tq, SEL_LANES), NEG, F32)
    for _ in range(n_top):
        mx = jnp.max(imp, axis=-1, keepdims=True)
        idx = jnp.min(jnp.where(imp == mx, jf, float(SEL_LANES)), axis=-1, keepdims=True)
        hit = jf == idx
        bias = jnp.where(hit & (mx > 0.5 * NEG), 0.0, bias)
        imp = jnp.where(hit, -3e38, imp)
    bias = bias.astype(BF16)
    qs = (q * scale).astype(BF16)
    for h in range(NSA_HEADS):
        qaug_ref[h, :, 0:64] = qs[:, h * NSA_HD:(h + 1) * NSA_HD]
        qaug_ref[h, :, 64:128] = jnp.zeros((tq, 64), BF16)
        qaug_ref[h, :, 128:256] = bias


def _cmp_select(qn, kvcmp, ov, B, S):
    tq = 256
    nb = S // tq
    nc_pad = kvcmp.shape[1]
    n_top = min(SEL_TOPN, S // SEL_LEN)
    return pl.pallas_call(
        functools.partial(_cmp_select_kernel, tq=tq, nc_pad=nc_pad, n_top=n_top),
        grid=(B, nb),
        in_specs=[pl.BlockSpec((tq, 256), lambda b, i: (b * nb + i, 0)),
                  pl.BlockSpec((None, nc_pad, 128), lambda b, i: (b, 0, 0)),
                  pl.BlockSpec(ov.shape, lambda b, i: (0, 0))],
        out_specs=[pl.BlockSpec((tq, 256), lambda b, i: (b * nb + i, 0)),
                   pl.BlockSpec((None, NSA_HEADS, tq, 256), lambda b, i: (b, 0, i, 0))],
        out_shape=[jax.ShapeDtypeStruct((B * S, 256), F32),
                   jax.ShapeDtypeStruct((B, NSA_HEADS, S, 256), BF16)],
        compiler_params=_cparams(("parallel", "parallel")),
        name="nsa_cmp_select",
    )(qn, kvcmp, ov)


def _shift_scan(a, b, t, width):
    row = lax.broadcasted_iota(jnp.int32, (t, width), 0)
    s = 1
    while s < t:
        keep = row >= s
        a_sh = jnp.where(keep, pltpu.roll(a, s, axis=0), 1.0)
        b_sh = jnp.where(keep, pltpu.roll(b, s, axis=0), 0.0)
        b = a * b_sh + b
        a = a * a_sh
        s *= 2
    return a, b


def _causal_conv(xpad_ref, x, cw_ref, cb_ref, t):
    xpad_ref[8:8 + t, :] = x
    y = cb_ref[...] + cw_ref[CONV_W - 1:CONV_W, :] * x
    for k in range(CONV_W - 1):
        off = 8 - (CONV_W - 1) + k
        y = y + cw_ref[k:k + 1, :] * xpad_ref[off:off + t, :]
    xpad_ref[0:8, :] = x[t - 8:t, :]
    return y


def _lru_kernel(u_ref, cw_ref, cb_ref, wg_ref, bg_ref, sp_ref, y_ref, xpad_ref, h_ref, *, t):
    @pl.when(pl.program_id(1) == 0)
    def _init():
        xpad_ref[0:8, :] = jnp.zeros((8, LRU_W), F32)
        h_ref[...] = jnp.zeros(h_ref.shape, F32)

    x = u_ref[:, 0:LRU_W]
    gbr = u_ref[:, LRU_W:2 * LRU_W]
    xb = _causal_conv(xpad_ref, x, cw_ref, cb_ref, t)
    g = _dot(xb, wg_ref[...]) + bg_ref[...]
    r = _sigmoid(g[:, 0:LRU_W])
    ig = _sigmoid(g[:, LRU_W:2 * LRU_W])
    log_a = -LRU_C * r * sp_ref[...]
    a = jnp.exp(log_a)
    one_m = -jnp.tanh(log_a) * (a * a + 1.0)
    b = jnp.sqrt(jnp.maximum(one_m, 0.0)) * (ig * xb)
    a_cum, h = _shift_scan(a, b, t, LRU_W)
    h = h + a_cum * h_ref[0:1, :]
    h_ref[0:1, :] = h[t - 1:t, :]
    gelu = 0.5 * gbr * (1.0 + jnp.tanh(0.7978845608028654 * (gbr + 0.044715 * gbr * gbr * gbr)))
    y_ref[...] = h * gelu


def _lru(uc, cw, cb, wg, bg, sp, B, S):
    t = 512
    nb = S // t
    full = lambda a: pl.BlockSpec(a.shape, lambda b, i: (0,) * a.ndim)
    return pl.pallas_call(
        functools.partial(_lru_kernel, t=t),
        grid=(B, nb),
        in_specs=[pl.BlockSpec((t, 2 * LRU_W), lambda b, i: (b * nb + i, 0)),
                  full(cw), full(cb), full(wg), full(bg), full(sp)],
        out_specs=pl.BlockSpec((t, LRU_W), lambda b, i: (b * nb + i, 0)),
        out_shape=jax.ShapeDtypeStruct((B * S, LRU_W), F32),
        scratch_shapes=[pltpu.VMEM((t + 8, LRU_W), F32), pltpu.VMEM((8, LRU_W), F32)],
        compiler_params=_cparams(("parallel", "arbitrary")),
        name="rglru",
    )(uc, cw, cb, wg, bg, sp)


def _ssd_kernel(u_ref, cw_ref, cb_ref, dtb_ref, a_ref, d_ref, ng_ref, y_ref, xpad_ref, st_ref, *, t):
    @pl.when(pl.program_id(1) == 0)
    def _init():
        xpad_ref[0:8, :] = jnp.zeros((8, SSD_XBC), F32)
        st_ref[...] = jnp.zeros(st_ref.shape, F32)

    L = SSD_CHUNK
    z = u_ref[:, 0:SSD_DI]
    conv = _causal_conv(xpad_ref, u_ref[:, SSD_DI:SSD_DI + SSD_XBC], cw_ref, cb_ref, t)
    xbc = conv * _sigmoid(conv)
    dt = _softplus(u_ref[:, SSD_DI + SSD_XBC:SSD_IN_W] + dtb_ref[...])
    a = dt * a_ref[...]
    row = lax.broadcasted_iota(jnp.int32, (t, 128), 0)
    rin = jnp.bitwise_and(row, L - 1)
    cs = a
    s = 1
    while s < L:
        cs = cs + jnp.where(rin >= s, pltpu.roll(cs, s, axis=0), 0.0)
        s *= 2
    tril = lax.broadcasted_iota(jnp.int32, (L, L), 0) >= lax.broadcasted_iota(jnp.int32, (L, L), 1)
    gn = SSD_GROUPS * SSD_STATE
    rep = SSD_HEADS // SSD_GROUPS
    for c in range(t // L):
        rs = slice(c * L, (c + 1) * L)
        cs_c = cs[rs]
        cs_t = cs_c.T
        cs_last = cs_c[L - 1:L, :]
        ys = []
        for g in range(SSD_GROUPS):
            bg = xbc[rs, SSD_DI + g * SSD_STATE:SSD_DI + (g + 1) * SSD_STATE]
            cg = xbc[rs, SSD_DI + gn + g * SSD_STATE:SSD_DI + gn + (g + 1) * SSD_STATE]
            gmat = _dot_nt(cg, bg)
            bg_t = bg.T
            for hh in range(rep):
                h = g * rep + hh
                cs_col = cs_c[:, h:h + 1]
                lm = jnp.exp(jnp.where(tril, cs_col - cs_t[h:h + 1, :], NEG))
                xh = xbc[rs, h * SSD_HD:(h + 1) * SSD_HD]
                xdt = xh * dt[rs, h:h + 1]
                st_old = st_ref[h]
                y_h = _dot(gmat * lm, xdt) + _dot(cg, st_old) * jnp.exp(cs_col)
                dec = jnp.exp(cs_last[:, h:h + 1] - cs_col)
                st_ref[h] = jnp.exp(cs_last[:, h:h + 1]) * st_old + _dot(bg_t, xdt * dec)
                ys.append(y_h)
        y = jnp.concatenate(ys, axis=-1) + xbc[rs, 0:SSD_DI] * d_ref[...]
        zc = z[rs]
        y = y * (zc * _sigmoid(zc))
        y_ref[rs, :] = _rms(y, SSD_DI) * ng_ref[...]


def _ssd(ud, cw, cb, dtb, a_neg, d_vec, ng, B, S):
    t = 512
    nb = S // t
    full = lambda a: pl.BlockSpec(a.shape, lambda b, i: (0,) * a.ndim)
    return pl.pallas_call(
        functools.partial(_ssd_kernel, t=t),
        grid=(B, nb),
        in_specs=[pl.BlockSpec((t, SSD_IN_W), lambda b, i: (b * nb + i, 0)),
                  full(cw), full(cb), full(dtb), full(a_neg), full(d_vec), full(ng)],
        out_specs=pl.BlockSpec((t, SSD_DI), lambda b, i: (b * nb + i, 0)),
        out_shape=jax.ShapeDtypeStruct((B * S, SSD_DI), F32),
        scratch_shapes=[pltpu.VMEM((t + 8, SSD_XBC), F32), pltpu.VMEM((SSD_HEADS, SSD_STATE, SSD_HD), F32)],
        compiler_params=_cparams(("parallel", "arbitrary")),
        name="ssd",
    )(ud, cw, cb, dtb, a_neg, d_vec, ng)


def _outproj_kernel(*refs, with_router):
    if with_router:
        (ya_ref, oc_ref, os_ref, ow_ref, gate_ref, yc_ref, yd_ref, res_ref, gn_ref, ex_ref, w_ref, nf_ref,
         rw_ref, hres_ref, hn_ref, rg_ref) = refs
    else:
        (ya_ref, oc_ref, os_ref, ow_ref, gate_ref, yc_ref, yd_ref, res_ref, gn_ref, ex_ref, w_ref, nf_ref,
         hres_ref, hn_ref) = refs
    sg = _sigmoid(gate_ref[...])
    yb = (_dot_f32(sg, ex_ref[0]) * oc_ref[...] + _dot_f32(sg, ex_ref[1]) * os_ref[...]
          + _dot_f32(sg, ex_ref[2]) * ow_ref[...])
    y = jnp.concatenate([_rms(ya_ref[...], GROUP_W) * gn_ref[0:1, :],
                         _rms(yb, GROUP_W) * gn_ref[1:2, :],
                         _rms(yc_ref[...], GROUP_W) * gn_ref[2:3, :],
                         yd_ref[...]], axis=-1)
    hres = res_ref[...] + _dot(y, w_ref[...])
    hres_ref[...] = hres
    hn = _rms(hres, D_MODEL) * nf_ref[...]
    hn_ref[...] = hn.astype(BF16)
    if with_router:
        logits = _dot_f32(hn, rw_ref[...])
        lane = lax.broadcasted_iota(jnp.int32, logits.shape, 1)
        lf = lane.astype(F32)
        logits = jnp.where(lane < N_EXPERTS, logits, NEG)
        m1 = jnp.max(logits, axis=-1, keepdims=True)
        i1 = jnp.min(jnp.where(logits == m1, lf, 128.0), axis=-1, keepdims=True)
        rest = jnp.where(lf == i1, NEG, logits)
        m2 = jnp.max(rest, axis=-1, keepdims=True)
        i2 = jnp.min(jnp.where(rest == m2, lf, 128.0), axis=-1, keepdims=True)
        e2 = jnp.exp(m2 - m1)
        den = 1.0 + e2
        rg_ref[...] = jnp.where(lf == i1, 1.0 / den, 0.0) + jnp.where(lf == i2, e2 / den, 0.0)


def _outproj(ya, oc, osel, ow, gate, yc, yd, res, gn, ex, w, nf, rw):
    T = ya.shape[0]
    tm = 512
    with_router = rw is not None
    row = lambda wd: pl.BlockSpec((tm, wd), lambda i: (i, 0))
    full = lambda a: pl.BlockSpec(a.shape, lambda i: (0,) * a.ndim)
    ins = [ya, oc, osel, ow, gate, yc, yd, res, gn, ex, w, nf]
    in_specs = [row(256), row(256), row(256), row(256), row(128), row(256), row(256), row(D_MODEL),
                full(gn), full(ex), full(w), full(nf)]
    out_specs = [row(D_MODEL), row(D_MODEL)]
    out_shape = [jax.ShapeDtypeStruct((T, D_MODEL), F32), jax.ShapeDtypeStruct((T, D_MODEL), BF16)]
    if with_router:
        ins.append(rw)
        in_specs.append(full(rw))
        out_specs.append(row(128))
        out_shape.append(jax.ShapeDtypeStruct((T, 128), F32))
    return pl.pallas_call(
        functools.partial(_outproj_kernel, with_router=with_router),
        grid=(T // tm,),
        in_specs=in_specs,
        out_specs=out_specs,
        out_shape=out_shape,
        compiler_params=_cparams(("parallel",)),
        name="outproj",
    )(*ins)


def _ffn_kernel(h_ref, res_ref, wg_ref, wu_ref, wd_ref, o_ref, acc_ref):
    j = pl.program_id(1)

    @pl.when(j == 0)
    def _init():
        acc_ref[...] = jnp.zeros(acc_ref.shape, F32)

    h = h_ref[...]
    g = jnp.dot(h, wg_ref[...], preferred_element_type=F32)
    u = jnp.dot(h, wu_ref[...], preferred_element_type=F32)
    acc_ref[...] += _dot(g * _sigmoid(g) * u, wd_ref[...])

    @pl.when(j == pl.num_programs(1) - 1)
    def _fin():
        o_ref[...] = res_ref[...] + acc_ref[...]


def _ffn(hn, res, wg, wu, wd):
    T = hn.shape[0]
    tm, tf = 512, 512
    return pl.pallas_call(
        _ffn_kernel,
        grid=(T // tm, D_FF // tf),
        in_specs=[pl.BlockSpec((tm, D_MODEL), lambda i, j: (i, 0)),
                  pl.BlockSpec((tm, D_MODEL), lambda i, j: (i, 0)),
                  pl.BlockSpec((D_MODEL, tf), lambda i, j: (0, j)),
                  pl.BlockSpec((D_MODEL, tf), lambda i, j: (0, j)),
                  pl.BlockSpec((tf, D_MODEL), lambda i, j: (j, 0))],
        out_specs=pl.BlockSpec((tm, D_MODEL), lambda i, j: (i, 0)),
        out_shape=jax.ShapeDtypeStruct((T, D_MODEL), F32),
        scratch_shapes=[pltpu.VMEM((tm, D_MODEL), F32)],
        compiler_params=_cparams(("parallel", "arbitrary")),
        name="ffn_dense",
    )(hn, res, wg, wu, wd)


def _moe_kernel(h_ref, res_ref, rg_ref, wg_ref, wu_ref, wd_ref, o_ref, acc_ref):
    e = pl.program_id(1)
    j = pl.program_id(2)

    @pl.when((e == 0) & (j == 0))
    def _init():
        acc_ref[...] = jnp.zeros(acc_ref.shape, F32)

    h = h_ref[...]
    rg = rg_ref[...]
    lane = lax.broadcasted_iota(jnp.int32, rg.shape, 1)
    ge = jnp.sum(jnp.where(lane == e, rg, 0.0), axis=-1, keepdims=True)
    g = jnp.dot(h, wg_ref[...], preferred_element_type=F32)
    u = jnp.dot(h, wu_ref[...], preferred_element_type=F32)
    acc_ref[...] += _dot(ge * (g * _sigmoid(g) * u), wd_ref[...])

    @pl.when((e == pl.num_programs(1) - 1) & (j == pl.num_programs(2) - 1))
    def _fin():
        o_ref[...] = res_ref[...] + acc_ref[...]


def _moe(hn, res, rg, wg, wu, wd):
    T = hn.shape[0]
    tm, tf = 512, 512
    return pl.pallas_call(
        _moe_kernel,
        grid=(T // tm, N_EXPERTS, D_FF // tf),
        in_specs=[pl.BlockSpec((tm, D_MODEL), lambda i, e, j: (i, 0)),
                  pl.BlockSpec((tm, D_MODEL), lambda i, e, j: (i, 0)),
                  pl.BlockSpec((tm, 128), lambda i, e, j: (i, 0)),
                  pl.BlockSpec((None, D_MODEL, tf), lambda i, e, j: (e, 0, j)),
                  pl.BlockSpec((None, D_MODEL, tf), lambda i, e, j: (e, 0, j)),
                  pl.BlockSpec((None, tf, D_MODEL), lambda i, e, j: (e, j, 0))],
        out_specs=pl.BlockSpec((tm, D_MODEL), lambda i, e, j: (i, 0)),
        out_shape=jax.ShapeDtypeStruct((T, D_MODEL), F32),
        scratch_shapes=[pltpu.VMEM((tm, D_MODEL), F32)],
        compiler_params=_cparams(("parallel", "arbitrary", "arbitrary")),
        name="ffn_moe",
    )(hn, res, rg, wg, wu, wd)


def _final_norm_kernel(x_ref, g_ref, o_ref):
    o_ref[...] = _rms(x_ref[...], D_MODEL) * g_ref[...]


def _final_norm(x2d, g):
    T = x2d.shape[0]
    tm = 1024
    return pl.pallas_call(
        _final_norm_kernel,
        grid=(T // tm,),
        in_specs=[pl.BlockSpec((tm, D_MODEL), lambda i: (i, 0)), pl.BlockSpec(g.shape, lambda i: (0, 0))],
        out_specs=pl.BlockSpec((tm, D_MODEL), lambda i: (i, 0)),
        out_shape=jax.ShapeDtypeStruct((T, D_MODEL), F32),
        compiler_params=_cparams(("parallel",)),
        name="final_norm",
    )(x2d, g)


def _rot_cols(w, half):
    return jnp.concatenate([-w[:, half:2 * half], w[:, 0:half]], axis=1)


def _nsa_rot64(w64):
    half = NSA_ROT // 2
    return jnp.concatenate([_rot_cols(w64[:, 0:NSA_ROT], half), jnp.zeros((w64.shape[0], NSA_HD - NSA_ROT), F32)],
                           axis=1)


def _ext_w_in(w):
    d = w.shape[0]
    zc = lambda n: jnp.zeros((d, n), F32)
    o1 = MLA_Q_RANK + MLA_KV_RANK + MLA_ROPE
    q0 = o1
    kv0 = o1 + NSA_HEADS * NSA_HD
    g0 = kv0 + 6 * NSA_HD
    o2 = g0 + 3 * NSA_HEADS
    o3 = o2 + 2 * LRU_W
    kv = lambda i: w[:, kv0 + i * NSA_HD:kv0 + (i + 1) * NSA_HD]
    kpe = w[:, MLA_Q_RANK + MLA_KV_RANK:o1]
    rope = [zc(64), kpe, zc(32),
            w[:, q0:kv0],
            kv(0), kv(1),
            kv(2), zc(64),
            kv(4), zc(64)]
    rot = [zc(64), _rot_cols(kpe, MLA_ROPE // 2), zc(32)]
    rot += [_nsa_rot64(w[:, q0 + h * NSA_HD:q0 + (h + 1) * NSA_HD]) for h in range(NSA_HEADS)]
    rot += [_nsa_rot64(kv(0)), zc(64), _nsa_rot64(kv(2)), zc(64), _nsa_rot64(kv(4)), zc(64)]
    rest = [w[:, 0:MLA_Q_RANK], zc(64),
            w[:, MLA_Q_RANK:MLA_Q_RANK + MLA_KV_RANK],
            kv(3), kv(5),
            w[:, g0:o2], zc(128 - 3 * NSA_HEADS),
            w[:, o2:o3],
            w[:, o3:o3 + SSD_DI + SSD_XBC],
            w[:, o3 + SSD_DI + SSD_XBC:], zc(128 - SSD_HEADS)]
    return jnp.concatenate(rope + rot + rest, axis=1).astype(BF16)


def _rope_tables(positions):
    pos = positions.astype(F32)[:, None]
    S = positions.shape[0]

    def cs(rot_dim):
        inv = ROPE_THETA ** (-jnp.arange(0, rot_dim, 2, dtype=F32) / rot_dim)
        ang = pos * inv[None, :]
        return jnp.cos(ang), jnp.sin(ang)

    cm, sm = cs(MLA_ROPE)
    cn, sn = cs(NSA_ROT)
    one = lambda n: jnp.ones((S, n), F32)
    zero = lambda n: jnp.zeros((S, n), F32)
    cos_m = jnp.concatenate([one(64), cm, cm, one(32)], axis=1)
    sin_m = jnp.concatenate([zero(64), sm, sm, zero(32)], axis=1)
    c64 = jnp.concatenate([cn, cn, one(NSA_HD - NSA_ROT)], axis=1)
    s64 = jnp.concatenate([sn, sn, zero(NSA_HD - NSA_ROT)], axis=1)
    cos_t = jnp.concatenate([cos_m] + [c64] * 4 + [c64, one(64)] * 3, axis=1)
    sin_t = jnp.concatenate([sin_m] + [s64] * 4 + [s64, zero(64)] * 3, axis=1)
    return cos_t, sin_t, cos_m, sin_m


def _mla_weights(w_uq, w_ukv, q_norm, kv_norm):
    hw = MLA_NOPE + MLA_ROPE
    pad_r = lambda m: jnp.concatenate([m, jnp.zeros((256 - MLA_Q_RANK, m.shape[1]), F32)], axis=0)
    z = lambda n: jnp.zeros((MLA_Q_RANK, n), F32)
    zk = lambda n: jnp.zeros((MLA_KV_RANK, n), F32)
    w1, w2, wk, wv = [], [], [], []
    for h in range(MLA_HEADS):
        qh = w_uq[:, h * hw:(h + 1) * hw]
        w1 += [qh, z(128 - hw)]
        w2 += [z(MLA_NOPE), _rot_cols(qh[:, MLA_NOPE:], MLA_ROPE // 2), z(128 - hw)]
        kvh = w_ukv[:, h * (MLA_NOPE + MLA_V):(h + 1) * (MLA_NOPE + MLA_V)]
        wk += [kvh[:, :MLA_NOPE], zk(128 - MLA_NOPE)]
        wv += [kvh[:, MLA_NOPE:]]
    qg = jnp.concatenate([q_norm, jnp.zeros((256 - MLA_Q_RANK,), F32)])[None, :]
    return (qg, pad_r(jnp.concatenate(w1, axis=1)).astype(BF16), pad_r(jnp.concatenate(w2, axis=1)).astype(BF16),
            kv_norm[None, :], jnp.concatenate(wk, axis=1).astype(BF16), jnp.concatenate(wv, axis=1).astype(BF16))


def _compress_weights(pe, w_cmp):
    half = CMP_LEN // 2
    wk = w_cmp[0].reshape(CMP_LEN, NSA_HD, NSA_HD)
    wv = w_cmp[1].reshape(CMP_LEN, NSA_HD, NSA_HD)
    z = jnp.zeros((half, NSA_HD, NSA_HD), F32)

    def interleave(ks, vs):
        top = jnp.concatenate([ks, z], axis=-1)
        bot = jnp.concatenate([z, vs], axis=-1)
        return jnp.concatenate([top, bot], axis=1).reshape(half * 2 * NSA_HD, 2 * NSA_HD)

    pad8 = lambda p: jnp.concatenate([p.reshape(1, -1), jnp.zeros((7, CMP_LEN * NSA_HD), F32)], axis=0)
    return (interleave(wk[:half], wv[:half]), interleave(wk[half:], wv[half:]),
            pad8(pe[0]), pad8(pe[1]), w_cmp[0], w_cmp[1])


def _overlap_matrix(S):
    nc_pad = S // CMP_STRIDE
    n = np.arange(nc_pad)[:, None]
    j = np.arange(SEL_LANES)[None, :]
    ov = ((n * CMP_STRIDE <= j * SEL_LEN + SEL_LEN - 1) & (n * CMP_STRIDE + CMP_LEN - 1 >= j * SEL_LEN)
          & (n < nc_pad - 1) & (j < S // SEL_LEN))
    return jnp.asarray(ov.astype(np.float32))


def _gate_expand():
    ex = np.zeros((3, 128, NSA_HEADS * NSA_HD), np.float32)
    for br in range(3):
        for h in range(NSA_HEADS):
            ex[br, h * 3 + br, h * NSA_HD:(h + 1) * NSA_HD] = 1.0
    return jnp.asarray(ex)


def _lru_gate_weights(w_gate, b_gate):
    wg = jnp.zeros((LRU_W, 2 * LRU_W), F32)
    bw = LRU_W // LRU_BLOCKS
    for g in range(2):
        for n in range(LRU_BLOCKS):
            wg = wg.at[n * bw:(n + 1) * bw, g * LRU_W + n * bw:g * LRU_W + (n + 1) * bw].set(w_gate[g, n])
    return wg.astype(BF16), b_gate.reshape(1, 2 * LRU_W)


def _pad_lanes(v, n=128):
    return jnp.concatenate([v, jnp.zeros((n - v.shape[0],), F32)])[None, :]


def kernel(x, positions, norm_mix, w_in, mla_q_norm, mla_w_uq, mla_kv_norm, mla_w_ukv, nsa_cmp_pe, nsa_w_cmp,
           lru_conv_w, lru_conv_b, lru_w_gate, lru_b_gate, lru_lambda, ssd_conv_w, ssd_conv_b, ssd_dt_bias,
           ssd_a_log, ssd_d, group_norm, w_out, norm_ffn, ffn_w_gate, ffn_w_up, ffn_w_down, moe_router,
           moe_w_gate, moe_w_up, moe_w_down, norm_final):
    B, S, D = x.shape
    T = B * S
    depth = w_in.shape[0]
    assert S // SEL_LEN <= SEL_LANES and S % 512 == 0
    cos_t, sin_t, cos_m, sin_m = _rope_tables(positions)
    ov = _overlap_matrix(S)
    ex = _gate_expand()
    h_res = x.reshape(T, D)
    for l in range(depth):
        ua, qn, kvc, ksel, kwin, vsw, gate, uc, ud = _inproj(
            h_res, norm_mix[l][None, :], _ext_w_in(w_in[l]), cos_t, sin_t, S)

        q_m, k_m, v_m = _mla_prep(ua, *_mla_weights(mla_w_uq[l], mla_w_ukv[l], mla_q_norm[l], mla_kv_norm[l]),
                                  cos_m, sin_m, B, S)
        y_a = _flash(q_m, k_m, v_m, window=None, out_lo=0, out_w=MLA_V).reshape(T, GROUP_W)

        kvcmp = _compress(kvc.reshape(B, S // CMP_STRIDE, CMP_STRIDE * 128),
                          *_compress_weights(nsa_cmp_pe[l], nsa_w_cmp[l]))
        o_c, q_aug = _cmp_select(qn, kvcmp, ov, B, S)
        v4 = vsw.reshape(B, 1, S, 128)
        o_s = _flash(q_aug, ksel.reshape(B, 1, S, 256), v4, window=None, out_lo=0, out_w=NSA_HD).reshape(T, GROUP_W)
        o_w = _flash(q_aug, kwin.reshape(B, 1, S, 256), v4, window=WINDOW, out_lo=NSA_HD,
                     out_w=NSA_HD).reshape(T, GROUP_W)

        wg_l, bg_l = _lru_gate_weights(lru_w_gate[l], lru_b_gate[l])
        y_c = _lru(uc, lru_conv_w[l], lru_conv_b[l][None, :], wg_l, bg_l,
                   jax.nn.softplus(-lru_lambda[l])[None, :], B, S)

        y_d = _ssd(ud, ssd_conv_w[l], ssd_conv_b[l][None, :], _pad_lanes(ssd_dt_bias[l]),
                   _pad_lanes(-jnp.exp(ssd_a_log[l])), jnp.repeat(ssd_d[l], SSD_HD)[None, :],
                   group_norm[l, 3][None, :], B, S)

        moe_layer = l % 2 == 1
        rw = None
        if moe_layer:
            rw = jnp.concatenate([moe_router[l // 2], jnp.zeros((D, 128 - N_EXPERTS), F32)], axis=1)
        outs = _outproj(y_a, o_c, o_s, o_w, gate, y_c, y_d, h_res, group_norm[l], ex, w_out[l].astype(BF16),
                        norm_ffn[l][None, :], rw)
        if moe_layer:
            h_res, hn, rg = outs
            h_res = _moe(hn, h_res, rg, moe_w_gate[l // 2].astype(BF16), moe_w_up[l // 2].astype(BF16),
                         moe_w_down[l // 2].astype(BF16))
        else:
            h_res, hn = outs
            h_res = _ffn(hn, h_res, ffn_w_gate[l // 2].astype(BF16), ffn_w_up[l // 2].astype(BF16),
                         ffn_w_down[l // 2].astype(BF16))
    return _final_norm(h_res, norm_final[None, :]).reshape(B, S, D).astype(x.dtype)
```

```python
import functools

import numpy as np
import jax
import jax.numpy as jnp
from jax import lax
from jax.experimental import pallas as pl
from jax.experimental.pallas import tpu as pltpu

F32 = jnp.float32
BF16 = jnp.bfloat16

D_MODEL = 1024
GROUP_W = 256
ROPE_THETA = 500000.0
NORM_EPS = 1e-6
NEG = -1e30
FORCE = 1e4
MLA_HEADS, MLA_NOPE, MLA_ROPE, MLA_V = 4, 64, 32, 64
MLA_Q_RANK, MLA_KV_RANK = 192, 128
NSA_HEADS, NSA_HD, NSA_ROT = 4, 64, 16
CMP_STRIDE, CMP_LEN, SEL_LEN, SEL_TOPN, WINDOW = 16, 32, 64, 16, 512
LRU_W, LRU_BLOCKS, LRU_C, CONV_W = 256, 4, 8.0, 4
SSD_HEADS, SSD_HD, SSD_GROUPS, SSD_STATE, SSD_CHUNK = 4, 64, 2, 128, 128
SSD_DI = 256
SSD_XBC = SSD_DI + 2 * SSD_GROUPS * SSD_STATE
D_FF = 3584
N_EXPERTS = 8

ROPE_W = 768
C_ROPE, C_ROT = 0, 768
C_CQ, C_CKV, C_VSW, C_GATE, C_LRU, C_SSD = 1536, 1792, 1920, 2176, 2304, 2816
SSD_IN_W = 1152
N_EXT = C_SSD + SSD_IN_W
LOG2E = 1.4426950408889634
SEL_LANES = 128

VMEM_LIMIT = 56 * 1024 * 1024


def _cparams(sem):
    return pltpu.CompilerParams(dimension_semantics=sem, vmem_limit_bytes=VMEM_LIMIT)


def _dot(a, b):
    return jnp.dot(a.astype(BF16), b.astype(BF16), preferred_element_type=F32)


def _dot_nt(a, b):
    return lax.dot_general(a.astype(BF16), b.astype(BF16), (((1,), (1,)), ((), ())),
                           preferred_element_type=F32)


def _dot_f32(a, b):
    return jnp.dot(a, b, precision=lax.Precision.HIGHEST, preferred_element_type=F32)


def _dot_f32_nt(a, b):
    return lax.dot_general(a, b, (((1,), (1,)), ((), ())), precision=lax.Precision.HIGHEST,
                           preferred_element_type=F32)


def _sigmoid(x):
    return 1.0 / (1.0 + jnp.exp(-x))


def _softplus(x):
    return jnp.maximum(x, 0.0) + jnp.log(1.0 + jnp.exp(-jnp.abs(x)))


def _rms(x, width):
    return x * lax.rsqrt(jnp.sum(x * x, axis=-1, keepdims=True) * (1.0 / width) + NORM_EPS)


def _inproj_kernel(x_ref, g_ref, w_ref, cos_ref, sin_ref,
                   ua_ref, qn_ref, kvc_ref, ksel_ref, kwin_ref, vsel_ref, vwin_ref, gate_ref, uc_ref, ud_ref,
                   *, tm, seq_blocks):
    x = x_ref[...]
    h = _rms(x, D_MODEL) * g_ref[...]
    y = _dot(h, w_ref[...])
    roped = y[:, C_ROPE:C_ROPE + ROPE_W] * cos_ref[...] + y[:, C_ROT:C_ROT + ROPE_W] * sin_ref[...]
    ua_ref[:, 0:384] = y[:, C_CQ:C_CQ + 384]
    ua_ref[:, 384:512] = roped[:, 0:128]
    qn_ref[...] = roped[:, 128:384]
    kvc_ref[...] = roped[:, 384:512]
    s0 = (pl.program_id(0) % seq_blocks) * tm
    pos = s0 + lax.broadcasted_iota(jnp.int32, (tm, SEL_LANES), 0)
    blk = lax.broadcasted_iota(jnp.int32, (tm, SEL_LANES), 1)
    onehot = jnp.where(jnp.right_shift(pos, 6) == blk, 1.0, 0.0)
    ksel_ref[:, 0:128] = roped[:, 512:640].astype(BF16)
    ksel_ref[:, 128:256] = onehot.astype(BF16)
    kwin_ref[:, 0:128] = roped[:, 640:768].astype(BF16)
    kwin_ref[:, 128:256] = jnp.zeros((tm, 128), BF16)
    ones_hi = jnp.where(blk >= 64, 1.0, 0.0)
    vsel_ref[...] = (y[:, C_VSW:C_VSW + 128] + ones_hi).astype(BF16)
    vwin_ref[...] = (y[:, C_VSW + 128:C_VSW + 256] + ones_hi).astype(BF16)
    gate_ref[...] = y[:, C_GATE:C_GATE + 128]
    uc_ref[...] = y[:, C_LRU:C_LRU + 512]
    ud_ref[...] = y[:, C_SSD:C_SSD + SSD_IN_W]


def _inproj(x2d, g, w_ext, cos_t, sin_t, seq):
    T = x2d.shape[0]
    tm = 256
    seq_blocks = seq // tm
    row = lambda w: pl.BlockSpec((tm, w), lambda i: (i, 0))
    full = lambda a: pl.BlockSpec(a.shape, lambda i: (0,) * a.ndim)
    tab = pl.BlockSpec((tm, ROPE_W), lambda i: (i % seq_blocks, 0))
    outs = [(512, F32), (256, F32), (128, F32), (256, BF16), (256, BF16), (128, BF16), (128, BF16), (128, F32),
            (512, F32), (SSD_IN_W, F32)]
    return pl.pallas_call(
        functools.partial(_inproj_kernel, tm=tm, seq_blocks=seq_blocks),
        grid=(T // tm,),
        in_specs=[row(D_MODEL), full(g), full(w_ext), tab, tab],
        out_specs=[row(w) for w, _ in outs],
        out_shape=[jax.ShapeDtypeStruct((T, w), dt) for w, dt in outs],
        compiler_params=_cparams(("parallel",)),
        name="inproj",
    )(x2d, g, w_ext, cos_t, sin_t)


def _mla_prep_kernel(ua_ref, qg_ref, w1_ref, w2_ref, kvg_ref, wk_ref, wv_ref, cos_ref, sin_ref,
                     q_ref, k_ref, v_ref):
    ua = ua_ref[...]
    cq = _rms(ua[:, 0:256], MLA_Q_RANK) * qg_ref[...]
    y1 = _dot(cq, w1_ref[...])
    y2 = _dot(cq, w2_ref[...])
    ckv = _rms(ua[:, 256:384], MLA_KV_RANK) * kvg_ref[...]
    kn = _dot(ckv, wk_ref[...])
    vv = _dot(ckv, wv_ref[...])
    kpe = ua[:, 384:512]
    cos = cos_ref[...]
    sin = sin_ref[...]
    scale = LOG2E * (MLA_NOPE + MLA_ROPE) ** -0.5
    ones_hi = jnp.where(lax.broadcasted_iota(jnp.int32, cos.shape, 1) >= 64, 1.0, 0.0)
    for h in range(MLA_HEADS):
        sl = slice(h * 128, (h + 1) * 128)
        q_ref[h] = ((y1[:, sl] * cos + y2[:, sl] * sin) * scale).astype(BF16)
        k_ref[h] = (kn[:, sl] + kpe).astype(BF16)
        v_ref[h] = (vv[:, sl] + ones_hi).astype(BF16)


def _mla_prep(ua, qg, w1, w2, kvg, wk, wv, cos_m, sin_m, B, S):
    tm = 512
    nb = S // tm
    full = lambda a: pl.BlockSpec(a.shape, lambda b, i: (0,) * a.ndim)
    tab = pl.BlockSpec((tm, 128), lambda b, i: (i, 0))
    hd = lambda w: pl.BlockSpec((None, MLA_HEADS, tm, w), lambda b, i: (b, 0, i, 0))
    return pl.pallas_call(
        _mla_prep_kernel,
        grid=(B, nb),
        in_specs=[pl.BlockSpec((tm, 512), lambda b, i: (b * nb + i, 0)),
                  full(qg), full(w1), full(w2), full(kvg), full(wk), full(wv), tab, tab],
        out_specs=[hd(128), hd(128), hd(128)],
        out_shape=[jax.ShapeDtypeStruct((B, MLA_HEADS, S, 128), BF16)] * 3,
        compiler_params=_cparams(("parallel", "parallel")),
        name="mla_prep",
    )(ua, qg, w1, w2, kvg, wk, wv, cos_m, sin_m)


def _flash_kernel(qi_ref, ki_ref, flag_ref, q_ref, k_ref, v_ref, o_ref, m_ref, acc_ref,
                  *, heads, kv_heads, t, window):
    p_idx = pl.program_id(1)
    qi = qi_ref[p_idx]
    ki = ki_ref[p_idx]
    flags = flag_ref[p_idx]

    @pl.when(jnp.bitwise_and(flags, 1) != 0)
    def _init():
        m_ref[...] = jnp.full(m_ref.shape, NEG, F32)
        acc_ref[...] = jnp.zeros(acc_ref.shape, F32)

    def step(masked):
        if masked:
            qpos = qi * t + lax.broadcasted_iota(jnp.int32, (t, t), 0)
            kpos = ki * t + lax.broadcasted_iota(jnp.int32, (t, t), 1)
            mask = kpos <= qpos
            if window is not None:
                mask = mask & (kpos > qpos - window)
        for h in range(heads):
            hk = h if kv_heads > 1 else 0
            s = lax.dot_general(q_ref[h], k_ref[hk], (((1,), (1,)), ((), ())), preferred_element_type=F32)
            if masked:
                s = jnp.where(mask, s, NEG)
            m_old = m_ref[h]
            m_new = jnp.maximum(m_old, jnp.max(s, axis=-1, keepdims=True))
            alpha = jnp.exp2(m_old - m_new)
            p = jnp.concatenate([jnp.exp2(s[:, c * 128:(c + 1) * 128] - m_new) for c in range(t // 128)],
                                axis=1).astype(BF16)
            acc_ref[h] = alpha * acc_ref[h] + jnp.dot(p, v_ref[hk], preferred_element_type=F32)
            m_ref[h] = m_new

    @pl.when(jnp.bitwise_and(flags, 4) != 0)
    def _masked():
        step(True)

    @pl.when(jnp.bitwise_and(flags, 4) == 0)
    def _plain():
        step(False)

    @pl.when(jnp.bitwise_and(flags, 2) != 0)
    def _fin():
        for h in range(heads):
            acc = acc_ref[h]
            o_ref[:, h * 64:(h + 1) * 64] = acc[:, 0:64] / acc[:, 64:128]


def _pair_tables(nq, back):
    qi, ki, flags = [], [], []
    for i in range(nq):
        lo = 0 if back is None else max(0, i - back)
        for j in range(lo, i + 1):
            qi.append(i)
            ki.append(j)
            masked = (j == i) or (back is not None)
            flags.append((1 if j == lo else 0) | (2 if j == i else 0) | (4 if masked else 0))
    mk = lambda a: jnp.asarray(np.asarray(a, np.int32))
    return mk(qi), mk(ki), mk(flags)


def _flash(q, k, v, *, window, t=512):
    B, H, S, dk = q.shape
    Hk = k.shape[1]
    t = min(t, S)
    nq = S // t
    back = None if window is None else -(-window // t)
    tabs = _pair_tables(nq, back)
    npairs = int(tabs[0].shape[0])
    grid_spec = pltpu.PrefetchScalarGridSpec(
        num_scalar_prefetch=3,
        grid=(B, npairs),
        in_specs=[pl.BlockSpec((None, H, t, dk), lambda b, p, qi, ki, fl: (b, 0, qi[p], 0)),
                  pl.BlockSpec((None, Hk, t, dk), lambda b, p, qi, ki, fl: (b, 0, ki[p], 0)),
                  pl.BlockSpec((None, Hk, t, 128), lambda b, p, qi, ki, fl: (b, 0, ki[p], 0))],
        out_specs=pl.BlockSpec((None, t, H * 64), lambda b, p, qi, ki, fl: (b, qi[p], 0)),
        scratch_shapes=[pltpu.VMEM((H, t, 128), F32), pltpu.VMEM((H, t, 128), F32)],
    )
    return pl.pallas_call(
        functools.partial(_flash_kernel, heads=H, kv_heads=Hk, t=t, window=window),
        grid_spec=grid_spec,
        out_shape=jax.ShapeDtypeStruct((B, S, H * 64), F32),
        compiler_params=_cparams(("parallel", "arbitrary")),
        name="flash_attn",
    )(*tabs, q, k, v)


def _compress_kernel(x_ref, wa_ref, wb_ref, pek_ref, pev_ref, wkf_ref, wvf_ref, o_ref, *, nc_pad):
    x = x_ref[...]
    a = _dot_f32(x, wa_ref[...])
    b = _dot_f32(x, wb_ref[...])
    ck = _dot_f32(pek_ref[...], wkf_ref[...])[0:1]
    cv = _dot_f32(pev_ref[...], wvf_ref[...])[0:1]
    const = jnp.concatenate([ck, cv], axis=-1)
    b_next = pltpu.roll(b, nc_pad - 1, axis=0)
    row = lax.broadcasted_iota(jnp.int32, (nc_pad, 128), 0)
    o_ref[...] = jnp.where(row < nc_pad - 1, a + b_next + const, 0.0)


def _compress(x, wa, wb, pek, pev, wkf, wvf):
    B, nc_pad, w = x.shape
    full = lambda a: pl.BlockSpec(a.shape, lambda b: (0,) * a.ndim)
    return pl.pallas_call(
        functools.partial(_compress_kernel, nc_pad=nc_pad),
        grid=(B,),
        in_specs=[pl.BlockSpec((None, nc_pad, w), lambda b: (b, 0, 0)),
                  full(wa), full(wb), full(pek), full(pev), full(wkf), full(wvf)],
        out_specs=pl.BlockSpec((None, nc_pad, 128), lambda b: (b, 0, 0)),
        out_shape=jax.ShapeDtypeStruct((B, nc_pad, 128), F32),
        compiler_params=_cparams(("parallel",)),
        name="nsa_compress",
    )(x, wa, wb, pek, pev, wkf, wvf)


def _cmp_select_kernel(q_ref, kvc_ref, ov_ref, oc_ref, qaug_ref, *, tq, nc_pad, n_top):
    i = pl.program_id(1)
    q = q_ref[...]
    kc = kvc_ref[:, 0:64]
    vc = kvc_ref[:, 64:128]
    scale = NSA_HD ** -0.5
    qpos = i * tq + lax.broadcasted_iota(jnp.int32, (tq, 1), 0)
    n_idx = lax.broadcasted_iota(jnp.int32, (1, nc_pad), 1)
    m_c = (n_idx * CMP_STRIDE + (CMP_LEN - 1) <= qpos) & (n_idx < nc_pad - 1)
    psum = jnp.zeros((tq, nc_pad), F32)
    for h in range(NSA_HEADS):
        qh = q[:, h * NSA_HD:(h + 1) * NSA_HD]
        s = jnp.where(m_c, _dot_f32_nt(qh, kc) * scale, NEG)
        e = jnp.where(m_c, jnp.exp(s - jnp.max(s, axis=-1, keepdims=True)), 0.0)
        den = jnp.sum(e, axis=-1, keepdims=True)
        p = e / jnp.where(den > 0.0, den, 1.0)
        oc_ref[:, h * NSA_HD:(h + 1) * NSA_HD] = _dot(p, vc)
        psum = psum + p
    imp = _dot_f32(psum, ov_ref[...])
    cur = jnp.right_shift(qpos, 6)
    jj = lax.broadcasted_iota(jnp.int32, (1, SEL_LANES), 1)
    forced = (jj == 0) | (jj == cur) | (jj == cur - 1)
    imp = jnp.where(jj <= cur, imp + FORCE * jnp.where(forced, 1.0, 0.0), NEG)
    jf = jj.astype(F32)
    bias = jnp.full((tq, SEL_LANES), NEG, F32)
    for _ in range(n_top):
        mx = jnp.max(imp, axis=-1, keepdims=True)
        idx = jnp.min(jnp.where(imp == mx, jf, float(SEL_LANES)), axis=-1, keepdims=True)
        hit = jf == idx
        bias = jnp.where(hit & (mx > 0.5 * NEG), 0.0, bias)
        imp = jnp.where(hit, -3e38, imp)
    bias = bias.astype(BF16)
    qs = (q * (scale * LOG2E)).astype(BF16)
    for h in range(NSA_HEADS):
        qaug_ref[h, :, 0:64] = qs[:, h * NSA_HD:(h + 1) * NSA_HD]
        qaug_ref[h, :, 64:128] = jnp.zeros((tq, 64), BF16)
        qaug_ref[h, :, 128:256] = bias


def _cmp_select(qn, kvcmp, ov, B, S):
    tq = 256
    nb = S // tq
    nc_pad = kvcmp.shape[1]
    n_top = min(SEL_TOPN, S // SEL_LEN)
    return pl.pallas_call(
        functools.partial(_cmp_select_kernel, tq=tq, nc_pad=nc_pad, n_top=n_top),
        grid=(B, nb),
        in_specs=[pl.BlockSpec((tq, 256), lambda b, i: (b * nb + i, 0)),
                  pl.BlockSpec((None, nc_pad, 128), lambda b, i: (b, 0, 0)),
                  pl.BlockSpec(ov.shape, lambda b, i: (0, 0))],
        out_specs=[pl.BlockSpec((tq, 256), lambda b, i: (b * nb + i, 0)),
                   pl.BlockSpec((None, NSA_HEADS, tq, 256), lambda b, i: (b, 0, i, 0))],
        out_shape=[jax.ShapeDtypeStruct((B * S, 256), F32),
                   jax.ShapeDtypeStruct((B, NSA_HEADS, S, 256), BF16)],
        compiler_params=_cparams(("parallel", "parallel")),
        name="nsa_cmp_select",
    )(qn, kvcmp, ov)


def _shift_scan(a, b, t, width):
    row = lax.broadcasted_iota(jnp.int32, (t, width), 0)
    s = 1
    while s < t:
        keep = row >= s
        a_sh = jnp.where(keep, pltpu.roll(a, s, axis=0), 1.0)
        b_sh = jnp.where(keep, pltpu.roll(b, s, axis=0), 0.0)
        b = a * b_sh + b
        a = a * a_sh
        s *= 2
    return a, b


def _causal_conv(xpad_ref, x, cw_ref, cb_ref, t):
    xpad_ref[8:8 + t, :] = x
    y = cb_ref[...] + cw_ref[CONV_W - 1:CONV_W, :] * x
    for k in range(CONV_W - 1):
        off = 8 - (CONV_W - 1) + k
        y = y + cw_ref[k:k + 1, :] * xpad_ref[off:off + t, :]
    xpad_ref[0:8, :] = x[t - 8:t, :]
    return y


def _lru_kernel(u_ref, cw_ref, cb_ref, wg_ref, bg_ref, sp_ref, y_ref, xpad_ref, h_ref, *, t):
    @pl.when(pl.program_id(1) == 0)
    def _init():
        xpad_ref[0:8, :] = jnp.zeros((8, LRU_W), F32)
        h_ref[...] = jnp.zeros(h_ref.shape, F32)

    x = u_ref[:, 0:LRU_W]
    gbr = u_ref[:, LRU_W:2 * LRU_W]
    xb = _causal_conv(xpad_ref, x, cw_ref, cb_ref, t)
    g = _dot(xb, wg_ref[...]) + bg_ref[...]
    r = _sigmoid(g[:, 0:LRU_W])
    ig = _sigmoid(g[:, LRU_W:2 * LRU_W])
    log_a = -LRU_C * r * sp_ref[...]
    a = jnp.exp(log_a)
    one_m = -jnp.tanh(log_a) * (a * a + 1.0)
    b = jnp.sqrt(jnp.maximum(one_m, 0.0)) * (ig * xb)
    a_cum, h = _shift_scan(a, b, t, LRU_W)
    h = h + a_cum * h_ref[0:1, :]
    h_ref[0:1, :] = h[t - 1:t, :]
    gelu = 0.5 * gbr * (1.0 + jnp.tanh(0.7978845608028654 * (gbr + 0.044715 * gbr * gbr * gbr)))
    y_ref[...] = h * gelu


def _lru(uc, cw, cb, wg, bg, sp, B, S):
    t = 512
    nb = S // t
    full = lambda a: pl.BlockSpec(a.shape, lambda b, i: (0,) * a.ndim)
    return pl.pallas_call(
        functools.partial(_lru_kernel, t=t),
        grid=(B, nb),
        in_specs=[pl.BlockSpec((t, 2 * LRU_W), lambda b, i: (b * nb + i, 0)),
                  full(cw), full(cb), full(wg), full(bg), full(sp)],
        out_specs=pl.BlockSpec((t, LRU_W), lambda b, i: (b * nb + i, 0)),
        out_shape=jax.ShapeDtypeStruct((B * S, LRU_W), F32),
        scratch_shapes=[pltpu.VMEM((t + 8, LRU_W), F32), pltpu.VMEM((8, LRU_W), F32)],
        compiler_params=_cparams(("parallel", "arbitrary")),
        name="rglru",
    )(uc, cw, cb, wg, bg, sp)


def _ssd_kernel(u_ref, cw_ref, cb_ref, dtb_ref, a_ref, d_ref, ng_ref, y_ref, xpad_ref, st_ref, *, t):
    @pl.when(pl.program_id(1) == 0)
    def _init():
        xpad_ref[0:8, :] = jnp.zeros((8, SSD_XBC), F32)
        st_ref[...] = jnp.zeros(st_ref.shape, F32)

    L = SSD_CHUNK
    z = u_ref[:, 0:SSD_DI]
    conv = _causal_conv(xpad_ref, u_ref[:, SSD_DI:SSD_DI + SSD_XBC], cw_ref, cb_ref, t)
    xbc = conv * _sigmoid(conv)
    dt = _softplus(u_ref[:, SSD_DI + SSD_XBC:SSD_IN_W] + dtb_ref[...])
    a = dt * a_ref[...]
    row = lax.broadcasted_iota(jnp.int32, (t, 128), 0)
    rin = jnp.bitwise_and(row, L - 1)
    cs = a
    s = 1
    while s < L:
        cs = cs + jnp.where(rin >= s, pltpu.roll(cs, s, axis=0), 0.0)
        s *= 2
    tril = lax.broadcasted_iota(jnp.int32, (L, L), 0) >= lax.broadcasted_iota(jnp.int32, (L, L), 1)
    gn = SSD_GROUPS * SSD_STATE
    rep = SSD_HEADS // SSD_GROUPS
    for c in range(t // L):
        rs = slice(c * L, (c + 1) * L)
        cs_c = cs[rs]
        cs_t = cs_c.T
        cs_last = cs_c[L - 1:L, :]
        ys = []
        for g in range(SSD_GROUPS):
            bg = xbc[rs, SSD_DI + g * SSD_STATE:SSD_DI + (g + 1) * SSD_STATE]
            cg = xbc[rs, SSD_DI + gn + g * SSD_STATE:SSD_DI + gn + (g + 1) * SSD_STATE]
            gmat = _dot_nt(cg, bg)
            bg_t = bg.T
            for hh in range(rep):
                h = g * rep + hh
                cs_col = cs_c[:, h:h + 1]
                lm = jnp.exp(jnp.where(tril, cs_col - cs_t[h:h + 1, :], NEG))
                xh = xbc[rs, h * SSD_HD:(h + 1) * SSD_HD]
                xdt = xh * dt[rs, h:h + 1]
                st_old = st_ref[h]
                y_h = _dot(gmat * lm, xdt) + _dot(cg, st_old) * jnp.exp(cs_col)
                dec = jnp.exp(cs_last[:, h:h + 1] - cs_col)
                st_ref[h] = jnp.exp(cs_last[:, h:h + 1]) * st_old + _dot(bg_t, xdt * dec)
                ys.append(y_h)
        y = jnp.concatenate(ys, axis=-1) + xbc[rs, 0:SSD_DI] * d_ref[...]
        zc = z[rs]
        y = y * (zc * _sigmoid(zc))
        y_ref[rs, :] = _rms(y, SSD_DI) * ng_ref[...]


def _ssd(ud, cw, cb, dtb, a_neg, d_vec, ng, B, S):
    t = 512
    nb = S // t
    full = lambda a: pl.BlockSpec(a.shape, lambda b, i: (0,) * a.ndim)
    return pl.pallas_call(
        functools.partial(_ssd_kernel, t=t),
        grid=(B, nb),
        in_specs=[pl.BlockSpec((t, SSD_IN_W), lambda b, i: (b * nb + i, 0)),
                  full(cw), full(cb), full(dtb), full(a_neg), full(d_vec), full(ng)],
        out_specs=pl.BlockSpec((t, SSD_DI), lambda b, i: (b * nb + i, 0)),
        out_shape=jax.ShapeDtypeStruct((B * S, SSD_DI), F32),
        scratch_shapes=[pltpu.VMEM((t + 8, SSD_XBC), F32), pltpu.VMEM((SSD_HEADS, SSD_STATE, SSD_HD), F32)],
        compiler_params=_cparams(("parallel", "arbitrary")),
        name="ssd",
    )(ud, cw, cb, dtb, a_neg, d_vec, ng)


def _outproj_kernel(*refs, with_router):
    if with_router:
        (ya_ref, oc_ref, os_ref, ow_ref, gate_ref, yc_ref, yd_ref, res_ref, gn_ref, ex_ref, w_ref, nf_ref,
         rw_ref, hres_ref, hn_ref, rg_ref) = refs
    else:
        (ya_ref, oc_ref, os_ref, ow_ref, gate_ref, yc_ref, yd_ref, res_ref, gn_ref, ex_ref, w_ref, nf_ref,
         hres_ref, hn_ref) = refs
    sg = _sigmoid(gate_ref[...])
    yb = (_dot_f32(sg, ex_ref[0]) * oc_ref[...] + _dot_f32(sg, ex_ref[1]) * os_ref[...]
          + _dot_f32(sg, ex_ref[2]) * ow_ref[...])
    y = jnp.concatenate([_rms(ya_ref[...], GROUP_W) * gn_ref[0:1, :],
                         _rms(yb, GROUP_W) * gn_ref[1:2, :],
                         _rms(yc_ref[...], GROUP_W) * gn_ref[2:3, :],
                         yd_ref[...]], axis=-1)
    hres = res_ref[...] + _dot(y, w_ref[...])
    hres_ref[...] = hres
    hn = _rms(hres, D_MODEL) * nf_ref[...]
    hn_ref[...] = hn.astype(hn_ref.dtype)
    if with_router:
        logits = _dot_f32(hn, rw_ref[...])
        lane = lax.broadcasted_iota(jnp.int32, logits.shape, 1)
        lf = lane.astype(F32)
        logits = jnp.where(lane < N_EXPERTS, logits, NEG)
        m1 = jnp.max(logits, axis=-1, keepdims=True)
        i1 = jnp.min(jnp.where(logits == m1, lf, 128.0), axis=-1, keepdims=True)
        rest = jnp.where(lf == i1, NEG, logits)
        m2 = jnp.max(rest, axis=-1, keepdims=True)
        i2 = jnp.min(jnp.where(rest == m2, lf, 128.0), axis=-1, keepdims=True)
        e2 = jnp.exp(m2 - m1)
        den = 1.0 + e2
        rg_ref[...] = jnp.where(lane == 0, i1, jnp.where(lane == 1, i2, jnp.where(
            lane == 2, 1.0 / den, jnp.where(lane == 3, e2 / den, 0.0))))


def _outproj(ya, oc, osel, ow, gate, yc, yd, res, gn, ex, w, nf, rw):
    T = ya.shape[0]
    tm = 512
    with_router = rw is not None
    row = lambda wd: pl.BlockSpec((tm, wd), lambda i: (i, 0))
    full = lambda a: pl.BlockSpec(a.shape, lambda i: (0,) * a.ndim)
    ins = [ya, oc, osel, ow, gate, yc, yd, res, gn, ex, w, nf]
    in_specs = [row(256), row(256), row(256), row(256), row(128), row(256), row(256), row(D_MODEL),
                full(gn), full(ex), full(w), full(nf)]
    out_specs = [row(D_MODEL), row(D_MODEL)]
    out_shape = [jax.ShapeDtypeStruct((T, D_MODEL), F32),
                 jax.ShapeDtypeStruct((T, D_MODEL), F32 if with_router else BF16)]
    if with_router:
        ins.append(rw)
        in_specs.append(full(rw))
        out_specs.append(row(128))
        out_shape.append(jax.ShapeDtypeStruct((T, 128), F32))
    return pl.pallas_call(
        functools.partial(_outproj_kernel, with_router=with_router),
        grid=(T // tm,),
        in_specs=in_specs,
        out_specs=out_specs,
        out_shape=out_shape,
        compiler_params=_cparams(("parallel",)),
        name="outproj",
    )(*ins)


def _ffn_kernel(h_ref, res_ref, wg_ref, wu_ref, wd_ref, o_ref, acc_ref):
    j = pl.program_id(1)

    @pl.when(j == 0)
    def _init():
        acc_ref[...] = jnp.zeros(acc_ref.shape, F32)

    h = h_ref[...]
    g = jnp.dot(h, wg_ref[...], preferred_element_type=F32)
    u = jnp.dot(h, wu_ref[...], preferred_element_type=F32)
    acc_ref[...] += _dot(g * _sigmoid(g) * u, wd_ref[...])

    @pl.when(j == pl.num_programs(1) - 1)
    def _fin():
        o_ref[...] = res_ref[...] + acc_ref[...]


def _ffn(hn, res, wg, wu, wd):
    T = hn.shape[0]
    tm, tf = 512, 512
    return pl.pallas_call(
        _ffn_kernel,
        grid=(T // tm, D_FF // tf),
        in_specs=[pl.BlockSpec((tm, D_MODEL), lambda i, j: (i, 0)),
                  pl.BlockSpec((tm, D_MODEL), lambda i, j: (i, 0)),
                  pl.BlockSpec((D_MODEL, tf), lambda i, j: (0, j)),
                  pl.BlockSpec((D_MODEL, tf), lambda i, j: (0, j)),
                  pl.BlockSpec((tf, D_MODEL), lambda i, j: (j, 0))],
        out_specs=pl.BlockSpec((tm, D_MODEL), lambda i, j: (i, 0)),
        out_shape=jax.ShapeDtypeStruct((T, D_MODEL), F32),
        scratch_shapes=[pltpu.VMEM((tm, D_MODEL), F32)],
        compiler_params=_cparams(("parallel", "arbitrary")),
        name="ffn_dense",
    )(hn, res, wg, wu, wd)


MOE_TILE = 1024
MOE_TM = 512


def _route_rank_kernel(route_ref, rank_ref, cnt_ref, carry_ref, *, tm):
    @pl.when(pl.program_id(0) == 0)
    def _init():
        carry_ref[...] = jnp.zeros(carry_ref.shape, F32)

    r = route_ref[...]
    i1 = r[:, 0:1]
    i2 = r[:, 1:2]
    lane = lax.broadcasted_iota(jnp.int32, (tm, 128), 1)
    lf = lane.astype(F32)
    oh = jnp.where((lf == i1) | (lf == i2), 1.0, 0.0)
    row = lax.broadcasted_iota(jnp.int32, (tm, 128), 0)
    cs = oh
    s = 1
    while s < tm:
        cs = cs + jnp.where(row >= s, pltpu.roll(cs, s, axis=0), 0.0)
        s *= 2
    excl = cs - oh + carry_ref[0:1, :]
    rank1 = jnp.sum(jnp.where(lf == i1, excl, 0.0), axis=-1, keepdims=True)
    rank2 = jnp.sum(jnp.where(lf == i2, excl, 0.0), axis=-1, keepdims=True)
    rank_ref[...] = jnp.where(lane == 0, rank1, jnp.where(lane == 1, rank2, 0.0))
    carry_ref[0:1, :] = carry_ref[0:1, :] + cs[tm - 1:tm, :]
    cnt_ref[...] = carry_ref[...]


def _route_rank(route):
    T = route.shape[0]
    tm = MOE_TM
    return pl.pallas_call(
        functools.partial(_route_rank_kernel, tm=tm),
        grid=(T // tm,),
        in_specs=[pl.BlockSpec((tm, 128), lambda i: (i, 0))],
        out_specs=[pl.BlockSpec((tm, 128), lambda i: (i, 0)), pl.BlockSpec((8, 128), lambda i: (0, 0))],
        out_shape=[jax.ShapeDtypeStruct((T, 128), F32), jax.ShapeDtypeStruct((8, 128), F32)],
        scratch_shapes=[pltpu.VMEM((8, 128), F32)],
        compiler_params=_cparams(("arbitrary",)),
        name="moe_rank",
    )(route)


def _row_copies(idx_ref, tm, make_copy):
    def body(r, carry):
        for k in range(2):
            make_copy(k, r, idx_ref[k * tm + r]).start()
        return carry

    lax.fori_loop(0, tm, body, 0, unroll=8)


def _dispatch_kernel(slots_ref, h_ref, xs_in_ref, xs_ref, idx_ref, sem_idx, sem_row, *, tm):
    del xs_in_ref
    i = pl.program_id(0)
    cp = pltpu.make_async_copy(slots_ref.at[pl.ds(i * 2 * tm, 2 * tm)], idx_ref, sem_idx)
    cp.start()
    cp.wait()
    _row_copies(idx_ref, tm, lambda k, r, s: pltpu.make_async_copy(
        h_ref.at[pl.ds(r, 1)], xs_ref.at[pl.ds(s, 1)], sem_row))
    for _ in range(2):
        pltpu.make_async_copy(h_ref, xs_ref.at[pl.ds(0, tm)], sem_row).wait()


def _dispatch(slots, hn, n_slots):
    T = hn.shape[0]
    tm = MOE_TM
    xs0 = jnp.zeros((n_slots, D_MODEL), F32)
    return pl.pallas_call(
        functools.partial(_dispatch_kernel, tm=tm),
        grid=(T // tm,),
        in_specs=[pl.BlockSpec(memory_space=pl.ANY),
                  pl.BlockSpec((tm, D_MODEL), lambda i: (i, 0)),
                  pl.BlockSpec(memory_space=pl.ANY)],
        out_specs=pl.BlockSpec(memory_space=pl.ANY),
        out_shape=jax.ShapeDtypeStruct((n_slots, D_MODEL), F32),
        scratch_shapes=[pltpu.SMEM((2 * tm,), jnp.int32), pltpu.SemaphoreType.DMA, pltpu.SemaphoreType.DMA],
        input_output_aliases={2: 0},
        compiler_params=_cparams(("arbitrary",)),
        name="moe_dispatch",
    )(slots, hn, xs0)


def _expert_ffn_kernel(te_ref, nu_ref, x_ref, wg_ref, wu_ref, wd_ref, o_ref, acc_ref):
    del te_ref
    i = pl.program_id(0)
    j = pl.program_id(1)
    last = pl.num_programs(1) - 1
    used = i < nu_ref[0]

    @pl.when(used & (j == 0))
    def _init():
        acc_ref[...] = jnp.zeros(acc_ref.shape, F32)

    @pl.when(used)
    def _step():
        x = x_ref[...].astype(BF16)
        g = jnp.dot(x, wg_ref[...], preferred_element_type=F32)
        u = jnp.dot(x, wu_ref[...], preferred_element_type=F32)
        acc_ref[...] += _dot(g * _sigmoid(g) * u, wd_ref[...])

    @pl.when(used & (j == last))
    def _fin():
        o_ref[...] = acc_ref[...]

    @pl.when(jnp.logical_not(used) & (j == last))
    def _unused():
        o_ref[...] = jnp.zeros(o_ref.shape, F32)


def _expert_ffn(tile_expert, n_used, xs, wg, wu, wd):
    n_slots = xs.shape[0]
    tm, tf = MOE_TILE, 512
    jw = lambda i, j, nu: jnp.where(i < nu[0], j, 0)
    grid_spec = pltpu.PrefetchScalarGridSpec(
        num_scalar_prefetch=2,
        grid=(n_slots // tm, D_FF // tf),
        in_specs=[pl.BlockSpec((tm, D_MODEL), lambda i, j, te, nu: (i, 0)),
                  pl.BlockSpec((None, D_MODEL, tf), lambda i, j, te, nu: (te[i], 0, jw(i, j, nu))),
                  pl.BlockSpec((None, D_MODEL, tf), lambda i, j, te, nu: (te[i], 0, jw(i, j, nu))),
                  pl.BlockSpec((None, tf, D_MODEL), lambda i, j, te, nu: (te[i], jw(i, j, nu), 0))],
        out_specs=pl.BlockSpec((tm, D_MODEL), lambda i, j, te, nu: (i, 0)),
        scratch_shapes=[pltpu.VMEM((tm, D_MODEL), F32)],
    )
    return pl.pallas_call(
        _expert_ffn_kernel,
        grid_spec=grid_spec,
        out_shape=jax.ShapeDtypeStruct((n_slots, D_MODEL), F32),
        compiler_params=_cparams(("parallel", "arbitrary")),
        name="moe_expert_ffn",
    )(tile_expert, n_used, xs, wg, wu, wd)


def _combine_kernel(slots_ref, ys_ref, route_ref, res_ref, o_ref, idx_ref, buf_ref, sem_idx, sem_row, *, tm):
    i = pl.program_id(0)
    cp = pltpu.make_async_copy(slots_ref.at[pl.ds(i * 2 * tm, 2 * tm)], idx_ref, sem_idx)
    cp.start()
    cp.wait()
    _row_copies(idx_ref, tm, lambda k, r, s: pltpu.make_async_copy(
        ys_ref.at[pl.ds(s, 1)], buf_ref.at[k, pl.ds(r, 1)], sem_row))
    for k in range(2):
        pltpu.make_async_copy(ys_ref.at[pl.ds(0, tm)], buf_ref.at[k], sem_row).wait()
    route = route_ref[...]
    o_ref[...] = res_ref[...] + route[:, 2:3] * buf_ref[0] + route[:, 3:4] * buf_ref[1]


def _combine(slots, ys, route, res):
    T = res.shape[0]
    tm = MOE_TM
    return pl.pallas_call(
        functools.partial(_combine_kernel, tm=tm),
        grid=(T // tm,),
        in_specs=[pl.BlockSpec(memory_space=pl.ANY),
                  pl.BlockSpec(memory_space=pl.ANY),
                  pl.BlockSpec((tm, 128), lambda i: (i, 0)),
                  pl.BlockSpec((tm, D_MODEL), lambda i: (i, 0))],
        out_specs=pl.BlockSpec((tm, D_MODEL), lambda i: (i, 0)),
        out_shape=jax.ShapeDtypeStruct((T, D_MODEL), F32),
        scratch_shapes=[pltpu.SMEM((2 * tm,), jnp.int32), pltpu.VMEM((2, tm, D_MODEL), F32),
                        pltpu.SemaphoreType.DMA, pltpu.SemaphoreType.DMA],
        compiler_params=_cparams(("arbitrary",)),
        name="moe_combine",
    )(slots, ys, route, res)


def _moe(hn, res, route, wg, wu, wd):
    T = hn.shape[0]
    n_slots = 2 * T + N_EXPERTS * MOE_TILE
    n_tiles = n_slots // MOE_TILE
    rank, cnt = _route_rank(route)
    counts = cnt[0, :N_EXPERTS].astype(jnp.int32)
    padded = (counts + MOE_TILE - 1) // MOE_TILE * MOE_TILE
    ends = jnp.cumsum(padded)
    starts = ends - padded
    ids = route[:, 0:2].astype(jnp.int32)
    slot = jnp.take(starts, ids) + rank[:, 0:2].astype(jnp.int32)
    slots = slot.reshape(T // MOE_TM, MOE_TM, 2).transpose(0, 2, 1).reshape(-1)
    tile_start = jnp.arange(n_tiles, dtype=jnp.int32) * MOE_TILE
    tile_expert = jnp.minimum(jnp.sum(tile_start[:, None] >= ends[None, :], axis=1), N_EXPERTS - 1).astype(jnp.int32)
    n_used = (ends[-1:] // MOE_TILE).astype(jnp.int32)
    xs = _dispatch(slots, hn, n_slots)
    ys = _expert_ffn(tile_expert, n_used, xs, wg, wu, wd)
    return _combine(slots, ys, route, res)


def _final_norm_kernel(x_ref, g_ref, o_ref):
    o_ref[...] = _rms(x_ref[...], D_MODEL) * g_ref[...]


def _final_norm(x2d, g):
    T = x2d.shape[0]
    tm = 1024
    return pl.pallas_call(
        _final_norm_kernel,
        grid=(T // tm,),
        in_specs=[pl.BlockSpec((tm, D_MODEL), lambda i: (i, 0)), pl.BlockSpec(g.shape, lambda i: (0, 0))],
        out_specs=pl.BlockSpec((tm, D_MODEL), lambda i: (i, 0)),
        out_shape=jax.ShapeDtypeStruct((T, D_MODEL), F32),
        compiler_params=_cparams(("parallel",)),
        name="final_norm",
    )(x2d, g)


def _rot_cols(w, half):
    return jnp.concatenate([-w[:, half:2 * half], w[:, 0:half]], axis=1)


def _nsa_rot64(w64):
    half = NSA_ROT // 2
    return jnp.concatenate([_rot_cols(w64[:, 0:NSA_ROT], half), jnp.zeros((w64.shape[0], NSA_HD - NSA_ROT), F32)],
                           axis=1)


def _ext_w_in(w):
    d = w.shape[0]
    zc = lambda n: jnp.zeros((d, n), F32)
    o1 = MLA_Q_RANK + MLA_KV_RANK + MLA_ROPE
    q0 = o1
    kv0 = o1 + NSA_HEADS * NSA_HD
    g0 = kv0 + 6 * NSA_HD
    o2 = g0 + 3 * NSA_HEADS
    o3 = o2 + 2 * LRU_W
    kv = lambda i: w[:, kv0 + i * NSA_HD:kv0 + (i + 1) * NSA_HD]
    kpe = w[:, MLA_Q_RANK + MLA_KV_RANK:o1]
    rope = [zc(64), kpe, zc(32),
            w[:, q0:kv0],
            kv(0), kv(1),
            kv(2), zc(64),
            kv(4), zc(64)]
    rot = [zc(64), _rot_cols(kpe, MLA_ROPE // 2), zc(32)]
    rot += [_nsa_rot64(w[:, q0 + h * NSA_HD:q0 + (h + 1) * NSA_HD]) for h in range(NSA_HEADS)]
    rot += [_nsa_rot64(kv(0)), zc(64), _nsa_rot64(kv(2)), zc(64), _nsa_rot64(kv(4)), zc(64)]
    rest = [w[:, 0:MLA_Q_RANK], zc(64),
            w[:, MLA_Q_RANK:MLA_Q_RANK + MLA_KV_RANK],
            kv(3), zc(64), kv(5), zc(64),
            w[:, g0:o2], zc(128 - 3 * NSA_HEADS),
            w[:, o2:o3],
            w[:, o3:o3 + SSD_DI + SSD_XBC],
            w[:, o3 + SSD_DI + SSD_XBC:], zc(128 - SSD_HEADS)]
    return jnp.concatenate(rope + rot + rest, axis=1).astype(BF16)


def _rope_tables(positions):
    pos = positions.astype(F32)[:, None]
    S = positions.shape[0]

    def cs(rot_dim):
        inv = ROPE_THETA ** (-jnp.arange(0, rot_dim, 2, dtype=F32) / rot_dim)
        ang = pos * inv[None, :]
        return jnp.cos(ang), jnp.sin(ang)

    cm, sm = cs(MLA_ROPE)
    cn, sn = cs(NSA_ROT)
    one = lambda n: jnp.ones((S, n), F32)
    zero = lambda n: jnp.zeros((S, n), F32)
    cos_m = jnp.concatenate([one(64), cm, cm, one(32)], axis=1)
    sin_m = jnp.concatenate([zero(64), sm, sm, zero(32)], axis=1)
    c64 = jnp.concatenate([cn, cn, one(NSA_HD - NSA_ROT)], axis=1)
    s64 = jnp.concatenate([sn, sn, zero(NSA_HD - NSA_ROT)], axis=1)
    cos_t = jnp.concatenate([cos_m] + [c64] * 4 + [c64, one(64)] * 3, axis=1)
    sin_t = jnp.concatenate([sin_m] + [s64] * 4 + [s64, zero(64)] * 3, axis=1)
    return cos_t, sin_t, cos_m, sin_m


def _mla_weights(w_uq, w_ukv, q_norm, kv_norm):
    hw = MLA_NOPE + MLA_ROPE
    pad_r = lambda m: jnp.concatenate([m, jnp.zeros((256 - MLA_Q_RANK, m.shape[1]), F32)], axis=0)
    z = lambda n: jnp.zeros((MLA_Q_RANK, n), F32)
    zk = lambda n: jnp.zeros((MLA_KV_RANK, n), F32)
    w1, w2, wk, wv = [], [], [], []
    for h in range(MLA_HEADS):
        qh = w_uq[:, h * hw:(h + 1) * hw]
        w1 += [qh, z(128 - hw)]
        w2 += [z(MLA_NOPE), _rot_cols(qh[:, MLA_NOPE:], MLA_ROPE // 2), z(128 - hw)]
        kvh = w_ukv[:, h * (MLA_NOPE + MLA_V):(h + 1) * (MLA_NOPE + MLA_V)]
        wk += [kvh[:, :MLA_NOPE], zk(128 - MLA_NOPE)]
        wv += [kvh[:, MLA_NOPE:], zk(128 - MLA_V)]
    qg = jnp.concatenate([q_norm, jnp.zeros((256 - MLA_Q_RANK,), F32)])[None, :]
    return (qg, pad_r(jnp.concatenate(w1, axis=1)).astype(BF16), pad_r(jnp.concatenate(w2, axis=1)).astype(BF16),
            kv_norm[None, :], jnp.concatenate(wk, axis=1).astype(BF16), jnp.concatenate(wv, axis=1).astype(BF16))


def _compress_weights(pe, w_cmp):
    half = CMP_LEN // 2
    wk = w_cmp[0].reshape(CMP_LEN, NSA_HD, NSA_HD)
    wv = w_cmp[1].reshape(CMP_LEN, NSA_HD, NSA_HD)
    z = jnp.zeros((half, NSA_HD, NSA_HD), F32)

    def interleave(ks, vs):
        top = jnp.concatenate([ks, z], axis=-1)
        bot = jnp.concatenate([z, vs], axis=-1)
        return jnp.concatenate([top, bot], axis=1).reshape(half * 2 * NSA_HD, 2 * NSA_HD)

    pad8 = lambda p: jnp.concatenate([p.reshape(1, -1), jnp.zeros((7, CMP_LEN * NSA_HD), F32)], axis=0)
    return (interleave(wk[:half], wv[:half]), interleave(wk[half:], wv[half:]),
            pad8(pe[0]), pad8(pe[1]), w_cmp[0], w_cmp[1])


def _overlap_matrix(S):
    nc_pad = S // CMP_STRIDE
    n = np.arange(nc_pad)[:, None]
    j = np.arange(SEL_LANES)[None, :]
    ov = ((n * CMP_STRIDE <= j * SEL_LEN + SEL_LEN - 1) & (n * CMP_STRIDE + CMP_LEN - 1 >= j * SEL_LEN)
          & (n < nc_pad - 1) & (j < S // SEL_LEN))
    return jnp.asarray(ov.astype(np.float32))


def _gate_expand():
    ex = np.zeros((3, 128, NSA_HEADS * NSA_HD), np.float32)
    for br in range(3):
        for h in range(NSA_HEADS):
            ex[br, h * 3 + br, h * NSA_HD:(h + 1) * NSA_HD] = 1.0
    return jnp.asarray(ex)


def _lru_gate_weights(w_gate, b_gate):
    wg = jnp.zeros((LRU_W, 2 * LRU_W), F32)
    bw = LRU_W // LRU_BLOCKS
    for g in range(2):
        for n in range(LRU_BLOCKS):
            wg = wg.at[n * bw:(n + 1) * bw, g * LRU_W + n * bw:g * LRU_W + (n + 1) * bw].set(w_gate[g, n])
    return wg.astype(BF16), b_gate.reshape(1, 2 * LRU_W)


def _pad_lanes(v, n=128):
    return jnp.concatenate([v, jnp.zeros((n - v.shape[0],), F32)])[None, :]


def kernel(x, positions, norm_mix, w_in, mla_q_norm, mla_w_uq, mla_kv_norm, mla_w_ukv, nsa_cmp_pe, nsa_w_cmp,
           lru_conv_w, lru_conv_b, lru_w_gate, lru_b_gate, lru_lambda, ssd_conv_w, ssd_conv_b, ssd_dt_bias,
           ssd_a_log, ssd_d, group_norm, w_out, norm_ffn, ffn_w_gate, ffn_w_up, ffn_w_down, moe_router,
           moe_w_gate, moe_w_up, moe_w_down, norm_final):
    B, S, D = x.shape
    T = B * S
    depth = w_in.shape[0]
    assert S // SEL_LEN <= SEL_LANES and S % 512 == 0
    cos_t, sin_t, cos_m, sin_m = _rope_tables(positions)
    ov = _overlap_matrix(S)
    ex = _gate_expand()
    h_res = x.reshape(T, D)
    for l in range(depth):
        ua, qn, kvc, ksel, kwin, vsel, vwin, gate, uc, ud = _inproj(
            h_res, norm_mix[l][None, :], _ext_w_in(w_in[l]), cos_t, sin_t, S)

        q_m, k_m, v_m = _mla_prep(ua, *_mla_weights(mla_w_uq[l], mla_w_ukv[l], mla_q_norm[l], mla_kv_norm[l]),
                                  cos_m, sin_m, B, S)
        y_a = _flash(q_m, k_m, v_m, window=None).reshape(T, GROUP_W)

        kvcmp = _compress(kvc.reshape(B, S // CMP_STRIDE, CMP_STRIDE * 128),
                          *_compress_weights(nsa_cmp_pe[l], nsa_w_cmp[l]))
        o_c, q_aug = _cmp_select(qn, kvcmp, ov, B, S)
        o_s = _flash(q_aug, ksel.reshape(B, 1, S, 256), vsel.reshape(B, 1, S, 128), window=None).reshape(T, GROUP_W)
        o_w = _flash(q_aug, kwin.reshape(B, 1, S, 256), vwin.reshape(B, 1, S, 128),
                     window=WINDOW).reshape(T, GROUP_W)

        wg_l, bg_l = _lru_gate_weights(lru_w_gate[l], lru_b_gate[l])
        y_c = _lru(uc, lru_conv_w[l], lru_conv_b[l][None, :], wg_l, bg_l,
                   jax.nn.softplus(-lru_lambda[l])[None, :], B, S)

        y_d = _ssd(ud, ssd_conv_w[l], ssd_conv_b[l][None, :], _pad_lanes(ssd_dt_bias[l]),
                   _pad_lanes(-jnp.exp(ssd_a_log[l])), jnp.repeat(ssd_d[l], SSD_HD)[None, :],
                   group_norm[l, 3][None, :], B, S)

        moe_layer = l % 2 == 1
        rw = None
        if moe_layer:
            rw = jnp.concatenate([moe_router[l // 2], jnp.zeros((D, 128 - N_EXPERTS), F32)], axis=1)
        outs = _outproj(y_a, o_c, o_s, o_w, gate, y_c, y_d, h_res, group_norm[l], ex, w_out[l].astype(BF16),
                        norm_ffn[l][None, :], rw)
        if moe_layer:
            h_res, hn, rg = outs
            h_res = _moe(hn, h_res, rg, moe_w_gate[l // 2].astype(BF16), moe_w_up[l // 2].astype(BF16),
                         moe_w_down[l // 2].astype(BF16))
        else:
            h_res, hn = outs
            h_res = _ffn(hn, h_res, ffn_w_gate[l // 2].astype(BF16), ffn_w_up[l // 2].astype(BF16),
                         ffn_w_down[l // 2].astype(BF16))
    return _final_norm(h_res, norm_final[None, :]).reshape(B, S, D).astype(x.dtype)
```

```python
import functools

import numpy as np
import jax
import jax.numpy as jnp
from jax import lax
from jax.experimental import pallas as pl
from jax.experimental.pallas import tpu as pltpu

F32 = jnp.float32
BF16 = jnp.bfloat16

D_MODEL = 1024
GROUP_W = 256
ROPE_THETA = 500000.0
NORM_EPS = 1e-6
NEG = -1e30
FORCE = 1e4
MLA_HEADS, MLA_NOPE, MLA_ROPE, MLA_V = 4, 64, 32, 64
MLA_Q_RANK, MLA_KV_RANK = 192, 128
NSA_HEADS, NSA_HD, NSA_ROT = 4, 64, 16
CMP_STRIDE, CMP_LEN, SEL_LEN, SEL_TOPN, WINDOW = 16, 32, 64, 16, 512
LRU_W, LRU_BLOCKS, LRU_C, CONV_W = 256, 4, 8.0, 4
SSD_HEADS, SSD_HD, SSD_GROUPS, SSD_STATE, SSD_CHUNK = 4, 64, 2, 128, 128
SSD_DI = 256
SSD_XBC = SSD_DI + 2 * SSD_GROUPS * SSD_STATE
D_FF = 3584
N_EXPERTS = 8

ROPE_W = 768
C_ROPE, C_ROT = 0, 768
C_CQ, C_CKV, C_VSW, C_GATE, C_LRU, C_SSD = 1536, 1792, 1920, 2176, 2304, 2816
SSD_IN_W = 1152
N_EXT = C_SSD + SSD_IN_W
LOG2E = 1.4426950408889634
SEL_LANES = 128

VMEM_LIMIT = 56 * 1024 * 1024


def _cparams(sem):
    return pltpu.CompilerParams(dimension_semantics=sem, vmem_limit_bytes=VMEM_LIMIT)


def _dot(a, b):
    return jnp.dot(a.astype(BF16), b.astype(BF16), preferred_element_type=F32)


def _dot_nt(a, b):
    return lax.dot_general(a.astype(BF16), b.astype(BF16), (((1,), (1,)), ((), ())),
                           preferred_element_type=F32)


def _dot_f32(a, b):
    return jnp.dot(a, b, precision=lax.Precision.HIGHEST, preferred_element_type=F32)


def _dot_f32_nt(a, b):
    return lax.dot_general(a, b, (((1,), (1,)), ((), ())), precision=lax.Precision.HIGHEST,
                           preferred_element_type=F32)


def _split2(x):
    hi = x.astype(BF16)
    return hi, (x - hi.astype(F32)).astype(BF16)


def _sigmoid(x):
    return 1.0 / (1.0 + jnp.exp(-x))


def _softplus(x):
    return jnp.maximum(x, 0.0) + jnp.log(1.0 + jnp.exp(-jnp.abs(x)))


def _rms(x, width):
    return x * lax.rsqrt(jnp.sum(x * x, axis=-1, keepdims=True) * (1.0 / width) + NORM_EPS)


def _inproj_kernel(x_ref, g_ref, w_ref, cos_ref, sin_ref,
                   ua_ref, qn_ref, kvc_ref, ksel_ref, kwin_ref, vsel_ref, vwin_ref, gate_ref, uc_ref, ud_ref,
                   *, tm, seq_blocks):
    x = x_ref[...]
    h = _rms(x, D_MODEL) * g_ref[...]
    y = _dot(h, w_ref[...])
    roped = y[:, C_ROPE:C_ROPE + ROPE_W] * cos_ref[...] + y[:, C_ROT:C_ROT + ROPE_W] * sin_ref[...]
    ua_ref[:, 0:384] = y[:, C_CQ:C_CQ + 384]
    ua_ref[:, 384:512] = roped[:, 0:128]
    qn_ref[...] = roped[:, 128:384]
    kvc_ref[...] = roped[:, 384:512]
    s0 = (pl.program_id(0) % seq_blocks) * tm
    pos = s0 + lax.broadcasted_iota(jnp.int32, (tm, SEL_LANES), 0)
    blk = lax.broadcasted_iota(jnp.int32, (tm, SEL_LANES), 1)
    onehot = jnp.where(jnp.right_shift(pos, 6) == blk, 1.0, 0.0)
    ksel_ref[:, 0:128] = roped[:, 512:640].astype(BF16)
    ksel_ref[:, 128:256] = onehot.astype(BF16)
    kwin_ref[:, 0:128] = roped[:, 640:768].astype(BF16)
    kwin_ref[:, 128:256] = jnp.zeros((tm, 128), BF16)
    ones_hi = jnp.where(blk >= 64, 1.0, 0.0)
    vsel_ref[...] = (y[:, C_VSW:C_VSW + 128] + ones_hi).astype(BF16)
    vwin_ref[...] = (y[:, C_VSW + 128:C_VSW + 256] + ones_hi).astype(BF16)
    gate_ref[...] = y[:, C_GATE:C_GATE + 128]
    uc_ref[...] = y[:, C_LRU:C_LRU + 512]
    ud_ref[...] = y[:, C_SSD:C_SSD + SSD_IN_W]


def _inproj(x2d, g, w_ext, cos_t, sin_t, seq):
    T = x2d.shape[0]
    tm = 256
    seq_blocks = seq // tm
    row = lambda w: pl.BlockSpec((tm, w), lambda i: (i, 0))
    full = lambda a: pl.BlockSpec(a.shape, lambda i: (0,) * a.ndim)
    tab = pl.BlockSpec((tm, ROPE_W), lambda i: (i % seq_blocks, 0))
    outs = [(512, F32), (256, F32), (128, F32), (256, BF16), (256, BF16), (128, BF16), (128, BF16), (128, F32),
            (512, F32), (SSD_IN_W, F32)]
    return pl.pallas_call(
        functools.partial(_inproj_kernel, tm=tm, seq_blocks=seq_blocks),
        grid=(T // tm,),
        in_specs=[row(D_MODEL), full(g), full(w_ext), tab, tab],
        out_specs=[row(w) for w, _ in outs],
        out_shape=[jax.ShapeDtypeStruct((T, w), dt) for w, dt in outs],
        compiler_params=_cparams(("parallel",)),
        name="inproj",
    )(x2d, g, w_ext, cos_t, sin_t)


def _mla_prep_kernel(ua_ref, qg_ref, w1_ref, w2_ref, kvg_ref, wk_ref, wv_ref, cos_ref, sin_ref,
                     q_ref, k_ref, v_ref):
    ua = ua_ref[...]
    cq = _rms(ua[:, 0:256], MLA_Q_RANK) * qg_ref[...]
    y1 = _dot(cq, w1_ref[...])
    y2 = _dot(cq, w2_ref[...])
    ckv = _rms(ua[:, 256:384], MLA_KV_RANK) * kvg_ref[...]
    kn = _dot(ckv, wk_ref[...])
    vv = _dot(ckv, wv_ref[...])
    kpe = ua[:, 384:512]
    cos = cos_ref[...]
    sin = sin_ref[...]
    scale = LOG2E * (MLA_NOPE + MLA_ROPE) ** -0.5
    ones_hi = jnp.where(lax.broadcasted_iota(jnp.int32, cos.shape, 1) >= 64, 1.0, 0.0)
    for h in range(MLA_HEADS):
        sl = slice(h * 128, (h + 1) * 128)
        q_ref[h] = ((y1[:, sl] * cos + y2[:, sl] * sin) * scale).astype(BF16)
        k_ref[h] = (kn[:, sl] + kpe).astype(BF16)
        v_ref[h] = (vv[:, sl] + ones_hi).astype(BF16)


def _mla_prep(ua, qg, w1, w2, kvg, wk, wv, cos_m, sin_m, B, S):
    tm = 512
    nb = S // tm
    full = lambda a: pl.BlockSpec(a.shape, lambda b, i: (0,) * a.ndim)
    tab = pl.BlockSpec((tm, 128), lambda b, i: (i, 0))
    hd = lambda w: pl.BlockSpec((None, MLA_HEADS, tm, w), lambda b, i: (b, 0, i, 0))
    return pl.pallas_call(
        _mla_prep_kernel,
        grid=(B, nb),
        in_specs=[pl.BlockSpec((tm, 512), lambda b, i: (b * nb + i, 0)),
                  full(qg), full(w1), full(w2), full(kvg), full(wk), full(wv), tab, tab],
        out_specs=[hd(128), hd(128), hd(128)],
        out_shape=[jax.ShapeDtypeStruct((B, MLA_HEADS, S, 128), BF16)] * 3,
        compiler_params=_cparams(("parallel", "parallel")),
        name="mla_prep",
    )(ua, qg, w1, w2, kvg, wk, wv, cos_m, sin_m)


def _flash_kernel(qi_ref, ki_ref, flag_ref, q_ref, k_ref, v_ref, o_ref, m_ref, acc_ref,
                  *, heads, kv_heads, t, window):
    p_idx = pl.program_id(1)
    qi = qi_ref[p_idx]
    ki = ki_ref[p_idx]
    flags = flag_ref[p_idx]

    @pl.when(jnp.bitwise_and(flags, 1) != 0)
    def _init():
        m_ref[...] = jnp.full(m_ref.shape, NEG, F32)
        acc_ref[...] = jnp.zeros(acc_ref.shape, F32)

    def step(masked):
        if masked:
            qpos = qi * t + lax.broadcasted_iota(jnp.int32, (t, t), 0)
            kpos = ki * t + lax.broadcasted_iota(jnp.int32, (t, t), 1)
            mask = kpos <= qpos
            if window is not None:
                mask = mask & (kpos > qpos - window)
        for h in range(heads):
            hk = h if kv_heads > 1 else 0
            s = lax.dot_general(q_ref[h], k_ref[hk], (((1,), (1,)), ((), ())), preferred_element_type=F32)
            if masked:
                s = jnp.where(mask, s, NEG)
            m_old = m_ref[h]
            m_new = jnp.maximum(m_old, jnp.max(s, axis=-1, keepdims=True))
            alpha = jnp.exp2(m_old - m_new)
            p = jnp.concatenate([jnp.exp2(s[:, c * 128:(c + 1) * 128] - m_new) for c in range(t // 128)],
                                axis=1).astype(BF16)
            acc_ref[h] = alpha * acc_ref[h] + jnp.dot(p, v_ref[hk], preferred_element_type=F32)
            m_ref[h] = m_new

    @pl.when(jnp.bitwise_and(flags, 4) != 0)
    def _masked():
        step(True)

    @pl.when(jnp.bitwise_and(flags, 4) == 0)
    def _plain():
        step(False)

    @pl.when(jnp.bitwise_and(flags, 2) != 0)
    def _fin():
        for h in range(heads):
            acc = acc_ref[h]
            o_ref[:, h * 64:(h + 1) * 64] = acc[:, 0:64] / acc[:, 64:128]


def _pair_tables(nq, back):
    qi, ki, flags = [], [], []
    for i in range(nq):
        lo = 0 if back is None else max(0, i - back)
        for j in range(lo, i + 1):
            qi.append(i)
            ki.append(j)
            masked = (j == i) or (back is not None)
            flags.append((1 if j == lo else 0) | (2 if j == i else 0) | (4 if masked else 0))
    mk = lambda a: jnp.asarray(np.asarray(a, np.int32))
    return mk(qi), mk(ki), mk(flags)


def _flash(q, k, v, *, window, t=512):
    B, H, S, dk = q.shape
    Hk = k.shape[1]
    t = min(t, S)
    nq = S // t
    back = None if window is None else -(-window // t)
    tabs = _pair_tables(nq, back)
    npairs = int(tabs[0].shape[0])
    grid_spec = pltpu.PrefetchScalarGridSpec(
        num_scalar_prefetch=3,
        grid=(B, npairs),
        in_specs=[pl.BlockSpec((None, H, t, dk), lambda b, p, qi, ki, fl: (b, 0, qi[p], 0)),
                  pl.BlockSpec((None, Hk, t, dk), lambda b, p, qi, ki, fl: (b, 0, ki[p], 0)),
                  pl.BlockSpec((None, Hk, t, 128), lambda b, p, qi, ki, fl: (b, 0, ki[p], 0))],
        out_specs=pl.BlockSpec((None, t, H * 64), lambda b, p, qi, ki, fl: (b, qi[p], 0)),
        scratch_shapes=[pltpu.VMEM((H, t, 128), F32), pltpu.VMEM((H, t, 128), F32)],
    )
    return pl.pallas_call(
        functools.partial(_flash_kernel, heads=H, kv_heads=Hk, t=t, window=window),
        grid_spec=grid_spec,
        out_shape=jax.ShapeDtypeStruct((B, S, H * 64), F32),
        compiler_params=_cparams(("parallel", "arbitrary")),
        name="flash_attn",
    )(*tabs, q, k, v)


def _compress_kernel(x_ref, wa_ref, wb_ref, pek_ref, pev_ref, wkf_ref, wvf_ref, o_ref, *, nc_pad):
    x = x_ref[...]
    a = _dot_f32(x, wa_ref[...])
    b = _dot_f32(x, wb_ref[...])
    ck = _dot_f32(pek_ref[...], wkf_ref[...])[0:1]
    cv = _dot_f32(pev_ref[...], wvf_ref[...])[0:1]
    const = jnp.concatenate([ck, cv], axis=-1)
    b_next = pltpu.roll(b, nc_pad - 1, axis=0)
    row = lax.broadcasted_iota(jnp.int32, (nc_pad, 128), 0)
    o_ref[...] = jnp.where(row < nc_pad - 1, a + b_next + const, 0.0)


def _compress(x, wa, wb, pek, pev, wkf, wvf):
    B, nc_pad, w = x.shape
    full = lambda a: pl.BlockSpec(a.shape, lambda b: (0,) * a.ndim)
    return pl.pallas_call(
        functools.partial(_compress_kernel, nc_pad=nc_pad),
        grid=(B,),
        in_specs=[pl.BlockSpec((None, nc_pad, w), lambda b: (b, 0, 0)),
                  full(wa), full(wb), full(pek), full(pev), full(wkf), full(wvf)],
        out_specs=pl.BlockSpec((None, nc_pad, 128), lambda b: (b, 0, 0)),
        out_shape=jax.ShapeDtypeStruct((B, nc_pad, 128), F32),
        compiler_params=_cparams(("parallel",)),
        name="nsa_compress",
    )(x, wa, wb, pek, pev, wkf, wvf)


def _cmp_select_kernel(q_ref, kvc_ref, ov_ref, oc_ref, qaug_ref, *, tq, nc_pad, n_top):
    i = pl.program_id(1)
    q = q_ref[...]
    kc = kvc_ref[:, 0:64]
    vc = kvc_ref[:, 64:128]
    scale = NSA_HD ** -0.5
    qpos = i * tq + lax.broadcasted_iota(jnp.int32, (tq, 1), 0)
    n_idx = lax.broadcasted_iota(jnp.int32, (1, nc_pad), 1)
    m_c = (n_idx * CMP_STRIDE + (CMP_LEN - 1) <= qpos) & (n_idx < nc_pad - 1)
    kc_hi, kc_lo = _split2(kc)
    kc3 = jnp.concatenate([kc_hi, kc_lo, kc_hi], axis=1)
    psum = jnp.zeros((tq, nc_pad), F32)
    for h in range(NSA_HEADS):
        q_hi, q_lo = _split2(q[:, h * NSA_HD:(h + 1) * NSA_HD])
        q3 = jnp.concatenate([q_hi, q_hi, q_lo], axis=1)
        s = lax.dot_general(q3, kc3, (((1,), (1,)), ((), ())), preferred_element_type=F32)
        s = jnp.where(m_c, s * scale, NEG)
        e = jnp.where(m_c, jnp.exp(s - jnp.max(s, axis=-1, keepdims=True)), 0.0)
        den = jnp.sum(e, axis=-1, keepdims=True)
        p = e / jnp.where(den > 0.0, den, 1.0)
        oc_ref[:, h * NSA_HD:(h + 1) * NSA_HD] = _dot(p, vc)
        psum = psum + p
    ov = ov_ref[...]
    p_hi = psum.astype(BF16)
    p_r = psum - p_hi.astype(F32)
    p_mid = p_r.astype(BF16)
    p_lo = (p_r - p_mid.astype(F32)).astype(BF16)
    imp = (jnp.dot(p_hi, ov, preferred_element_type=F32) + jnp.dot(p_mid, ov, preferred_element_type=F32)
           + jnp.dot(p_lo, ov, preferred_element_type=F32))
    cur = jnp.right_shift(qpos, 6)
    jj = lax.broadcasted_iota(jnp.int32, (1, SEL_LANES), 1)
    forced = (jj == 0) | (jj == cur) | (jj == cur - 1)
    bias = jnp.where(forced, 0.0, NEG)
    imp = jnp.where((jj <= cur) & jnp.logical_not(forced), imp, NEG)
    jf = jj.astype(F32)
    for _ in range(n_top - 3):
        mx = jnp.max(imp, axis=-1, keepdims=True)
        idx = jnp.min(jnp.where(imp == mx, jf, float(SEL_LANES)), axis=-1, keepdims=True)
        hit = jf == idx
        bias = jnp.where(hit & (mx > 0.5 * NEG), 0.0, bias)
        imp = jnp.where(hit, -3e38, imp)
    bias = bias.astype(BF16)
    qs = (q * (scale * LOG2E)).astype(BF16)
    for h in range(NSA_HEADS):
        qaug_ref[h, :, 0:64] = qs[:, h * NSA_HD:(h + 1) * NSA_HD]
        qaug_ref[h, :, 64:128] = jnp.zeros((tq, 64), BF16)
        qaug_ref[h, :, 128:256] = bias


def _cmp_select(qn, kvcmp, ov, B, S):
    tq = min(1024, S)
    nb = S // tq
    nc_pad = kvcmp.shape[1]
    n_top = min(SEL_TOPN, S // SEL_LEN)
    assert n_top >= 3
    return pl.pallas_call(
        functools.partial(_cmp_select_kernel, tq=tq, nc_pad=nc_pad, n_top=n_top),
        grid=(B, nb),
        in_specs=[pl.BlockSpec((tq, 256), lambda b, i: (b * nb + i, 0)),
                  pl.BlockSpec((None, nc_pad, 128), lambda b, i: (b, 0, 0)),
                  pl.BlockSpec(ov.shape, lambda b, i: (0, 0))],
        out_specs=[pl.BlockSpec((tq, 256), lambda b, i: (b * nb + i, 0)),
                   pl.BlockSpec((None, NSA_HEADS, tq, 256), lambda b, i: (b, 0, i, 0))],
        out_shape=[jax.ShapeDtypeStruct((B * S, 256), F32),
                   jax.ShapeDtypeStruct((B, NSA_HEADS, S, 256), BF16)],
        compiler_params=_cparams(("parallel", "parallel")),
        name="nsa_cmp_select",
    )(qn, kvcmp, ov)


def _shift_scan(a, b, t, width):
    row = lax.broadcasted_iota(jnp.int32, (t, width), 0)
    s = 1
    while s < t:
        keep = row >= s
        a_sh = jnp.where(keep, pltpu.roll(a, s, axis=0), 1.0)
        b_sh = jnp.where(keep, pltpu.roll(b, s, axis=0), 0.0)
        b = a * b_sh + b
        a = a * a_sh
        s *= 2
    return a, b


def _causal_conv(xpad_ref, x, cw_ref, cb_ref, t):
    xpad_ref[8:8 + t, :] = x
    y = cb_ref[...] + cw_ref[CONV_W - 1:CONV_W, :] * x
    for k in range(CONV_W - 1):
        off = 8 - (CONV_W - 1) + k
        y = y + cw_ref[k:k + 1, :] * xpad_ref[off:off + t, :]
    xpad_ref[0:8, :] = x[t - 8:t, :]
    return y


def _lru_kernel(u_ref, cw_ref, cb_ref, wg_ref, bg_ref, sp_ref, y_ref, xpad_ref, h_ref, *, t):
    @pl.when(pl.program_id(1) == 0)
    def _init():
        xpad_ref[0:8, :] = jnp.zeros((8, LRU_W), F32)
        h_ref[...] = jnp.zeros(h_ref.shape, F32)

    x = u_ref[:, 0:LRU_W]
    gbr = u_ref[:, LRU_W:2 * LRU_W]
    xb = _causal_conv(xpad_ref, x, cw_ref, cb_ref, t)
    g = _dot(xb, wg_ref[...]) + bg_ref[...]
    r = _sigmoid(g[:, 0:LRU_W])
    ig = _sigmoid(g[:, LRU_W:2 * LRU_W])
    log_a = -LRU_C * r * sp_ref[...]
    a = jnp.exp(log_a)
    one_m = -jnp.tanh(log_a) * (a * a + 1.0)
    b = jnp.sqrt(jnp.maximum(one_m, 0.0)) * (ig * xb)
    a_cum, h = _shift_scan(a, b, t, LRU_W)
    h = h + a_cum * h_ref[0:1, :]
    h_ref[0:1, :] = h[t - 1:t, :]
    gelu = 0.5 * gbr * (1.0 + jnp.tanh(0.7978845608028654 * (gbr + 0.044715 * gbr * gbr * gbr)))
    y_ref[...] = h * gelu


def _lru(uc, cw, cb, wg, bg, sp, B, S):
    t = 512
    nb = S // t
    full = lambda a: pl.BlockSpec(a.shape, lambda b, i: (0,) * a.ndim)
    return pl.pallas_call(
        functools.partial(_lru_kernel, t=t),
        grid=(B, nb),
        in_specs=[pl.BlockSpec((t, 2 * LRU_W), lambda b, i: (b * nb + i, 0)),
                  full(cw), full(cb), full(wg), full(bg), full(sp)],
        out_specs=pl.BlockSpec((t, LRU_W), lambda b, i: (b * nb + i, 0)),
        out_shape=jax.ShapeDtypeStruct((B * S, LRU_W), F32),
        scratch_shapes=[pltpu.VMEM((t + 8, LRU_W), F32), pltpu.VMEM((8, LRU_W), F32)],
        compiler_params=_cparams(("parallel", "arbitrary")),
        name="rglru",
    )(uc, cw, cb, wg, bg, sp)


def _ssd_kernel(u_ref, cw_ref, cb_ref, dtb_ref, a_ref, d_ref, ng_ref, y_ref, xpad_ref, st_ref, *, t):
    @pl.when(pl.program_id(1) == 0)
    def _init():
        xpad_ref[0:8, :] = jnp.zeros((8, SSD_XBC), F32)
        st_ref[...] = jnp.zeros(st_ref.shape, F32)

    L = SSD_CHUNK
    z = u_ref[:, 0:SSD_DI]
    conv = _causal_conv(xpad_ref, u_ref[:, SSD_DI:SSD_DI + SSD_XBC], cw_ref, cb_ref, t)
    xbc = conv * _sigmoid(conv)
    dt = _softplus(u_ref[:, SSD_DI + SSD_XBC:SSD_IN_W] + dtb_ref[...])
    a = dt * a_ref[...]
    row = lax.broadcasted_iota(jnp.int32, (t, 128), 0)
    rin = jnp.bitwise_and(row, L - 1)
    cs = a
    s = 1
    while s < L:
        cs = cs + jnp.where(rin >= s, pltpu.roll(cs, s, axis=0), 0.0)
        s *= 2
    tril = lax.broadcasted_iota(jnp.int32, (L, L), 0) >= lax.broadcasted_iota(jnp.int32, (L, L), 1)
    gn = SSD_GROUPS * SSD_STATE
    rep = SSD_HEADS // SSD_GROUPS
    for c in range(t // L):
        rs = slice(c * L, (c + 1) * L)
        cs_c = cs[rs]
        cs_t = cs_c.T
        cs_last = cs_c[L - 1:L, :]
        ys = []
        for g in range(SSD_GROUPS):
            bg = xbc[rs, SSD_DI + g * SSD_STATE:SSD_DI + (g + 1) * SSD_STATE]
            cg = xbc[rs, SSD_DI + gn + g * SSD_STATE:SSD_DI + gn + (g + 1) * SSD_STATE]
            gmat = _dot_nt(cg, bg)
            bg_t = bg.T
            for hh in range(rep):
                h = g * rep + hh
                cs_col = cs_c[:, h:h + 1]
                lm = jnp.exp(jnp.where(tril, cs_col - cs_t[h:h + 1, :], NEG))
                xh = xbc[rs, h * SSD_HD:(h + 1) * SSD_HD]
                xdt = xh * dt[rs, h:h + 1]
                st_old = st_ref[h]
                y_h = _dot(gmat * lm, xdt) + _dot(cg, st_old) * jnp.exp(cs_col)
                dec = jnp.exp(cs_last[:, h:h + 1] - cs_col)
                st_ref[h] = jnp.exp(cs_last[:, h:h + 1]) * st_old + _dot(bg_t, xdt * dec)
                ys.append(y_h)
        y = jnp.concatenate(ys, axis=-1) + xbc[rs, 0:SSD_DI] * d_ref[...]
        zc = z[rs]
        y = y * (zc * _sigmoid(zc))
        y_ref[rs, :] = _rms(y, SSD_DI) * ng_ref[...]


def _ssd(ud, cw, cb, dtb, a_neg, d_vec, ng, B, S):
    t = 512
    nb = S // t
    full = lambda a: pl.BlockSpec(a.shape, lambda b, i: (0,) * a.ndim)
    return pl.pallas_call(
        functools.partial(_ssd_kernel, t=t),
        grid=(B, nb),
        in_specs=[pl.BlockSpec((t, SSD_IN_W), lambda b, i: (b * nb + i, 0)),
                  full(cw), full(cb), full(dtb), full(a_neg), full(d_vec), full(ng)],
        out_specs=pl.BlockSpec((t, SSD_DI), lambda b, i: (b * nb + i, 0)),
        out_shape=jax.ShapeDtypeStruct((B * S, SSD_DI), F32),
        scratch_shapes=[pltpu.VMEM((t + 8, SSD_XBC), F32), pltpu.VMEM((SSD_HEADS, SSD_STATE, SSD_HD), F32)],
        compiler_params=_cparams(("parallel", "arbitrary")),
        name="ssd",
    )(ud, cw, cb, dtb, a_neg, d_vec, ng)


def _outproj_kernel(*refs, with_router):
    if with_router:
        (ya_ref, oc_ref, os_ref, ow_ref, gate_ref, yc_ref, yd_ref, res_ref, gn_ref, ex_ref, w_ref, nf_ref,
         rw_ref, hres_ref, hn_ref, rg_ref) = refs
    else:
        (ya_ref, oc_ref, os_ref, ow_ref, gate_ref, yc_ref, yd_ref, res_ref, gn_ref, ex_ref, w_ref, nf_ref,
         hres_ref, hn_ref) = refs
    sg_hi, sg_lo = _split2(_sigmoid(gate_ref[...]))
    ex = ex_ref[...]
    gx = jnp.dot(sg_hi, ex, preferred_element_type=F32) + jnp.dot(sg_lo, ex, preferred_element_type=F32)
    yb = gx[:, 0:256] * oc_ref[...] + gx[:, 256:512] * os_ref[...] + gx[:, 512:768] * ow_ref[...]
    y = jnp.concatenate([_rms(ya_ref[...], GROUP_W) * gn_ref[0:1, :],
                         _rms(yb, GROUP_W) * gn_ref[1:2, :],
                         _rms(yc_ref[...], GROUP_W) * gn_ref[2:3, :],
                         yd_ref[...]], axis=-1)
    hres = res_ref[...] + _dot(y, w_ref[...])
    hres_ref[...] = hres
    hn = _rms(hres, D_MODEL) * nf_ref[...]
    hn_ref[...] = hn.astype(hn_ref.dtype)
    if with_router:
        h_hi, h_lo = _split2(hn)
        logits = (jnp.dot(h_hi, rw_ref[0], preferred_element_type=F32)
                  + jnp.dot(h_hi, rw_ref[1], preferred_element_type=F32)
                  + jnp.dot(h_lo, rw_ref[0], preferred_element_type=F32))
        lane = lax.broadcasted_iota(jnp.int32, logits.shape, 1)
        lf = lane.astype(F32)
        logits = jnp.where(lane < N_EXPERTS, logits, NEG)
        m1 = jnp.max(logits, axis=-1, keepdims=True)
        i1 = jnp.min(jnp.where(logits == m1, lf, 128.0), axis=-1, keepdims=True)
        rest = jnp.where(lf == i1, NEG, logits)
        m2 = jnp.max(rest, axis=-1, keepdims=True)
        i2 = jnp.min(jnp.where(rest == m2, lf, 128.0), axis=-1, keepdims=True)
        e2 = jnp.exp(m2 - m1)
        den = 1.0 + e2
        rg_ref[...] = jnp.where(lane == 0, i1, jnp.where(lane == 1, i2, jnp.where(
            lane == 2, 1.0 / den, jnp.where(lane == 3, e2 / den, 0.0))))


def _outproj(ya, oc, osel, ow, gate, yc, yd, res, gn, ex, w, nf, rw):
    T = ya.shape[0]
    tm = 512
    with_router = rw is not None
    row = lambda wd: pl.BlockSpec((tm, wd), lambda i: (i, 0))
    full = lambda a: pl.BlockSpec(a.shape, lambda i: (0,) * a.ndim)
    ins = [ya, oc, osel, ow, gate, yc, yd, res, gn, ex, w, nf]
    in_specs = [row(256), row(256), row(256), row(256), row(128), row(256), row(256), row(D_MODEL),
                full(gn), full(ex), full(w), full(nf)]
    out_specs = [row(D_MODEL), row(D_MODEL)]
    out_shape = [jax.ShapeDtypeStruct((T, D_MODEL), F32),
                 jax.ShapeDtypeStruct((T, D_MODEL), F32 if with_router else BF16)]
    if with_router:
        ins.append(rw)
        in_specs.append(full(rw))
        out_specs.append(row(128))
        out_shape.append(jax.ShapeDtypeStruct((T, 128), F32))
    return pl.pallas_call(
        functools.partial(_outproj_kernel, with_router=with_router),
        grid=(T // tm,),
        in_specs=in_specs,
        out_specs=out_specs,
        out_shape=out_shape,
        compiler_params=_cparams(("parallel",)),
        name="outproj",
    )(*ins)


def _ffn_kernel(h_ref, res_ref, wg_ref, wu_ref, wd_ref, o_ref, acc_ref):
    j = pl.program_id(1)

    @pl.when(j == 0)
    def _init():
        acc_ref[...] = jnp.zeros(acc_ref.shape, F32)

    h = h_ref[...]
    g = jnp.dot(h, wg_ref[...], preferred_element_type=F32)
    u = jnp.dot(h, wu_ref[...], preferred_element_type=F32)
    acc_ref[...] += _dot(g * _sigmoid(g) * u, wd_ref[...])

    @pl.when(j == pl.num_programs(1) - 1)
    def _fin():
        o_ref[...] = res_ref[...] + acc_ref[...]


def _ffn(hn, res, wg, wu, wd):
    T = hn.shape[0]
    tm, tf = 1024, 512
    return pl.pallas_call(
        _ffn_kernel,
        grid=(T // tm, D_FF // tf),
        in_specs=[pl.BlockSpec((tm, D_MODEL), lambda i, j: (i, 0)),
                  pl.BlockSpec((tm, D_MODEL), lambda i, j: (i, 0)),
                  pl.BlockSpec((D_MODEL, tf), lambda i, j: (0, j)),
                  pl.BlockSpec((D_MODEL, tf), lambda i, j: (0, j)),
                  pl.BlockSpec((tf, D_MODEL), lambda i, j: (j, 0))],
        out_specs=pl.BlockSpec((tm, D_MODEL), lambda i, j: (i, 0)),
        out_shape=jax.ShapeDtypeStruct((T, D_MODEL), F32),
        scratch_shapes=[pltpu.VMEM((tm, D_MODEL), F32)],
        compiler_params=_cparams(("parallel", "arbitrary")),
        name="ffn_dense",
    )(hn, res, wg, wu, wd)


MOE_TILE = 1024
MOE_TM = 512


def _route_rank_kernel(route_ref, rank_ref, cnt_ref, carry_ref, *, tm):
    @pl.when(pl.program_id(0) == 0)
    def _init():
        carry_ref[...] = jnp.zeros(carry_ref.shape, F32)

    r = route_ref[...]
    i1 = r[:, 0:1]
    i2 = r[:, 1:2]
    lane = lax.broadcasted_iota(jnp.int32, (tm, 128), 1)
    lf = lane.astype(F32)
    oh = jnp.where((lf == i1) | (lf == i2), 1.0, 0.0)
    row = lax.broadcasted_iota(jnp.int32, (tm, 128), 0)
    cs = oh
    s = 1
    while s < tm:
        cs = cs + jnp.where(row >= s, pltpu.roll(cs, s, axis=0), 0.0)
        s *= 2
    excl = cs - oh + carry_ref[0:1, :]
    rank1 = jnp.sum(jnp.where(lf == i1, excl, 0.0), axis=-1, keepdims=True)
    rank2 = jnp.sum(jnp.where(lf == i2, excl, 0.0), axis=-1, keepdims=True)
    rank_ref[...] = jnp.where(lane == 0, rank1, jnp.where(lane == 1, rank2, 0.0))
    carry_ref[0:1, :] = carry_ref[0:1, :] + cs[tm - 1:tm, :]
    cnt_ref[...] = carry_ref[...]


def _route_rank(route):
    T = route.shape[0]
    tm = MOE_TM
    return pl.pallas_call(
        functools.partial(_route_rank_kernel, tm=tm),
        grid=(T // tm,),
        in_specs=[pl.BlockSpec((tm, 128), lambda i: (i, 0))],
        out_specs=[pl.BlockSpec((tm, 128), lambda i: (i, 0)), pl.BlockSpec((8, 128), lambda i: (0, 0))],
        out_shape=[jax.ShapeDtypeStruct((T, 128), F32), jax.ShapeDtypeStruct((8, 128), F32)],
        scratch_shapes=[pltpu.VMEM((8, 128), F32)],
        compiler_params=_cparams(("arbitrary",)),
        name="moe_rank",
    )(route)


def _row_copies(idx_ref, tm, make_copy):
    def body(r, carry):
        for k in range(2):
            make_copy(k, r, idx_ref[k * tm + r]).start()
        return carry

    lax.fori_loop(0, tm, body, 0, unroll=8)


def _dispatch_kernel(slots_ref, h_ref, xs_in_ref, xs_ref, idx_ref, sem_idx, sem_row, *, tm):
    del xs_in_ref
    i = pl.program_id(0)
    cp = pltpu.make_async_copy(slots_ref.at[pl.ds(i * 2 * tm, 2 * tm)], idx_ref, sem_idx)
    cp.start()
    cp.wait()
    _row_copies(idx_ref, tm, lambda k, r, s: pltpu.make_async_copy(
        h_ref.at[pl.ds(r, 1)], xs_ref.at[pl.ds(s, 1)], sem_row))
    for _ in range(2):
        pltpu.make_async_copy(h_ref, xs_ref.at[pl.ds(0, tm)], sem_row).wait()


def _dispatch(slots, hn, n_slots):
    T = hn.shape[0]
    tm = MOE_TM
    xs0 = jnp.zeros((n_slots, D_MODEL), F32)
    return pl.pallas_call(
        functools.partial(_dispatch_kernel, tm=tm),
        grid=(T // tm,),
        in_specs=[pl.BlockSpec(memory_space=pl.ANY),
                  pl.BlockSpec((tm, D_MODEL), lambda i: (i, 0)),
                  pl.BlockSpec(memory_space=pl.ANY)],
        out_specs=pl.BlockSpec(memory_space=pl.ANY),
        out_shape=jax.ShapeDtypeStruct((n_slots, D_MODEL), F32),
        scratch_shapes=[pltpu.SMEM((2 * tm,), jnp.int32), pltpu.SemaphoreType.DMA, pltpu.SemaphoreType.DMA],
        input_output_aliases={2: 0},
        compiler_params=_cparams(("arbitrary",)),
        name="moe_dispatch",
    )(slots, hn, xs0)


def _expert_ffn_kernel(te_ref, nu_ref, x_ref, wg_ref, wu_ref, wd_ref, o_ref, acc_ref):
    del te_ref
    i = pl.program_id(0)
    j = pl.program_id(1)
    last = pl.num_programs(1) - 1
    used = i < nu_ref[0]

    @pl.when(used & (j == 0))
    def _init():
        acc_ref[...] = jnp.zeros(acc_ref.shape, F32)

    @pl.when(used)
    def _step():
        x = x_ref[...].astype(BF16)
        g = jnp.dot(x, wg_ref[...], preferred_element_type=F32)
        u = jnp.dot(x, wu_ref[...], preferred_element_type=F32)
        acc_ref[...] += _dot(g * _sigmoid(g) * u, wd_ref[...])

    @pl.when(used & (j == last))
    def _fin():
        o_ref[...] = acc_ref[...]

    @pl.when(jnp.logical_not(used) & (j == last))
    def _unused():
        o_ref[...] = jnp.zeros(o_ref.shape, F32)


def _expert_ffn(tile_expert, n_used, xs, wg, wu, wd):
    n_slots = xs.shape[0]
    tm, tf = MOE_TILE, 512
    jw = lambda i, j, nu: jnp.where(i < nu[0], j, 0)
    grid_spec = pltpu.PrefetchScalarGridSpec(
        num_scalar_prefetch=2,
        grid=(n_slots // tm, D_FF // tf),
        in_specs=[pl.BlockSpec((tm, D_MODEL), lambda i, j, te, nu: (i, 0)),
                  pl.BlockSpec((None, D_MODEL, tf), lambda i, j, te, nu: (te[i], 0, jw(i, j, nu))),
                  pl.BlockSpec((None, D_MODEL, tf), lambda i, j, te, nu: (te[i], 0, jw(i, j, nu))),
                  pl.BlockSpec((None, tf, D_MODEL), lambda i, j, te, nu: (te[i], jw(i, j, nu), 0))],
        out_specs=pl.BlockSpec((tm, D_MODEL), lambda i, j, te, nu: (i, 0)),
        scratch_shapes=[pltpu.VMEM((tm, D_MODEL), F32)],
    )
    return pl.pallas_call(
        _expert_ffn_kernel,
        grid_spec=grid_spec,
        out_shape=jax.ShapeDtypeStruct((n_slots, D_MODEL), F32),
        compiler_params=_cparams(("parallel", "arbitrary")),
        name="moe_expert_ffn",
    )(tile_expert, n_used, xs, wg, wu, wd)


def _combine_kernel(slots_ref, ys_ref, route_ref, res_ref, ng_ref, o_ref, idx_ref, buf_ref, sem_idx, sem_row,
                    *, tm, final_norm):
    i = pl.program_id(0)
    cp = pltpu.make_async_copy(slots_ref.at[pl.ds(i * 2 * tm, 2 * tm)], idx_ref, sem_idx)
    cp.start()
    cp.wait()
    _row_copies(idx_ref, tm, lambda k, r, s: pltpu.make_async_copy(
        ys_ref.at[pl.ds(s, 1)], buf_ref.at[k, pl.ds(r, 1)], sem_row))
    for k in range(2):
        pltpu.make_async_copy(ys_ref.at[pl.ds(0, tm)], buf_ref.at[k], sem_row).wait()
    route = route_ref[...]
    out = res_ref[...] + route[:, 2:3] * buf_ref[0] + route[:, 3:4] * buf_ref[1]
    if final_norm:
        out = _rms(out, D_MODEL) * ng_ref[...]
    o_ref[...] = out


def _combine(slots, ys, route, res, norm_g, final_norm):
    T = res.shape[0]
    tm = MOE_TM
    return pl.pallas_call(
        functools.partial(_combine_kernel, tm=tm, final_norm=final_norm),
        grid=(T // tm,),
        in_specs=[pl.BlockSpec(memory_space=pl.ANY),
                  pl.BlockSpec(memory_space=pl.ANY),
                  pl.BlockSpec((tm, 128), lambda i: (i, 0)),
                  pl.BlockSpec((tm, D_MODEL), lambda i: (i, 0)),
                  pl.BlockSpec((1, D_MODEL), lambda i: (0, 0))],
        out_specs=pl.BlockSpec((tm, D_MODEL), lambda i: (i, 0)),
        out_shape=jax.ShapeDtypeStruct((T, D_MODEL), F32),
        scratch_shapes=[pltpu.SMEM((2 * tm,), jnp.int32), pltpu.VMEM((2, tm, D_MODEL), F32),
                        pltpu.SemaphoreType.DMA, pltpu.SemaphoreType.DMA],
        compiler_params=_cparams(("arbitrary",)),
        name="moe_combine",
    )(slots, ys, route, res, norm_g)


def _moe(hn, res, route, wg, wu, wd, norm_g, final_norm):
    T = hn.shape[0]
    n_slots = 2 * T + N_EXPERTS * MOE_TILE
    n_tiles = n_slots // MOE_TILE
    rank, cnt = _route_rank(route)
    counts = cnt[0, :N_EXPERTS].astype(jnp.int32)
    padded = (counts + MOE_TILE - 1) // MOE_TILE * MOE_TILE
    ends = jnp.cumsum(padded)
    starts = ends - padded
    ids = route[:, 0:2].astype(jnp.int32)
    slot = jnp.take(starts, ids) + rank[:, 0:2].astype(jnp.int32)
    slots = slot.reshape(T // MOE_TM, MOE_TM, 2).transpose(0, 2, 1).reshape(-1)
    tile_start = jnp.arange(n_tiles, dtype=jnp.int32) * MOE_TILE
    tile_expert = jnp.minimum(jnp.sum(tile_start[:, None] >= ends[None, :], axis=1), N_EXPERTS - 1).astype(jnp.int32)
    n_used = (ends[-1:] // MOE_TILE).astype(jnp.int32)
    xs = _dispatch(slots, hn, n_slots)
    ys = _expert_ffn(tile_expert, n_used, xs, wg, wu, wd)
    return _combine(slots, ys, route, res, norm_g, final_norm)


def _final_norm_kernel(x_ref, g_ref, o_ref):
    o_ref[...] = _rms(x_ref[...], D_MODEL) * g_ref[...]


def _final_norm(x2d, g):
    T = x2d.shape[0]
    tm = 1024
    return pl.pallas_call(
        _final_norm_kernel,
        grid=(T // tm,),
        in_specs=[pl.BlockSpec((tm, D_MODEL), lambda i: (i, 0)), pl.BlockSpec(g.shape, lambda i: (0, 0))],
        out_specs=pl.BlockSpec((tm, D_MODEL), lambda i: (i, 0)),
        out_shape=jax.ShapeDtypeStruct((T, D_MODEL), F32),
        compiler_params=_cparams(("parallel",)),
        name="final_norm",
    )(x2d, g)


def _rot_cols(w, half):
    return jnp.concatenate([-w[:, half:2 * half], w[:, 0:half]], axis=1)


def _nsa_rot64(w64):
    half = NSA_ROT // 2
    return jnp.concatenate([_rot_cols(w64[:, 0:NSA_ROT], half), jnp.zeros((w64.shape[0], NSA_HD - NSA_ROT), F32)],
                           axis=1)


def _ext_w_in(w):
    d = w.shape[0]
    zc = lambda n: jnp.zeros((d, n), F32)
    o1 = MLA_Q_RANK + MLA_KV_RANK + MLA_ROPE
    q0 = o1
    kv0 = o1 + NSA_HEADS * NSA_HD
    g0 = kv0 + 6 * NSA_HD
    o2 = g0 + 3 * NSA_HEADS
    o3 = o2 + 2 * LRU_W
    kv = lambda i: w[:, kv0 + i * NSA_HD:kv0 + (i + 1) * NSA_HD]
    kpe = w[:, MLA_Q_RANK + MLA_KV_RANK:o1]
    rope = [zc(64), kpe, zc(32),
            w[:, q0:kv0],
            kv(0), kv(1),
            kv(2), zc(64),
            kv(4), zc(64)]
    rot = [zc(64), _rot_cols(kpe, MLA_ROPE // 2), zc(32)]
    rot += [_nsa_rot64(w[:, q0 + h * NSA_HD:q0 + (h + 1) * NSA_HD]) for h in range(NSA_HEADS)]
    rot += [_nsa_rot64(kv(0)), zc(64), _nsa_rot64(kv(2)), zc(64), _nsa_rot64(kv(4)), zc(64)]
    rest = [w[:, 0:MLA_Q_RANK], zc(64),
            w[:, MLA_Q_RANK:MLA_Q_RANK + MLA_KV_RANK],
            kv(3), zc(64), kv(5), zc(64),
            w[:, g0:o2], zc(128 - 3 * NSA_HEADS),
            w[:, o2:o3],
            w[:, o3:o3 + SSD_DI + SSD_XBC],
            w[:, o3 + SSD_DI + SSD_XBC:], zc(128 - SSD_HEADS)]
    return jnp.concatenate(rope + rot + rest, axis=1).astype(BF16)


def _rope_tables(positions):
    pos = positions.astype(F32)[:, None]
    S = positions.shape[0]

    def cs(rot_dim):
        inv = ROPE_THETA ** (-jnp.arange(0, rot_dim, 2, dtype=F32) / rot_dim)
        ang = pos * inv[None, :]
        return jnp.cos(ang), jnp.sin(ang)

    cm, sm = cs(MLA_ROPE)
    cn, sn = cs(NSA_ROT)
    one = lambda n: jnp.ones((S, n), F32)
    zero = lambda n: jnp.zeros((S, n), F32)
    cos_m = jnp.concatenate([one(64), cm, cm, one(32)], axis=1)
    sin_m = jnp.concatenate([zero(64), sm, sm, zero(32)], axis=1)
    c64 = jnp.concatenate([cn, cn, one(NSA_HD - NSA_ROT)], axis=1)
    s64 = jnp.concatenate([sn, sn, zero(NSA_HD - NSA_ROT)], axis=1)
    cos_t = jnp.concatenate([cos_m] + [c64] * 4 + [c64, one(64)] * 3, axis=1)
    sin_t = jnp.concatenate([sin_m] + [s64] * 4 + [s64, zero(64)] * 3, axis=1)
    return cos_t, sin_t, cos_m, sin_m


def _mla_weights(w_uq, w_ukv, q_norm, kv_norm):
    hw = MLA_NOPE + MLA_ROPE
    pad_r = lambda m: jnp.concatenate([m, jnp.zeros((256 - MLA_Q_RANK, m.shape[1]), F32)], axis=0)
    z = lambda n: jnp.zeros((MLA_Q_RANK, n), F32)
    zk = lambda n: jnp.zeros((MLA_KV_RANK, n), F32)
    w1, w2, wk, wv = [], [], [], []
    for h in range(MLA_HEADS):
        qh = w_uq[:, h * hw:(h + 1) * hw]
        w1 += [qh, z(128 - hw)]
        w2 += [z(MLA_NOPE), _rot_cols(qh[:, MLA_NOPE:], MLA_ROPE // 2), z(128 - hw)]
        kvh = w_ukv[:, h * (MLA_NOPE + MLA_V):(h + 1) * (MLA_NOPE + MLA_V)]
        wk += [kvh[:, :MLA_NOPE], zk(128 - MLA_NOPE)]
        wv += [kvh[:, MLA_NOPE:], zk(128 - MLA_V)]
    qg = jnp.concatenate([q_norm, jnp.zeros((256 - MLA_Q_RANK,), F32)])[None, :]
    return (qg, pad_r(jnp.concatenate(w1, axis=1)).astype(BF16), pad_r(jnp.concatenate(w2, axis=1)).astype(BF16),
            kv_norm[None, :], jnp.concatenate(wk, axis=1).astype(BF16), jnp.concatenate(wv, axis=1).astype(BF16))


def _compress_weights(pe, w_cmp):
    half = CMP_LEN // 2
    wk = w_cmp[0].reshape(CMP_LEN, NSA_HD, NSA_HD)
    wv = w_cmp[1].reshape(CMP_LEN, NSA_HD, NSA_HD)
    z = jnp.zeros((half, NSA_HD, NSA_HD), F32)

    def interleave(ks, vs):
        top = jnp.concatenate([ks, z], axis=-1)
        bot = jnp.concatenate([z, vs], axis=-1)
        return jnp.concatenate([top, bot], axis=1).reshape(half * 2 * NSA_HD, 2 * NSA_HD)

    pad8 = lambda p: jnp.concatenate([p.reshape(1, -1), jnp.zeros((7, CMP_LEN * NSA_HD), F32)], axis=0)
    return (interleave(wk[:half], wv[:half]), interleave(wk[half:], wv[half:]),
            pad8(pe[0]), pad8(pe[1]), w_cmp[0], w_cmp[1])


def _overlap_matrix(S):
    nc_pad = S // CMP_STRIDE
    n = np.arange(nc_pad)[:, None]
    j = np.arange(SEL_LANES)[None, :]
    ov = ((n * CMP_STRIDE <= j * SEL_LEN + SEL_LEN - 1) & (n * CMP_STRIDE + CMP_LEN - 1 >= j * SEL_LEN)
          & (n < nc_pad - 1) & (j < S // SEL_LEN))
    return jnp.asarray(ov.astype(np.float32)).astype(BF16)


def _gate_expand():
    gw = NSA_HEADS * NSA_HD
    ex = np.zeros((128, 3 * gw), np.float32)
    for br in range(3):
        for h in range(NSA_HEADS):
            ex[h * 3 + br, br * gw + h * NSA_HD:br * gw + (h + 1) * NSA_HD] = 1.0
    return jnp.asarray(ex).astype(BF16)


def _lru_gate_weights(w_gate, b_gate):
    wg = jnp.zeros((LRU_W, 2 * LRU_W), F32)
    bw = LRU_W // LRU_BLOCKS
    for g in range(2):
        for n in range(LRU_BLOCKS):
            wg = wg.at[n * bw:(n + 1) * bw, g * LRU_W + n * bw:g * LRU_W + (n + 1) * bw].set(w_gate[g, n])
    return wg.astype(BF16), b_gate.reshape(1, 2 * LRU_W)


def _pad_lanes(v, n=128):
    return jnp.concatenate([v, jnp.zeros((n - v.shape[0],), F32)])[None, :]


def kernel(x, positions, norm_mix, w_in, mla_q_norm, mla_w_uq, mla_kv_norm, mla_w_ukv, nsa_cmp_pe, nsa_w_cmp,
           lru_conv_w, lru_conv_b, lru_w_gate, lru_b_gate, lru_lambda, ssd_conv_w, ssd_conv_b, ssd_dt_bias,
           ssd_a_log, ssd_d, group_norm, w_out, norm_ffn, ffn_w_gate, ffn_w_up, ffn_w_down, moe_router,
           moe_w_gate, moe_w_up, moe_w_down, norm_final):
    B, S, D = x.shape
    T = B * S
    depth = w_in.shape[0]
    assert S // SEL_LEN <= SEL_LANES and S % 512 == 0
    cos_t, sin_t, cos_m, sin_m = _rope_tables(positions)
    ov = _overlap_matrix(S)
    ex = _gate_expand()
    h_res = x.reshape(T, D)
    for l in range(depth):
        ua, qn, kvc, ksel, kwin, vsel, vwin, gate, uc, ud = _inproj(
            h_res, norm_mix[l][None, :], _ext_w_in(w_in[l]), cos_t, sin_t, S)

        q_m, k_m, v_m = _mla_prep(ua, *_mla_weights(mla_w_uq[l], mla_w_ukv[l], mla_q_norm[l], mla_kv_norm[l]),
                                  cos_m, sin_m, B, S)
        y_a = _flash(q_m, k_m, v_m, window=None).reshape(T, GROUP_W)

        kvcmp = _compress(kvc.reshape(B, S // CMP_STRIDE, CMP_STRIDE * 128),
                          *_compress_weights(nsa_cmp_pe[l], nsa_w_cmp[l]))
        o_c, q_aug = _cmp_select(qn, kvcmp, ov, B, S)
        o_s = _flash(q_aug, ksel.reshape(B, 1, S, 256), vsel.reshape(B, 1, S, 128), window=None).reshape(T, GROUP_W)
        o_w = _flash(q_aug, kwin.reshape(B, 1, S, 256), vwin.reshape(B, 1, S, 128),
                     window=WINDOW).reshape(T, GROUP_W)

        wg_l, bg_l = _lru_gate_weights(lru_w_gate[l], lru_b_gate[l])
        y_c = _lru(uc, lru_conv_w[l], lru_conv_b[l][None, :], wg_l, bg_l,
                   jax.nn.softplus(-lru_lambda[l])[None, :], B, S)

        y_d = _ssd(ud, ssd_conv_w[l], ssd_conv_b[l][None, :], _pad_lanes(ssd_dt_bias[l]),
                   _pad_lanes(-jnp.exp(ssd_a_log[l])), jnp.repeat(ssd_d[l], SSD_HD)[None, :],
                   group_norm[l, 3][None, :], B, S)

        moe_layer = l % 2 == 1
        rw = None
        if moe_layer:
            rw = jnp.concatenate([moe_router[l // 2], jnp.zeros((D, 128 - N_EXPERTS), F32)], axis=1)
            rw = jnp.stack(_split2(rw))
        outs = _outproj(y_a, o_c, o_s, o_w, gate, y_c, y_d, h_res, group_norm[l], ex, w_out[l].astype(BF16),
                        norm_ffn[l][None, :], rw)
        if moe_layer:
            h_res, hn, rg = outs
            normed = l == depth - 1
            h_res = _moe(hn, h_res, rg, moe_w_gate[l // 2].astype(BF16), moe_w_up[l // 2].astype(BF16),
                         moe_w_down[l // 2].astype(BF16), norm_final[None, :], normed)
        else:
            normed = False
            h_res, hn = outs
            h_res = _ffn(hn, h_res, ffn_w_gate[l // 2].astype(BF16), ffn_w_up[l // 2].astype(BF16),
                         ffn_w_down[l // 2].astype(BF16))
    if not normed:
        h_res = _final_norm(h_res, norm_final[None, :])
    return h_res.reshape(B, S, D).astype(x.dtype)
```

```python
import functools

import numpy as np
import jax
import jax.numpy as jnp
from jax import lax
from jax.experimental import pallas as pl
from jax.experimental.pallas import tpu as pltpu

F32 = jnp.float32
BF16 = jnp.bfloat16

D_MODEL = 1024
GROUP_W = 256
ROPE_THETA = 500000.0
NORM_EPS = 1e-6
NEG = -1e30
FORCE = 1e4
MLA_HEADS, MLA_NOPE, MLA_ROPE, MLA_V = 4, 64, 32, 64
MLA_Q_RANK, MLA_KV_RANK = 192, 128
NSA_HEADS, NSA_HD, NSA_ROT = 4, 64, 16
CMP_STRIDE, CMP_LEN, SEL_LEN, SEL_TOPN, WINDOW = 16, 32, 64, 16, 512
LRU_W, LRU_BLOCKS, LRU_C, CONV_W = 256, 4, 8.0, 4
SSD_HEADS, SSD_HD, SSD_GROUPS, SSD_STATE, SSD_CHUNK = 4, 64, 2, 128, 128
SSD_DI = 256
SSD_XBC = SSD_DI + 2 * SSD_GROUPS * SSD_STATE
D_FF = 3584
N_EXPERTS = 8

O_NSA = MLA_Q_RANK + MLA_KV_RANK + MLA_ROPE
O_LRU = O_NSA + NSA_HEADS * NSA_HD + 6 * NSA_HD + 3 * NSA_HEADS
O_SSD = O_LRU + 2 * LRU_W
SSD_IN_W = 1152
P_NSA, P_LRU, W_RAW = 384, 1152, 2816
LOG2E = 1.4426950408889634
SEL_LANES = 128

VMEM_LIMIT = 56 * 1024 * 1024


def _cparams(sem):
    return pltpu.CompilerParams(dimension_semantics=sem, vmem_limit_bytes=VMEM_LIMIT)


def _dot(a, b):
    return jnp.dot(a.astype(BF16), b.astype(BF16), preferred_element_type=F32)


def _dot_nt(a, b):
    return lax.dot_general(a.astype(BF16), b.astype(BF16), (((1,), (1,)), ((), ())),
                           preferred_element_type=F32)


def _dot_f32(a, b):
    return jnp.dot(a, b, precision=lax.Precision.HIGHEST, preferred_element_type=F32)


def _dot_f32_nt(a, b):
    return lax.dot_general(a, b, (((1,), (1,)), ((), ())), precision=lax.Precision.HIGHEST,
                           preferred_element_type=F32)


def _split2(x):
    hi = x.astype(BF16)
    return hi, (x - hi.astype(F32)).astype(BF16)


def _sigmoid(x):
    return 1.0 / (1.0 + jnp.exp(-x))


def _softplus(x):
    return jnp.maximum(x, 0.0) + jnp.log(1.0 + jnp.exp(-jnp.abs(x)))


def _rms(x, width):
    return x * lax.rsqrt(jnp.sum(x * x, axis=-1, keepdims=True) * (1.0 / width) + NORM_EPS)


def _rope_tile(x, tab_ref, slot, half):
    t0 = 3 * 128 * slot
    return (x * tab_ref[:, t0:t0 + 128] + pltpu.roll(x, half, axis=1) * tab_ref[:, t0 + 128:t0 + 256]
            + pltpu.roll(x, 128 - half, axis=1) * tab_ref[:, t0 + 256:t0 + 384])


def _inproj_kernel(x_ref, g_ref, w_ref, tab_ref,
                   ua_ref, qn_ref, kvc_ref, ksel_ref, kwin_ref, vsel_ref, vwin_ref, gate_ref, uc_ref, ud_ref,
                   *, tm, seq_blocks):
    x = x_ref[...]
    h = _rms(x, D_MODEL) * g_ref[...]
    y = _dot(h, w_ref[...])
    lane = lax.broadcasted_iota(jnp.int32, (tm, 128), 1)
    lo64 = lane < 64
    ua_ref[:, 0:128] = y[:, 0:128]
    ua_ref[:, 128:256] = jnp.where(lo64, y[:, 128:256], 0.0)
    ua_ref[:, 256:384] = y[:, MLA_Q_RANK:MLA_Q_RANK + MLA_KV_RANK]
    kpe = jnp.where((lane >= 64) & (lane < 64 + MLA_ROPE), y[:, 256:384], 0.0)
    ua_ref[:, 384:512] = _rope_tile(kpe, tab_ref, 0, MLA_ROPE // 2)
    yn = y[:, P_NSA:P_NSA + 768]
    qn_ref[:, 0:128] = _rope_tile(yn[:, 0:128], tab_ref, 1, NSA_ROT // 2)
    qn_ref[:, 128:256] = _rope_tile(yn[:, 128:256], tab_ref, 1, NSA_ROT // 2)
    kvc_ref[...] = _rope_tile(yn[:, 256:384], tab_ref, 2, NSA_ROT // 2)
    s0 = (pl.program_id(0) % seq_blocks) * tm
    pos = s0 + lax.broadcasted_iota(jnp.int32, (tm, SEL_LANES), 0)
    onehot = jnp.where(jnp.right_shift(pos, 6) == lane, 1.0, 0.0)
    for k_ref, v_ref, c0, extra in ((ksel_ref, vsel_ref, 384, onehot), (kwin_ref, vwin_ref, 512, None)):
        kv = yn[:, c0:c0 + 128]
        k_ref[:, 0:128] = jnp.where(lo64, _rope_tile(kv, tab_ref, 2, NSA_ROT // 2), 0.0).astype(BF16)
        k_ref[:, 128:256] = (jnp.zeros((tm, 128), F32) if extra is None else extra).astype(BF16)
        v_ref[...] = jnp.where(lo64, pltpu.roll(kv, 64, axis=1), 1.0).astype(BF16)
    gate_ref[...] = jnp.where(lane < 3 * NSA_HEADS, yn[:, 640:768], 0.0)
    yl = y[:, P_LRU:P_LRU + 512 + SSD_IN_W]
    uc_ref[...] = yl[:, 0:512]
    ud_ref[:, 0:SSD_IN_W - 128] = yl[:, 512:512 + SSD_IN_W - 128]
    ud_ref[:, SSD_IN_W - 128:SSD_IN_W] = jnp.where(lane < SSD_HEADS, yl[:, 512 + SSD_IN_W - 128:512 + SSD_IN_W], 0.0)


def _inproj(x2d, g, w_raw, tab, seq):
    T = x2d.shape[0]
    tm = 256
    seq_blocks = seq // tm
    row = lambda w: pl.BlockSpec((tm, w), lambda i: (i, 0))
    full = lambda a: pl.BlockSpec(a.shape, lambda i: (0,) * a.ndim)
    outs = [(512, F32), (256, F32), (128, F32), (256, BF16), (256, BF16), (128, BF16), (128, BF16), (128, F32),
            (512, F32), (SSD_IN_W, F32)]
    return pl.pallas_call(
        functools.partial(_inproj_kernel, tm=tm, seq_blocks=seq_blocks),
        grid=(T // tm,),
        in_specs=[row(D_MODEL), full(g), full(w_raw), pl.BlockSpec((tm, tab.shape[1]), lambda i: (i % seq_blocks, 0))],
        out_specs=[row(w) for w, _ in outs],
        out_shape=[jax.ShapeDtypeStruct((T, w), dt) for w, dt in outs],
        compiler_params=_cparams(("parallel",)),
        name="inproj",
    )(x2d, g, w_raw, tab)


def _mla_prep_kernel(ua_ref, qg_ref, w1_ref, w2_ref, kvg_ref, wk_ref, wv_ref, cos_ref, sin_ref,
                     q_ref, k_ref, v_ref):
    ua = ua_ref[...]
    cq = _rms(ua[:, 0:256], MLA_Q_RANK) * qg_ref[...]
    y1 = _dot(cq, w1_ref[...])
    y2 = _dot(cq, w2_ref[...])
    ckv = _rms(ua[:, 256:384], MLA_KV_RANK) * kvg_ref[...]
    kn = _dot(ckv, wk_ref[...])
    vv = _dot(ckv, wv_ref[...])
    kpe = ua[:, 384:512]
    cos = cos_ref[...]
    sin = sin_ref[...]
    scale = LOG2E * (MLA_NOPE + MLA_ROPE) ** -0.5
    ones_hi = jnp.where(lax.broadcasted_iota(jnp.int32, cos.shape, 1) >= 64, 1.0, 0.0)
    for h in range(MLA_HEADS):
        sl = slice(h * 128, (h + 1) * 128)
        q_ref[h] = ((y1[:, sl] * cos + y2[:, sl] * sin) * scale).astype(BF16)
        k_ref[h] = (kn[:, sl] + kpe).astype(BF16)
        v_ref[h] = (vv[:, sl] + ones_hi).astype(BF16)


def _mla_prep(ua, qg, w1, w2, kvg, wk, wv, cos_m, sin_m, B, S):
    tm = 512
    nb = S // tm
    full = lambda a: pl.BlockSpec(a.shape, lambda b, i: (0,) * a.ndim)
    tab = pl.BlockSpec((tm, 128), lambda b, i: (i, 0))
    hd = lambda w: pl.BlockSpec((None, MLA_HEADS, tm, w), lambda b, i: (b, 0, i, 0))
    return pl.pallas_call(
        _mla_prep_kernel,
        grid=(B, nb),
        in_specs=[pl.BlockSpec((tm, 512), lambda b, i: (b * nb + i, 0)),
                  full(qg), full(w1), full(w2), full(kvg), full(wk), full(wv), tab, tab],
        out_specs=[hd(128), hd(128), hd(128)],
        out_shape=[jax.ShapeDtypeStruct((B, MLA_HEADS, S, 128), BF16)] * 3,
        compiler_params=_cparams(("parallel", "parallel")),
        name="mla_prep",
    )(ua, qg, w1, w2, kvg, wk, wv, cos_m, sin_m)


def _flash_kernel(qi_ref, ki_ref, flag_ref, q_ref, k_ref, v_ref, o_ref, m_ref, acc_ref,
                  *, heads, kv_heads, t, window):
    p_idx = pl.program_id(1)
    qi = qi_ref[p_idx]
    ki = ki_ref[p_idx]
    flags = flag_ref[p_idx]

    @pl.when(jnp.bitwise_and(flags, 1) != 0)
    def _init():
        m_ref[...] = jnp.full(m_ref.shape, NEG, F32)
        acc_ref[...] = jnp.zeros(acc_ref.shape, F32)

    def step(masked):
        if masked:
            qpos = qi * t + lax.broadcasted_iota(jnp.int32, (t, t), 0)
            kpos = ki * t + lax.broadcasted_iota(jnp.int32, (t, t), 1)
            mask = kpos <= qpos
            if window is not None:
                mask = mask & (kpos > qpos - window)
        for h in range(heads):
            hk = h if kv_heads > 1 else 0
            s = lax.dot_general(q_ref[h], k_ref[hk], (((1,), (1,)), ((), ())), preferred_element_type=F32)
            if masked:
                s = jnp.where(mask, s, NEG)
            m_old = m_ref[h]
            m_new = jnp.maximum(m_old, jnp.max(s, axis=-1, keepdims=True))
            alpha = jnp.exp2(m_old - m_new)
            p = jnp.concatenate([jnp.exp2(s[:, c * 128:(c + 1) * 128] - m_new) for c in range(t // 128)],
                                axis=1).astype(BF16)
            acc_ref[h] = alpha * acc_ref[h] + jnp.dot(p, v_ref[hk], preferred_element_type=F32)
            m_ref[h] = m_new

    @pl.when(jnp.bitwise_and(flags, 4) != 0)
    def _masked():
        step(True)

    @pl.when(jnp.bitwise_and(flags, 4) == 0)
    def _plain():
        step(False)

    @pl.when(jnp.bitwise_and(flags, 2) != 0)
    def _fin():
        for h in range(heads):
            acc = acc_ref[h]
            o_ref[:, h * 64:(h + 1) * 64] = acc[:, 0:64] / acc[:, 64:128]


def _pair_tables(nq, back):
    qi, ki, flags = [], [], []
    for i in range(nq):
        lo = 0 if back is None else max(0, i - back)
        for j in range(lo, i + 1):
            qi.append(i)
            ki.append(j)
            masked = (j == i) or (back is not None)
            flags.append((1 if j == lo else 0) | (2 if j == i else 0) | (4 if masked else 0))
    mk = lambda a: jnp.asarray(np.asarray(a, np.int32))
    return mk(qi), mk(ki), mk(flags)


def _flash(q, k, v, *, window, t=512):
    B, H, S, dk = q.shape
    Hk = k.shape[1]
    t = min(t, S)
    nq = S // t
    back = None if window is None else -(-window // t)
    tabs = _pair_tables(nq, back)
    npairs = int(tabs[0].shape[0])
    grid_spec = pltpu.PrefetchScalarGridSpec(
        num_scalar_prefetch=3,
        grid=(B, npairs),
        in_specs=[pl.BlockSpec((None, H, t, dk), lambda b, p, qi, ki, fl: (b, 0, qi[p], 0)),
                  pl.BlockSpec((None, Hk, t, dk), lambda b, p, qi, ki, fl: (b, 0, ki[p], 0)),
                  pl.BlockSpec((None, Hk, t, 128), lambda b, p, qi, ki, fl: (b, 0, ki[p], 0))],
        out_specs=pl.BlockSpec((None, t, H * 64), lambda b, p, qi, ki, fl: (b, qi[p], 0)),
        scratch_shapes=[pltpu.VMEM((H, t, 128), F32), pltpu.VMEM((H, t, 128), F32)],
    )
    return pl.pallas_call(
        functools.partial(_flash_kernel, heads=H, kv_heads=Hk, t=t, window=window),
        grid_spec=grid_spec,
        out_shape=jax.ShapeDtypeStruct((B, S, H * 64), F32),
        compiler_params=_cparams(("parallel", "arbitrary")),
        name="flash_attn",
    )(*tabs, q, k, v)


def _compress_kernel(x_ref, wa_ref, wb_ref, pek_ref, pev_ref, wkf_ref, wvf_ref, o_ref, *, nc_pad):
    x = x_ref[...]
    a = _dot_f32(x, wa_ref[...])
    b = _dot_f32(x, wb_ref[...])
    ck = _dot_f32(pek_ref[...], wkf_ref[...])[0:1]
    cv = _dot_f32(pev_ref[...], wvf_ref[...])[0:1]
    const = jnp.concatenate([ck, cv], axis=-1)
    b_next = pltpu.roll(b, nc_pad - 1, axis=0)
    row = lax.broadcasted_iota(jnp.int32, (nc_pad, 128), 0)
    o_ref[...] = jnp.where(row < nc_pad - 1, a + b_next + const, 0.0)


def _compress(x, wa, wb, pek, pev, wkf, wvf):
    B, nc_pad, w = x.shape
    full = lambda a: pl.BlockSpec(a.shape, lambda b: (0,) * a.ndim)
    return pl.pallas_call(
        functools.partial(_compress_kernel, nc_pad=nc_pad),
        grid=(B,),
        in_specs=[pl.BlockSpec((None, nc_pad, w), lambda b: (b, 0, 0)),
                  full(wa), full(wb), full(pek), full(pev), full(wkf), full(wvf)],
        out_specs=pl.BlockSpec((None, nc_pad, 128), lambda b: (b, 0, 0)),
        out_shape=jax.ShapeDtypeStruct((B, nc_pad, 128), F32),
        compiler_params=_cparams(("parallel",)),
        name="nsa_compress",
    )(x, wa, wb, pek, pev, wkf, wvf)


def _cmp_select_kernel(q_ref, kvc_ref, ov_ref, oc_ref, qaug_ref, *, tq, nc_pad, n_top):
    i = pl.program_id(1)
    q = q_ref[...]
    kc = kvc_ref[:, 0:64]
    vc = kvc_ref[:, 64:128]
    scale = NSA_HD ** -0.5
    qpos = i * tq + lax.broadcasted_iota(jnp.int32, (tq, 1), 0)
    n_idx = lax.broadcasted_iota(jnp.int32, (1, nc_pad), 1)
    m_c = (n_idx * CMP_STRIDE + (CMP_LEN - 1) <= qpos) & (n_idx < nc_pad - 1)
    kc_hi, kc_lo = _split2(kc)
    kc3 = jnp.concatenate([kc_hi, kc_lo, kc_hi], axis=1)
    psum = jnp.zeros((tq, nc_pad), F32)
    for h in range(NSA_HEADS):
        q_hi, q_lo = _split2(q[:, h * NSA_HD:(h + 1) * NSA_HD])
        q3 = jnp.concatenate([q_hi, q_hi, q_lo], axis=1)
        s = lax.dot_general(q3, kc3, (((1,), (1,)), ((), ())), preferred_element_type=F32)
        s = jnp.where(m_c, s * scale, NEG)
        e = jnp.where(m_c, jnp.exp(s - jnp.max(s, axis=-1, keepdims=True)), 0.0)
        den = jnp.sum(e, axis=-1, keepdims=True)
        p = e / jnp.where(den > 0.0, den, 1.0)
        oc_ref[:, h * NSA_HD:(h + 1) * NSA_HD] = _dot(p, vc)
        psum = psum + p
    ov = ov_ref[...]
    p_hi = psum.astype(BF16)
    p_r = psum - p_hi.astype(F32)
    p_mid = p_r.astype(BF16)
    p_lo = (p_r - p_mid.astype(F32)).astype(BF16)
    imp = (jnp.dot(p_hi, ov, preferred_element_type=F32) + jnp.dot(p_mid, ov, preferred_element_type=F32)
           + jnp.dot(p_lo, ov, preferred_element_type=F32))
    cur = jnp.right_shift(qpos, 6)
    jj = lax.broadcasted_iota(jnp.int32, (1, SEL_LANES), 1)
    forced = (jj == 0) | (jj == cur) | (jj == cur - 1)
    bias = jnp.where(forced, 0.0, NEG)
    imp = jnp.where((jj <= cur) & jnp.logical_not(forced), imp, NEG)
    jf = jj.astype(F32)
    for _ in range(n_top - 3):
        mx = jnp.max(imp, axis=-1, keepdims=True)
        idx = jnp.min(jnp.where(imp == mx, jf, float(SEL_LANES)), axis=-1, keepdims=True)
        hit = jf == idx
        bias = jnp.where(hit & (mx > 0.5 * NEG), 0.0, bias)
        imp = jnp.where(hit, -3e38, imp)
    bias = bias.astype(BF16)
    qs = (q * (scale * LOG2E)).astype(BF16)
    for h in range(NSA_HEADS):
        qaug_ref[h, :, 0:64] = qs[:, h * NSA_HD:(h + 1) * NSA_HD]
        qaug_ref[h, :, 64:128] = jnp.zeros((tq, 64), BF16)
        qaug_ref[h, :, 128:256] = bias


def _cmp_select(qn, kvcmp, ov, B, S):
    tq = min(1024, S)
    nb = S // tq
    nc_pad = kvcmp.shape[1]
    n_top = min(SEL_TOPN, S // SEL_LEN)
    assert n_top >= 3
    return pl.pallas_call(
        functools.partial(_cmp_select_kernel, tq=tq, nc_pad=nc_pad, n_top=n_top),
        grid=(B, nb),
        in_specs=[pl.BlockSpec((tq, 256), lambda b, i: (b * nb + i, 0)),
                  pl.BlockSpec((None, nc_pad, 128), lambda b, i: (b, 0, 0)),
                  pl.BlockSpec(ov.shape, lambda b, i: (0, 0))],
        out_specs=[pl.BlockSpec((tq, 256), lambda b, i: (b * nb + i, 0)),
                   pl.BlockSpec((None, NSA_HEADS, tq, 256), lambda b, i: (b, 0, i, 0))],
        out_shape=[jax.ShapeDtypeStruct((B * S, 256), F32),
                   jax.ShapeDtypeStruct((B, NSA_HEADS, S, 256), BF16)],
        compiler_params=_cparams(("parallel", "parallel")),
        name="nsa_cmp_select",
    )(qn, kvcmp, ov)


def _shift_scan(a, b, t, width):
    row = lax.broadcasted_iota(jnp.int32, (t, width), 0)
    s = 1
    while s < t:
        keep = row >= s
        a_sh = jnp.where(keep, pltpu.roll(a, s, axis=0), 1.0)
        b_sh = jnp.where(keep, pltpu.roll(b, s, axis=0), 0.0)
        b = a * b_sh + b
        a = a * a_sh
        s *= 2
    return a, b


def _causal_conv(xpad_ref, x, cw_ref, cb_ref, t):
    xpad_ref[8:8 + t, :] = x
    y = cb_ref[...] + cw_ref[CONV_W - 1:CONV_W, :] * x
    for k in range(CONV_W - 1):
        off = 8 - (CONV_W - 1) + k
        y = y + cw_ref[k:k + 1, :] * xpad_ref[off:off + t, :]
    xpad_ref[0:8, :] = x[t - 8:t, :]
    return y


def _lru_kernel(u_ref, cw_ref, cb_ref, wg_ref, bg_ref, sp_ref, y_ref, xpad_ref, h_ref, *, t):
    @pl.when(pl.program_id(1) == 0)
    def _init():
        xpad_ref[0:8, :] = jnp.zeros((8, LRU_W), F32)
        h_ref[...] = jnp.zeros(h_ref.shape, F32)

    x = u_ref[:, 0:LRU_W]
    gbr = u_ref[:, LRU_W:2 * LRU_W]
    xb = _causal_conv(xpad_ref, x, cw_ref, cb_ref, t)
    g = _dot(xb, wg_ref[...]) + bg_ref[...]
    r = _sigmoid(g[:, 0:LRU_W])
    ig = _sigmoid(g[:, LRU_W:2 * LRU_W])
    log_a = -LRU_C * r * sp_ref[...]
    a = jnp.exp(log_a)
    one_m = -jnp.tanh(log_a) * (a * a + 1.0)
    b = jnp.sqrt(jnp.maximum(one_m, 0.0)) * (ig * xb)
    a_cum, h = _shift_scan(a, b, t, LRU_W)
    h = h + a_cum * h_ref[0:1, :]
    h_ref[0:1, :] = h[t - 1:t, :]
    gelu = 0.5 * gbr * (1.0 + jnp.tanh(0.7978845608028654 * (gbr + 0.044715 * gbr * gbr * gbr)))
    y_ref[...] = h * gelu


def _lru(uc, cw, cb, wg, bg, sp, B, S):
    t = 512
    nb = S // t
    full = lambda a: pl.BlockSpec(a.shape, lambda b, i: (0,) * a.ndim)
    return pl.pallas_call(
        functools.partial(_lru_kernel, t=t),
        grid=(B, nb),
        in_specs=[pl.BlockSpec((t, 2 * LRU_W), lambda b, i: (b * nb + i, 0)),
                  full(cw), full(cb), full(wg), full(bg), full(sp)],
        out_specs=pl.BlockSpec((t, LRU_W), lambda b, i: (b * nb + i, 0)),
        out_shape=jax.ShapeDtypeStruct((B * S, LRU_W), F32),
        scratch_shapes=[pltpu.VMEM((t + 8, LRU_W), F32), pltpu.VMEM((8, LRU_W), F32)],
        compiler_params=_cparams(("parallel", "arbitrary")),
        name="rglru",
    )(uc, cw, cb, wg, bg, sp)


def _ssd_kernel(u_ref, cw_ref, cb_ref, dtb_ref, a_ref, d_ref, ng_ref, y_ref, xpad_ref, st_ref, *, t):
    @pl.when(pl.program_id(1) == 0)
    def _init():
        xpad_ref[0:8, :] = jnp.zeros((8, SSD_XBC), F32)
        st_ref[...] = jnp.zeros(st_ref.shape, F32)

    L = SSD_CHUNK
    z = u_ref[:, 0:SSD_DI]
    conv = _causal_conv(xpad_ref, u_ref[:, SSD_DI:SSD_DI + SSD_XBC], cw_ref, cb_ref, t)
    xbc = conv * _sigmoid(conv)
    dt = _softplus(u_ref[:, SSD_DI + SSD_XBC:SSD_IN_W] + dtb_ref[...])
    a = dt * a_ref[...]
    row = lax.broadcasted_iota(jnp.int32, (t, 128), 0)
    rin = jnp.bitwise_and(row, L - 1)
    cs = a
    s = 1
    while s < L:
        cs = cs + jnp.where(rin >= s, pltpu.roll(cs, s, axis=0), 0.0)
        s *= 2
    tril = lax.broadcasted_iota(jnp.int32, (L, L), 0) >= lax.broadcasted_iota(jnp.int32, (L, L), 1)
    gn = SSD_GROUPS * SSD_STATE
    rep = SSD_HEADS // SSD_GROUPS
    for c in range(t // L):
        rs = slice(c * L, (c + 1) * L)
        cs_c = cs[rs]
        cs_t = cs_c.T
        cs_last = cs_c[L - 1:L, :]
        ys = []
        for g in range(SSD_GROUPS):
            bg = xbc[rs, SSD_DI + g * SSD_STATE:SSD_DI + (g + 1) * SSD_STATE]
            cg = xbc[rs, SSD_DI + gn + g * SSD_STATE:SSD_DI + gn + (g + 1) * SSD_STATE]
            gmat = _dot_nt(cg, bg)
            bg_t = bg.T
            for hh in range(rep):
                h = g * rep + hh
                cs_col = cs_c[:, h:h + 1]
                lm = jnp.exp(jnp.where(tril, cs_col - cs_t[h:h + 1, :], NEG))
                xh = xbc[rs, h * SSD_HD:(h + 1) * SSD_HD]
                xdt = xh * dt[rs, h:h + 1]
                st_old = st_ref[h]
                y_h = _dot(gmat * lm, xdt) + _dot(cg, st_old) * jnp.exp(cs_col)
                dec = jnp.exp(cs_last[:, h:h + 1] - cs_col)
                st_ref[h] = jnp.exp(cs_last[:, h:h + 1]) * st_old + _dot(bg_t, xdt * dec)
                ys.append(y_h)
        y = jnp.concatenate(ys, axis=-1) + xbc[rs, 0:SSD_DI] * d_ref[...]
        zc = z[rs]
        y = y * (zc * _sigmoid(zc))
        y_ref[rs, :] = _rms(y, SSD_DI) * ng_ref[...]


def _ssd(ud, cw, cb, dtb, a_neg, d_vec, ng, B, S):
    t = 512
    nb = S // t
    full = lambda a: pl.BlockSpec(a.shape, lambda b, i: (0,) * a.ndim)
    return pl.pallas_call(
        functools.partial(_ssd_kernel, t=t),
        grid=(B, nb),
        in_specs=[pl.BlockSpec((t, SSD_IN_W), lambda b, i: (b * nb + i, 0)),
                  full(cw), full(cb), full(dtb), full(a_neg), full(d_vec), full(ng)],
        out_specs=pl.BlockSpec((t, SSD_DI), lambda b, i: (b * nb + i, 0)),
        out_shape=jax.ShapeDtypeStruct((B * S, SSD_DI), F32),
        scratch_shapes=[pltpu.VMEM((t + 8, SSD_XBC), F32), pltpu.VMEM((SSD_HEADS, SSD_STATE, SSD_HD), F32)],
        compiler_params=_cparams(("parallel", "arbitrary")),
        name="ssd",
    )(ud, cw, cb, dtb, a_neg, d_vec, ng)


def _outproj_kernel(*refs, with_router):
    if with_router:
        (ya_ref, oc_ref, os_ref, ow_ref, gate_ref, yc_ref, yd_ref, res_ref, gn_ref, ex_ref, w_ref, nf_ref,
         rw_ref, hres_ref, hn_ref, rg_ref) = refs
    else:
        (ya_ref, oc_ref, os_ref, ow_ref, gate_ref, yc_ref, yd_ref, res_ref, gn_ref, ex_ref, w_ref, nf_ref,
         hres_ref, hn_ref) = refs
    sg_hi, sg_lo = _split2(_sigmoid(gate_ref[...]))
    ex = ex_ref[...]
    gx = jnp.dot(sg_hi, ex, preferred_element_type=F32) + jnp.dot(sg_lo, ex, preferred_element_type=F32)
    yb = gx[:, 0:256] * oc_ref[...] + gx[:, 256:512] * os_ref[...] + gx[:, 512:768] * ow_ref[...]
    y = jnp.concatenate([_rms(ya_ref[...], GROUP_W) * gn_ref[0:1, :],
                         _rms(yb, GROUP_W) * gn_ref[1:2, :],
                         _rms(yc_ref[...], GROUP_W) * gn_ref[2:3, :],
                         yd_ref[...]], axis=-1)
    hres = res_ref[...] + _dot(y, w_ref[...])
    hres_ref[...] = hres
    hn = _rms(hres, D_MODEL) * nf_ref[...]
    hn_ref[...] = hn.astype(hn_ref.dtype)
    if with_router:
        h_hi, h_lo = _split2(hn)
        logits = (jnp.dot(h_hi, rw_ref[0], preferred_element_type=F32)
                  + jnp.dot(h_hi, rw_ref[1], preferred_element_type=F32)
                  + jnp.dot(h_lo, rw_ref[0], preferred_element_type=F32))
        lane = lax.broadcasted_iota(jnp.int32, logits.shape, 1)
        lf = lane.astype(F32)
        logits = jnp.where(lane < N_EXPERTS, logits, NEG)
        m1 = jnp.max(logits, axis=-1, keepdims=True)
        i1 = jnp.min(jnp.where(logits == m1, lf, 128.0), axis=-1, keepdims=True)
        rest = jnp.where(lf == i1, NEG, logits)
        m2 = jnp.max(rest, axis=-1, keepdims=True)
        i2 = jnp.min(jnp.where(rest == m2, lf, 128.0), axis=-1, keepdims=True)
        e2 = jnp.exp(m2 - m1)
        den = 1.0 + e2
        rg_ref[...] = jnp.where(lane == 0, i1, jnp.where(lane == 1, i2, jnp.where(
            lane == 2, 1.0 / den, jnp.where(lane == 3, e2 / den, 0.0))))


def _outproj(ya, oc, osel, ow, gate, yc, yd, res, gn, ex, w, nf, rw):
    T = ya.shape[0]
    tm = 512
    with_router = rw is not None
    row = lambda wd: pl.BlockSpec((tm, wd), lambda i: (i, 0))
    full = lambda a: pl.BlockSpec(a.shape, lambda i: (0,) * a.ndim)
    ins = [ya, oc, osel, ow, gate, yc, yd, res, gn, ex, w, nf]
    in_specs = [row(256), row(256), row(256), row(256), row(128), row(256), row(256), row(D_MODEL),
                full(gn), full(ex), full(w), full(nf)]
    out_specs = [row(D_MODEL), row(D_MODEL)]
    out_shape = [jax.ShapeDtypeStruct((T, D_MODEL), F32),
                 jax.ShapeDtypeStruct((T, D_MODEL), F32 if with_router else BF16)]
    if with_router:
        ins.append(rw)
        in_specs.append(full(rw))
        out_specs.append(row(128))
        out_shape.append(jax.ShapeDtypeStruct((T, 128), F32))
    return pl.pallas_call(
        functools.partial(_outproj_kernel, with_router=with_router),
        grid=(T // tm,),
        in_specs=in_specs,
        out_specs=out_specs,
        out_shape=out_shape,
        compiler_params=_cparams(("parallel",)),
        name="outproj",
    )(*ins)


def _ffn_kernel(h_ref, res_ref, wg_ref, wu_ref, wd_ref, o_ref, acc_ref):
    j = pl.program_id(1)

    @pl.when(j == 0)
    def _init():
        acc_ref[...] = jnp.zeros(acc_ref.shape, F32)

    h = h_ref[...]
    g = jnp.dot(h, wg_ref[...], preferred_element_type=F32)
    u = jnp.dot(h, wu_ref[...], preferred_element_type=F32)
    acc_ref[...] += _dot(g * _sigmoid(g) * u, wd_ref[...])

    @pl.when(j == pl.num_programs(1) - 1)
    def _fin():
        o_ref[...] = res_ref[...] + acc_ref[...]


def _ffn(hn, res, wg, wu, wd):
    T = hn.shape[0]
    tm, tf = 1024, 512
    return pl.pallas_call(
        _ffn_kernel,
        grid=(T // tm, D_FF // tf),
        in_specs=[pl.BlockSpec((tm, D_MODEL), lambda i, j: (i, 0)),
                  pl.BlockSpec((tm, D_MODEL), lambda i, j: (i, 0)),
                  pl.BlockSpec((D_MODEL, tf), lambda i, j: (0, j)),
                  pl.BlockSpec((D_MODEL, tf), lambda i, j: (0, j)),
                  pl.BlockSpec((tf, D_MODEL), lambda i, j: (j, 0))],
        out_specs=pl.BlockSpec((tm, D_MODEL), lambda i, j: (i, 0)),
        out_shape=jax.ShapeDtypeStruct((T, D_MODEL), F32),
        scratch_shapes=[pltpu.VMEM((tm, D_MODEL), F32)],
        compiler_params=_cparams(("parallel", "arbitrary")),
        name="ffn_dense",
    )(hn, res, wg, wu, wd)


MOE_TILE = 1024
MOE_TM = 512


def _route_rank_kernel(route_ref, rank_ref, cnt_ref, carry_ref, *, tm):
    @pl.when(pl.program_id(0) == 0)
    def _init():
        carry_ref[...] = jnp.zeros(carry_ref.shape, F32)

    r = route_ref[...]
    i1 = r[:, 0:1]
    i2 = r[:, 1:2]
    lane = lax.broadcasted_iota(jnp.int32, (tm, 128), 1)
    lf = lane.astype(F32)
    oh = jnp.where((lf == i1) | (lf == i2), 1.0, 0.0)
    row = lax.broadcasted_iota(jnp.int32, (tm, 128), 0)
    cs = oh
    s = 1
    while s < tm:
        cs = cs + jnp.where(row >= s, pltpu.roll(cs, s, axis=0), 0.0)
        s *= 2
    excl = cs - oh + carry_ref[0:1, :]
    rank1 = jnp.sum(jnp.where(lf == i1, excl, 0.0), axis=-1, keepdims=True)
    rank2 = jnp.sum(jnp.where(lf == i2, excl, 0.0), axis=-1, keepdims=True)
    rank_ref[...] = jnp.where(lane == 0, rank1, jnp.where(lane == 1, rank2, 0.0))
    carry_ref[0:1, :] = carry_ref[0:1, :] + cs[tm - 1:tm, :]
    cnt_ref[...] = carry_ref[...]


def _route_rank(route):
    T = route.shape[0]
    tm = MOE_TM
    return pl.pallas_call(
        functools.partial(_route_rank_kernel, tm=tm),
        grid=(T // tm,),
        in_specs=[pl.BlockSpec((tm, 128), lambda i: (i, 0))],
        out_specs=[pl.BlockSpec((tm, 128), lambda i: (i, 0)), pl.BlockSpec((8, 128), lambda i: (0, 0))],
        out_shape=[jax.ShapeDtypeStruct((T, 128), F32), jax.ShapeDtypeStruct((8, 128), F32)],
        scratch_shapes=[pltpu.VMEM((8, 128), F32)],
        compiler_params=_cparams(("arbitrary",)),
        name="moe_rank",
    )(route)


def _row_copies(idx_ref, tm, make_copy):
    def body(r, carry):
        for k in range(2):
            make_copy(k, r, idx_ref[k * tm + r]).start(priority=k)
        return carry

    lax.fori_loop(0, tm, body, 0, unroll=8)


def _dispatch_kernel(ends_ref, slots_ref, h_ref, xs_ref, idx_ref, zero_ref, sem_idx, sem_row, *, tm):
    i = pl.program_id(0)

    @pl.when(i == 0)
    def _zero_tiles():
        zero_ref[...] = jnp.zeros(zero_ref.shape, F32)
        zr = zero_ref.shape[0]
        jobs = []
        for e in range(N_EXPERTS):
            start = ends_ref[e - 1] if e else 0
            jobs.append((ends_ref[e] > start, ends_ref[e] - MOE_TILE))
        for tile in range(N_EXPERTS):
            base = xs_ref.shape[0] - (tile + 1) * MOE_TILE
            jobs.append((base >= ends_ref[N_EXPERTS - 1], base))

        def tile_copies(base):
            rows = [base + q * zr for q in range(MOE_TILE // zr)]
            rows = [r if isinstance(r, int) else pl.multiple_of(r, zr) for r in rows]
            return [pltpu.make_async_copy(zero_ref, xs_ref.at[pl.ds(r, zr)], sem_idx) for r in rows]

        for cond, base in jobs:
            @pl.when(cond)
            def _start(base=base):
                for cp in tile_copies(base):
                    cp.start()
        for cond, base in jobs:
            @pl.when(cond)
            def _wait(base=base):
                for cp in tile_copies(base):
                    cp.wait()

    cp = pltpu.make_async_copy(slots_ref.at[pl.ds(i * 2 * tm, 2 * tm)], idx_ref, sem_idx)
    cp.start()
    cp.wait()
    _row_copies(idx_ref, tm, lambda k, r, s: pltpu.make_async_copy(
        h_ref.at[pl.ds(r, 1)], xs_ref.at[pl.ds(s, 1)], sem_row))
    for _ in range(2):
        pltpu.make_async_copy(h_ref, xs_ref.at[pl.ds(0, tm)], sem_row).wait()


def _dispatch(ends, slots, hn, n_slots):
    T = hn.shape[0]
    tm = MOE_TM
    grid_spec = pltpu.PrefetchScalarGridSpec(
        num_scalar_prefetch=1,
        grid=(T // tm,),
        in_specs=[pl.BlockSpec(memory_space=pl.ANY),
                  pl.BlockSpec((tm, D_MODEL), lambda i, ends: (i, 0))],
        out_specs=pl.BlockSpec(memory_space=pl.ANY),
        scratch_shapes=[pltpu.SMEM((2 * tm,), jnp.int32), pltpu.VMEM((256, D_MODEL), F32),
                        pltpu.SemaphoreType.DMA, pltpu.SemaphoreType.DMA],
    )
    return pl.pallas_call(
        functools.partial(_dispatch_kernel, tm=tm),
        grid_spec=grid_spec,
        out_shape=jax.ShapeDtypeStruct((n_slots, D_MODEL), F32),
        compiler_params=_cparams(("arbitrary",)),
        name="moe_dispatch",
    )(ends, slots, hn)


def _expert_ffn_kernel(te_ref, nu_ref, x_ref, wg_ref, wu_ref, wd_ref, o_ref, acc_ref):
    del te_ref
    i = pl.program_id(0)
    j = pl.program_id(1)
    last = pl.num_programs(1) - 1
    used = i < nu_ref[0]

    @pl.when(used & (j == 0))
    def _init():
        acc_ref[...] = jnp.zeros(acc_ref.shape, F32)

    @pl.when(used)
    def _step():
        x = x_ref[...]
        g = _dot(x, wg_ref[...])
        u = _dot(x, wu_ref[...])
        acc_ref[...] += _dot(g * _sigmoid(g) * u, wd_ref[...])

    @pl.when(used & (j == last))
    def _fin():
        o_ref[...] = acc_ref[...]

    @pl.when(jnp.logical_not(used) & (j == last))
    def _unused():
        o_ref[...] = jnp.zeros(o_ref.shape, F32)


def _expert_ffn(tile_expert, n_used, xs, wg, wu, wd):
    n_slots = xs.shape[0]
    tm, tf = MOE_TILE, 512
    jw = lambda i, j, nu: jnp.where(i < nu[0], j, 0)
    grid_spec = pltpu.PrefetchScalarGridSpec(
        num_scalar_prefetch=2,
        grid=(n_slots // tm, D_FF // tf),
        in_specs=[pl.BlockSpec((tm, D_MODEL), lambda i, j, te, nu: (jnp.minimum(i, nu[0] - 1), 0)),
                  pl.BlockSpec((None, D_MODEL, tf), lambda i, j, te, nu: (te[i], 0, jw(i, j, nu))),
                  pl.BlockSpec((None, D_MODEL, tf), lambda i, j, te, nu: (te[i], 0, jw(i, j, nu))),
                  pl.BlockSpec((None, tf, D_MODEL), lambda i, j, te, nu: (te[i], jw(i, j, nu), 0))],
        out_specs=pl.BlockSpec((tm, D_MODEL), lambda i, j, te, nu: (i, 0)),
        scratch_shapes=[pltpu.VMEM((tm, D_MODEL), F32)],
    )
    return pl.pallas_call(
        _expert_ffn_kernel,
        grid_spec=grid_spec,
        out_shape=jax.ShapeDtypeStruct((n_slots, D_MODEL), F32),
        compiler_params=_cparams(("parallel", "arbitrary")),
        name="moe_expert_ffn",
    )(tile_expert, n_used, xs, wg, wu, wd)


def _combine_kernel(slots_ref, ys_ref, route_ref, res_ref, ng_ref, o_ref, idx_ref, buf_ref, sem_idx, sem_row,
                    *, tm, final_norm):
    i = pl.program_id(0)
    cp = pltpu.make_async_copy(slots_ref.at[pl.ds(i * 2 * tm, 2 * tm)], idx_ref, sem_idx)
    cp.start()
    cp.wait()
    _row_copies(idx_ref, tm, lambda k, r, s: pltpu.make_async_copy(
        ys_ref.at[pl.ds(s, 1)], buf_ref.at[k, pl.ds(r, 1)], sem_row))
    for k in range(2):
        pltpu.make_async_copy(ys_ref.at[pl.ds(0, tm)], buf_ref.at[k], sem_row).wait()
    route = route_ref[...]
    out = res_ref[...] + route[:, 2:3] * buf_ref[0] + route[:, 3:4] * buf_ref[1]
    if final_norm:
        out = _rms(out, D_MODEL) * ng_ref[...]
    o_ref[...] = out


def _combine(slots, ys, route, res, norm_g, final_norm):
    T = res.shape[0]
    tm = MOE_TM
    return pl.pallas_call(
        functools.partial(_combine_kernel, tm=tm, final_norm=final_norm),
        grid=(T // tm,),
        in_specs=[pl.BlockSpec(memory_space=pl.ANY),
                  pl.BlockSpec(memory_space=pl.ANY),
                  pl.BlockSpec((tm, 128), lambda i: (i, 0)),
                  pl.BlockSpec((tm, D_MODEL), lambda i: (i, 0)),
                  pl.BlockSpec((1, D_MODEL), lambda i: (0, 0))],
        out_specs=pl.BlockSpec((tm, D_MODEL), lambda i: (i, 0)),
        out_shape=jax.ShapeDtypeStruct((T, D_MODEL), F32),
        scratch_shapes=[pltpu.SMEM((2 * tm,), jnp.int32), pltpu.VMEM((2, tm, D_MODEL), F32),
                        pltpu.SemaphoreType.DMA, pltpu.SemaphoreType.DMA],
        compiler_params=_cparams(("arbitrary",)),
        name="moe_combine",
    )(slots, ys, route, res, norm_g)


def _moe(hn, res, route, wg, wu, wd, norm_g, final_norm):
    T = hn.shape[0]
    n_slots = 2 * T + N_EXPERTS * MOE_TILE
    n_tiles = n_slots // MOE_TILE
    rank, cnt = _route_rank(route)
    counts = cnt[0, :N_EXPERTS].astype(jnp.int32)
    padded = (counts + MOE_TILE - 1) // MOE_TILE * MOE_TILE
    ends = jnp.cumsum(padded)
    starts = ends - padded
    ids = route[:, 0:2].astype(jnp.int32)
    slot = jnp.take(starts, ids) + rank[:, 0:2].astype(jnp.int32)
    slots = slot.reshape(T // MOE_TM, MOE_TM, 2).transpose(0, 2, 1).reshape(-1)
    tile_start = jnp.arange(n_tiles, dtype=jnp.int32) * MOE_TILE
    tile_expert = jnp.minimum(jnp.sum(tile_start[:, None] >= ends[None, :], axis=1), N_EXPERTS - 1).astype(jnp.int32)
    n_used = (ends[-1:] // MOE_TILE).astype(jnp.int32)
    xs = _dispatch(ends.astype(jnp.int32), slots, hn, n_slots)
    ys = _expert_ffn(tile_expert, n_used, xs, wg, wu, wd)
    return _combine(slots, ys, route, res, norm_g, final_norm)


def _final_norm_kernel(x_ref, g_ref, o_ref):
    o_ref[...] = _rms(x_ref[...], D_MODEL) * g_ref[...]


def _final_norm(x2d, g):
    T = x2d.shape[0]
    tm = 1024
    return pl.pallas_call(
        _final_norm_kernel,
        grid=(T // tm,),
        in_specs=[pl.BlockSpec((tm, D_MODEL), lambda i: (i, 0)), pl.BlockSpec(g.shape, lambda i: (0, 0))],
        out_specs=pl.BlockSpec((tm, D_MODEL), lambda i: (i, 0)),
        out_shape=jax.ShapeDtypeStruct((T, D_MODEL), F32),
        compiler_params=_cparams(("parallel",)),
        name="final_norm",
    )(x2d, g)


def _rot_cols(w, half):
    return jnp.concatenate([-w[:, half:2 * half], w[:, 0:half]], axis=1)


def _aligned_w_in(w):
    z = lambda n: jnp.zeros((w.shape[0], n), F32)
    return jnp.concatenate([w[:, 0:O_NSA], z(P_NSA - O_NSA),
                            w[:, O_NSA:O_LRU], z(P_LRU - P_NSA - (O_LRU - O_NSA)),
                            w[:, O_LRU:], z(W_RAW - P_LRU - (w.shape[1] - O_LRU))], axis=1).astype(BF16)


def _rope_tables(positions):
    pos = positions.astype(F32)[:, None]
    S = positions.shape[0]

    def cs(rot_dim):
        inv = ROPE_THETA ** (-jnp.arange(0, rot_dim, 2, dtype=F32) / rot_dim)
        ang = pos * inv[None, :]
        return jnp.cos(ang), jnp.sin(ang)

    cm, sm = cs(MLA_ROPE)
    cn, sn = cs(NSA_ROT)
    one = lambda n: jnp.ones((S, n), F32)
    zero = lambda n: jnp.zeros((S, n), F32)
    cat = lambda parts: jnp.concatenate(parts, axis=1)
    cos_m = cat([one(64), cm, cm, one(32)])
    sin_m = cat([zero(64), sm, sm, zero(32)])
    kpe = [cos_m, cat([zero(80), sm, zero(32)]), cat([zero(64), -sm, zero(48)])]
    rest = NSA_HD - NSA_ROT
    c64, p64, m64 = cat([cn, cn, one(rest)]), cat([zero(8), sn, zero(rest)]), cat([-sn, zero(8), zero(rest)])
    q = [cat([c64, c64]), cat([p64, p64]), cat([m64, m64])]
    k = [cat([c64, one(64)]), cat([p64, zero(64)]), cat([m64, zero(64)])]
    return cat(kpe + q + k), cos_m, sin_m


def _mla_weights(w_uq, w_ukv, q_norm, kv_norm):
    hw = MLA_NOPE + MLA_ROPE
    pad_r = lambda m: jnp.concatenate([m, jnp.zeros((256 - MLA_Q_RANK, m.shape[1]), F32)], axis=0)
    z = lambda n: jnp.zeros((MLA_Q_RANK, n), F32)
    zk = lambda n: jnp.zeros((MLA_KV_RANK, n), F32)
    w1, w2, wk, wv = [], [], [], []
    for h in range(MLA_HEADS):
        qh = w_uq[:, h * hw:(h + 1) * hw]
        w1 += [qh, z(128 - hw)]
        w2 += [z(MLA_NOPE), _rot_cols(qh[:, MLA_NOPE:], MLA_ROPE // 2), z(128 - hw)]
        kvh = w_ukv[:, h * (MLA_NOPE + MLA_V):(h + 1) * (MLA_NOPE + MLA_V)]
        wk += [kvh[:, :MLA_NOPE], zk(128 - MLA_NOPE)]
        wv += [kvh[:, MLA_NOPE:], zk(128 - MLA_V)]
    qg = jnp.concatenate([q_norm, jnp.zeros((256 - MLA_Q_RANK,), F32)])[None, :]
    return (qg, pad_r(jnp.concatenate(w1, axis=1)).astype(BF16), pad_r(jnp.concatenate(w2, axis=1)).astype(BF16),
            kv_norm[None, :], jnp.concatenate(wk, axis=1).astype(BF16), jnp.concatenate(wv, axis=1).astype(BF16))


def _compress_weights(pe, w_cmp):
    half = CMP_LEN // 2
    wk = w_cmp[0].reshape(CMP_LEN, NSA_HD, NSA_HD)
    wv = w_cmp[1].reshape(CMP_LEN, NSA_HD, NSA_HD)
    z = jnp.zeros((half, NSA_HD, NSA_HD), F32)

    def interleave(ks, vs):
        top = jnp.concatenate([ks, z], axis=-1)
        bot = jnp.concatenate([z, vs], axis=-1)
        return jnp.concatenate([top, bot], axis=1).reshape(half * 2 * NSA_HD, 2 * NSA_HD)

    pad8 = lambda p: jnp.concatenate([p.reshape(1, -1), jnp.zeros((7, CMP_LEN * NSA_HD), F32)], axis=0)
    return (interleave(wk[:half], wv[:half]), interleave(wk[half:], wv[half:]),
            pad8(pe[0]), pad8(pe[1]), w_cmp[0], w_cmp[1])


def _overlap_matrix(S):
    nc_pad = S // CMP_STRIDE
    n = np.arange(nc_pad)[:, None]
    j = np.arange(SEL_LANES)[None, :]
    ov = ((n * CMP_STRIDE <= j * SEL_LEN + SEL_LEN - 1) & (n * CMP_STRIDE + CMP_LEN - 1 >= j * SEL_LEN)
          & (n < nc_pad - 1) & (j < S // SEL_LEN))
    return jnp.asarray(ov.astype(np.float32)).astype(BF16)


def _gate_expand():
    gw = NSA_HEADS * NSA_HD
    ex = np.zeros((128, 3 * gw), np.float32)
    for br in range(3):
        for h in range(NSA_HEADS):
            ex[h * 3 + br, br * gw + h * NSA_HD:br * gw + (h + 1) * NSA_HD] = 1.0
    return jnp.asarray(ex).astype(BF16)


def _lru_gate_weights(w_gate, b_gate):
    wg = jnp.zeros((LRU_W, 2 * LRU_W), F32)
    bw = LRU_W // LRU_BLOCKS
    for g in range(2):
        for n in range(LRU_BLOCKS):
            wg = wg.at[n * bw:(n + 1) * bw, g * LRU_W + n * bw:g * LRU_W + (n + 1) * bw].set(w_gate[g, n])
    return wg.astype(BF16), b_gate.reshape(1, 2 * LRU_W)


def _pad_lanes(v, n=128):
    return jnp.concatenate([v, jnp.zeros((n - v.shape[0],), F32)])[None, :]


def kernel(x, positions, norm_mix, w_in, mla_q_norm, mla_w_uq, mla_kv_norm, mla_w_ukv, nsa_cmp_pe, nsa_w_cmp,
           lru_conv_w, lru_conv_b, lru_w_gate, lru_b_gate, lru_lambda, ssd_conv_w, ssd_conv_b, ssd_dt_bias,
           ssd_a_log, ssd_d, group_norm, w_out, norm_ffn, ffn_w_gate, ffn_w_up, ffn_w_down, moe_router,
           moe_w_gate, moe_w_up, moe_w_down, norm_final):
    B, S, D = x.shape
    T = B * S
    depth = w_in.shape[0]
    assert S // SEL_LEN <= SEL_LANES and S % 512 == 0
    rope_tab, cos_m, sin_m = _rope_tables(positions)
    ov = _overlap_matrix(S)
    ex = _gate_expand()
    h_res = x.reshape(T, D)
    for l in range(depth):
        w_raw = _aligned_w_in(w_in[l])
        ua, qn, kvc, ksel, kwin, vsel, vwin, gate, uc, ud = _inproj(
            h_res, norm_mix[l][None, :], w_raw, rope_tab, S)

        q_m, k_m, v_m = _mla_prep(ua, *_mla_weights(mla_w_uq[l], mla_w_ukv[l], mla_q_norm[l], mla_kv_norm[l]),
                                  cos_m, sin_m, B, S)
        y_a = _flash(q_m, k_m, v_m, window=None).reshape(T, GROUP_W)

        kvcmp = _compress(kvc.reshape(B, S // CMP_STRIDE, CMP_STRIDE * 128),
                          *_compress_weights(nsa_cmp_pe[l], nsa_w_cmp[l]))
        o_c, q_aug = _cmp_select(qn, kvcmp, ov, B, S)
        o_s = _flash(q_aug, ksel.reshape(B, 1, S, 256), vsel.reshape(B, 1, S, 128), window=None).reshape(T, GROUP_W)
        o_w = _flash(q_aug, kwin.reshape(B, 1, S, 256), vwin.reshape(B, 1, S, 128),
                     window=WINDOW).reshape(T, GROUP_W)

        wg_l, bg_l = _lru_gate_weights(lru_w_gate[l], lru_b_gate[l])
        y_c = _lru(uc, lru_conv_w[l], lru_conv_b[l][None, :], wg_l, bg_l,
                   jax.nn.softplus(-lru_lambda[l])[None, :], B, S)

        y_d = _ssd(ud, ssd_conv_w[l], ssd_conv_b[l][None, :], _pad_lanes(ssd_dt_bias[l]),
                   _pad_lanes(-jnp.exp(ssd_a_log[l])), jnp.repeat(ssd_d[l], SSD_HD)[None, :],
                   group_norm[l, 3][None, :], B, S)

        moe_layer = l % 2 == 1
        rw = None
        if moe_layer:
            rw = jnp.concatenate([moe_router[l // 2], jnp.zeros((D, 128 - N_EXPERTS), F32)], axis=1)
            rw = jnp.stack(_split2(rw))
        outs = _outproj(y_a, o_c, o_s, o_w, gate, y_c, y_d, h_res, group_norm[l], ex, w_out[l].astype(BF16),
                        norm_ffn[l][None, :], rw)
        if moe_layer:
            h_res, hn, rg = outs
            normed = l == depth - 1
            h_res = _moe(hn, h_res, rg, moe_w_gate[l // 2], moe_w_up[l // 2], moe_w_down[l // 2],
                         norm_final[None, :], normed)
        else:
            normed = False
            h_res, hn = outs
            h_res = _ffn(hn, h_res, ffn_w_gate[l // 2].astype(BF16), ffn_w_up[l // 2].astype(BF16),
                         ffn_w_down[l // 2].astype(BF16))
    if not normed:
        h_res = _final_norm(h_res, norm_final[None, :])
    return h_res.reshape(B, S, D).astype(x.dtype)
```

```python
import functools

import numpy as np
import jax
import jax.numpy as jnp
from jax import lax
from jax.experimental import pallas as pl
from jax.experimental.pallas import tpu as pltpu

F32 = jnp.float32
BF16 = jnp.bfloat16

D_MODEL = 1024
GROUP_W = 256
ROPE_THETA = 500000.0
NORM_EPS = 1e-6
NEG = -1e30
FORCE = 1e4
MLA_HEADS, MLA_NOPE, MLA_ROPE, MLA_V = 4, 64, 32, 64
MLA_Q_RANK, MLA_KV_RANK = 192, 128
NSA_HEADS, NSA_HD, NSA_ROT = 4, 64, 16
CMP_STRIDE, CMP_LEN, SEL_LEN, SEL_TOPN, WINDOW = 16, 32, 64, 16, 512
LRU_W, LRU_BLOCKS, LRU_C, CONV_W = 256, 4, 8.0, 4
SSD_HEADS, SSD_HD, SSD_GROUPS, SSD_STATE, SSD_CHUNK = 4, 64, 2, 128, 128
SSD_DI = 256
SSD_XBC = SSD_DI + 2 * SSD_GROUPS * SSD_STATE
D_FF = 3584
N_EXPERTS = 8

O_NSA = MLA_Q_RANK + MLA_KV_RANK + MLA_ROPE
O_LRU = O_NSA + NSA_HEADS * NSA_HD + 6 * NSA_HD + 3 * NSA_HEADS
O_SSD = O_LRU + 2 * LRU_W
SSD_IN_W = 1152
P_NSA, P_LRU, W_RAW = 384, 1152, 2816
LOG2E = 1.4426950408889634
SEL_LANES = 128

VMEM_LIMIT = 56 * 1024 * 1024


def _cparams(sem):
    return pltpu.CompilerParams(dimension_semantics=sem, vmem_limit_bytes=VMEM_LIMIT)


def _dot(a, b):
    return jnp.dot(a.astype(BF16), b.astype(BF16), preferred_element_type=F32)


def _dot_nt(a, b):
    return lax.dot_general(a.astype(BF16), b.astype(BF16), (((1,), (1,)), ((), ())),
                           preferred_element_type=F32)


def _dot_f32(a, b):
    return jnp.dot(a, b, precision=lax.Precision.HIGHEST, preferred_element_type=F32)


def _dot_f32_nt(a, b):
    return lax.dot_general(a, b, (((1,), (1,)), ((), ())), precision=lax.Precision.HIGHEST,
                           preferred_element_type=F32)


def _split2(x):
    hi = x.astype(BF16)
    return hi, (x - hi.astype(F32)).astype(BF16)


def _sigmoid(x):
    return 1.0 / (1.0 + jnp.exp(-x))


def _softplus(x):
    return jnp.maximum(x, 0.0) + jnp.log(1.0 + jnp.exp(-jnp.abs(x)))


def _rms(x, width):
    return x * lax.rsqrt(jnp.sum(x * x, axis=-1, keepdims=True) * (1.0 / width) + NORM_EPS)


def _rope_tile(x, tab_ref, slot, half):
    t0 = 3 * 128 * slot
    return (x * tab_ref[:, t0:t0 + 128] + pltpu.roll(x, half, axis=1) * tab_ref[:, t0 + 128:t0 + 256]
            + pltpu.roll(x, 128 - half, axis=1) * tab_ref[:, t0 + 256:t0 + 384])


def _inproj_kernel(x_ref, g_ref, w_ref, tab_ref,
                   ua_ref, qn_ref, kvc_ref, ksel_ref, kwin_ref, vsel_ref, vwin_ref, gate_ref, uc_ref, ud_ref,
                   *, tm, seq_blocks):
    x = x_ref[...]
    h = _rms(x, D_MODEL) * g_ref[...]
    y = _dot(h, w_ref[...])
    lane = lax.broadcasted_iota(jnp.int32, (tm, 128), 1)
    lo64 = lane < 64
    ua_ref[:, 0:128] = y[:, 0:128]
    ua_ref[:, 128:256] = jnp.where(lo64, y[:, 128:256], 0.0)
    ua_ref[:, 256:384] = y[:, MLA_Q_RANK:MLA_Q_RANK + MLA_KV_RANK]
    kpe = jnp.where((lane >= 64) & (lane < 64 + MLA_ROPE), y[:, 256:384], 0.0)
    ua_ref[:, 384:512] = _rope_tile(kpe, tab_ref, 0, MLA_ROPE // 2)
    yn = y[:, P_NSA:P_NSA + 768]
    qn_ref[:, 0:128] = _rope_tile(yn[:, 0:128], tab_ref, 1, NSA_ROT // 2)
    qn_ref[:, 128:256] = _rope_tile(yn[:, 128:256], tab_ref, 1, NSA_ROT // 2)
    kvc_ref[...] = _rope_tile(yn[:, 256:384], tab_ref, 2, NSA_ROT // 2)
    s0 = (pl.program_id(0) % seq_blocks) * tm
    pos = s0 + lax.broadcasted_iota(jnp.int32, (tm, SEL_LANES), 0)
    onehot = jnp.where(jnp.right_shift(pos, 6) == lane, 1.0, 0.0)
    for k_ref, v_ref, c0, extra in ((ksel_ref, vsel_ref, 384, onehot), (kwin_ref, vwin_ref, 512, None)):
        kv = yn[:, c0:c0 + 128]
        k_ref[:, 0:128] = jnp.where(lo64, _rope_tile(kv, tab_ref, 2, NSA_ROT // 2), 0.0).astype(BF16)
        k_ref[:, 128:256] = (jnp.zeros((tm, 128), F32) if extra is None else extra).astype(BF16)
        v_ref[...] = jnp.where(lo64, pltpu.roll(kv, 64, axis=1), 1.0).astype(BF16)
    gate_ref[...] = jnp.where(lane < 3 * NSA_HEADS, yn[:, 640:768], 0.0)
    yl = y[:, P_LRU:P_LRU + 512 + SSD_IN_W]
    uc_ref[...] = yl[:, 0:512]
    ud_ref[:, 0:SSD_IN_W - 128] = yl[:, 512:512 + SSD_IN_W - 128]
    ud_ref[:, SSD_IN_W - 128:SSD_IN_W] = jnp.where(lane < SSD_HEADS, yl[:, 512 + SSD_IN_W - 128:512 + SSD_IN_W], 0.0)


def _inproj(x2d, g, w_raw, tab, seq):
    T = x2d.shape[0]
    tm = 256
    seq_blocks = seq // tm
    row = lambda w: pl.BlockSpec((tm, w), lambda i: (i, 0))
    full = lambda a: pl.BlockSpec(a.shape, lambda i: (0,) * a.ndim)
    outs = [(512, F32), (256, F32), (128, F32), (256, BF16), (256, BF16), (128, BF16), (128, BF16), (128, F32),
            (512, F32), (SSD_IN_W, F32)]
    return pl.pallas_call(
        functools.partial(_inproj_kernel, tm=tm, seq_blocks=seq_blocks),
        grid=(T // tm,),
        in_specs=[row(D_MODEL), full(g), full(w_raw), pl.BlockSpec((tm, tab.shape[1]), lambda i: (i % seq_blocks, 0))],
        out_specs=[row(w) for w, _ in outs],
        out_shape=[jax.ShapeDtypeStruct((T, w), dt) for w, dt in outs],
        compiler_params=_cparams(("parallel",)),
        name="inproj",
    )(x2d, g, w_raw, tab)


def _mla_prep_kernel(ua_ref, qg_ref, w1_ref, w2_ref, kvg_ref, wk_ref, wv_ref, cos_ref, sin_ref,
                     q_ref, k_ref, v_ref):
    ua = ua_ref[...]
    cq = _rms(ua[:, 0:256], MLA_Q_RANK) * qg_ref[...]
    y1 = _dot(cq, w1_ref[...])
    y2 = _dot(cq, w2_ref[...])
    ckv = _rms(ua[:, 256:384], MLA_KV_RANK) * kvg_ref[...]
    kn = _dot(ckv, wk_ref[...])
    vv = _dot(ckv, wv_ref[...])
    kpe = ua[:, 384:512]
    cos = cos_ref[...]
    sin = sin_ref[...]
    scale = LOG2E * (MLA_NOPE + MLA_ROPE) ** -0.5
    ones_hi = jnp.where(lax.broadcasted_iota(jnp.int32, cos.shape, 1) >= 64, 1.0, 0.0)
    for h in range(MLA_HEADS):
        sl = slice(h * 128, (h + 1) * 128)
        q_ref[h] = ((y1[:, sl] * cos + y2[:, sl] * sin) * scale).astype(BF16)
        k_ref[h] = (kn[:, sl] + kpe).astype(BF16)
        v_ref[h] = (vv[:, sl] + ones_hi).astype(BF16)


def _mla_prep(ua, qg, w1, w2, kvg, wk, wv, cos_m, sin_m, B, S):
    tm = 512
    nb = S // tm
    full = lambda a: pl.BlockSpec(a.shape, lambda b, i: (0,) * a.ndim)
    tab = pl.BlockSpec((tm, 128), lambda b, i: (i, 0))
    hd = lambda w: pl.BlockSpec((None, MLA_HEADS, tm, w), lambda b, i: (b, 0, i, 0))
    return pl.pallas_call(
        _mla_prep_kernel,
        grid=(B, nb),
        in_specs=[pl.BlockSpec((tm, 512), lambda b, i: (b * nb + i, 0)),
                  full(qg), full(w1), full(w2), full(kvg), full(wk), full(wv), tab, tab],
        out_specs=[hd(128), hd(128), hd(128)],
        out_shape=[jax.ShapeDtypeStruct((B, MLA_HEADS, S, 128), BF16)] * 3,
        compiler_params=_cparams(("parallel", "parallel")),
        name="mla_prep",
    )(ua, qg, w1, w2, kvg, wk, wv, cos_m, sin_m)


def _flash_kernel(qi_ref, ki_ref, flag_ref, q_ref, k_ref, v_ref, o_ref, m_ref, acc_ref,
                  *, heads, kv_heads, tq, tk, window):
    p_idx = pl.program_id(1)
    qi = qi_ref[p_idx]
    ki = ki_ref[p_idx]
    flags = flag_ref[p_idx]

    @pl.when(jnp.bitwise_and(flags, 1) != 0)
    def _init():
        m_ref[...] = jnp.full(m_ref.shape, NEG, F32)
        acc_ref[...] = jnp.zeros(acc_ref.shape, F32)

    def step(masked):
        if masked:
            qpos = qi * tq + lax.broadcasted_iota(jnp.int32, (tq, tk), 0)
            kpos = ki * tk + lax.broadcasted_iota(jnp.int32, (tq, tk), 1)
            mask = kpos <= qpos
            if window is not None:
                mask = mask & (kpos > qpos - window)
        nt = (((1,), (1,)), ((), ()))
        shared = kv_heads == 1
        if shared:
            dk = q_ref.shape[-1]
            s_all = lax.dot_general(q_ref[...].reshape(heads * tq, dk), k_ref[0], nt, preferred_element_type=F32)
        ps, alphas = [], []
        for h in range(heads):
            if shared:
                s = s_all[h * tq:(h + 1) * tq]
            else:
                s = lax.dot_general(q_ref[h], k_ref[h], nt, preferred_element_type=F32)
            if masked:
                s = jnp.where(mask, s, NEG)
            m_old = m_ref[h]
            m_new = jnp.maximum(m_old, jnp.max(s, axis=-1, keepdims=True))
            alpha = jnp.exp2(m_old - m_new)
            p = jnp.concatenate([jnp.exp2(s[:, c * 128:(c + 1) * 128] - m_new) for c in range(tk // 128)],
                                axis=1).astype(BF16)
            m_ref[h] = m_new
            if shared:
                ps.append(p)
                alphas.append(alpha)
            else:
                acc_ref[h] = alpha * acc_ref[h] + jnp.dot(p, v_ref[h], preferred_element_type=F32)
        if shared:
            pv = jnp.dot(jnp.concatenate(ps, axis=0), v_ref[0], preferred_element_type=F32)
            for h in range(heads):
                acc_ref[h] = alphas[h] * acc_ref[h] + pv[h * tq:(h + 1) * tq]

    @pl.when(jnp.bitwise_and(flags, 4) != 0)
    def _masked():
        step(True)

    @pl.when(jnp.bitwise_and(flags, 4) == 0)
    def _plain():
        step(False)

    @pl.when(jnp.bitwise_and(flags, 2) != 0)
    def _fin():
        for h in range(heads):
            acc = acc_ref[h]
            o_ref[:, h * 64:(h + 1) * 64] = acc[:, 0:64] / acc[:, 64:128]


def _pair_tables(S, tq, tk, window):
    qi, ki, flags = [], [], []
    for i in range(S // tq):
        q_lo, q_hi = i * tq, i * tq + tq - 1
        lo = 0 if window is None else max(0, q_lo - window + 1) // tk
        hi = q_hi // tk
        for j in range(lo, hi + 1):
            k_lo, k_hi = j * tk, j * tk + tk - 1
            masked = k_hi > q_lo or (window is not None and k_lo <= q_hi - window)
            qi.append(i)
            ki.append(j)
            flags.append((1 if j == lo else 0) | (2 if j == hi else 0) | (4 if masked else 0))
    mk = lambda a: jnp.asarray(np.asarray(a, np.int32))
    return mk(qi), mk(ki), mk(flags)


def _flash(q, k, v, *, window, tq, tk=512):
    B, H, S, dk = q.shape
    Hk = k.shape[1]
    tq, tk = min(tq, S), min(tk, S)
    tabs = _pair_tables(S, tq, tk, window)
    npairs = int(tabs[0].shape[0])
    grid_spec = pltpu.PrefetchScalarGridSpec(
        num_scalar_prefetch=3,
        grid=(B, npairs),
        in_specs=[pl.BlockSpec((None, H, tq, dk), lambda b, p, qi, ki, fl: (b, 0, qi[p], 0)),
                  pl.BlockSpec((None, Hk, tk, dk), lambda b, p, qi, ki, fl: (b, 0, ki[p], 0)),
                  pl.BlockSpec((None, Hk, tk, 128), lambda b, p, qi, ki, fl: (b, 0, ki[p], 0))],
        out_specs=pl.BlockSpec((None, tq, H * 64), lambda b, p, qi, ki, fl: (b, qi[p], 0)),
        scratch_shapes=[pltpu.VMEM((H, tq, 128), F32), pltpu.VMEM((H, tq, 128), F32)],
    )
    return pl.pallas_call(
        functools.partial(_flash_kernel, heads=H, kv_heads=Hk, tq=tq, tk=tk, window=window),
        grid_spec=grid_spec,
        out_shape=jax.ShapeDtypeStruct((B, S, H * 64), F32),
        compiler_params=_cparams(("parallel", "arbitrary")),
        name="flash_attn",
    )(*tabs, q, k, v)


def _compress_kernel(x_ref, wa_ref, wb_ref, pek_ref, pev_ref, wkf_ref, wvf_ref, o_ref, *, nc_pad):
    x = x_ref[...]
    a = _dot_f32(x, wa_ref[...])
    b = _dot_f32(x, wb_ref[...])
    ck = _dot_f32(pek_ref[...], wkf_ref[...])[0:1]
    cv = _dot_f32(pev_ref[...], wvf_ref[...])[0:1]
    const = jnp.concatenate([ck, cv], axis=-1)
    b_next = pltpu.roll(b, nc_pad - 1, axis=0)
    row = lax.broadcasted_iota(jnp.int32, (nc_pad, 128), 0)
    o_ref[...] = jnp.where(row < nc_pad - 1, a + b_next + const, 0.0)


def _compress(x, wa, wb, pek, pev, wkf, wvf):
    B, nc_pad, w = x.shape
    full = lambda a: pl.BlockSpec(a.shape, lambda b: (0,) * a.ndim)
    return pl.pallas_call(
        functools.partial(_compress_kernel, nc_pad=nc_pad),
        grid=(B,),
        in_specs=[pl.BlockSpec((None, nc_pad, w), lambda b: (b, 0, 0)),
                  full(wa), full(wb), full(pek), full(pev), full(wkf), full(wvf)],
        out_specs=pl.BlockSpec((None, nc_pad, 128), lambda b: (b, 0, 0)),
        out_shape=jax.ShapeDtypeStruct((B, nc_pad, 128), F32),
        compiler_params=_cparams(("parallel",)),
        name="nsa_compress",
    )(x, wa, wb, pek, pev, wkf, wvf)


def _cmp_select_kernel(q_ref, kvc_ref, ov_ref, oc_ref, qaug_ref, *, tq, nc_pad, n_top):
    i = pl.program_id(1)
    q = q_ref[...]
    kc = kvc_ref[:, 0:64]
    vc = kvc_ref[:, 64:128]
    scale = NSA_HD ** -0.5
    qpos = i * tq + lax.broadcasted_iota(jnp.int32, (tq, 1), 0)
    n_idx = lax.broadcasted_iota(jnp.int32, (1, nc_pad), 1)
    m_c = (n_idx * CMP_STRIDE + (CMP_LEN - 1) <= qpos) & (n_idx < nc_pad - 1)
    kc_hi, kc_lo = _split2(kc)
    kc3 = jnp.concatenate([kc_hi, kc_lo, kc_hi], axis=1)
    psum = jnp.zeros((tq, nc_pad), F32)
    for h in range(NSA_HEADS):
        q_hi, q_lo = _split2(q[:, h * NSA_HD:(h + 1) * NSA_HD])
        q3 = jnp.concatenate([q_hi, q_hi, q_lo], axis=1)
        s = lax.dot_general(q3, kc3, (((1,), (1,)), ((), ())), preferred_element_type=F32)
        s = jnp.where(m_c, s * scale, NEG)
        e = jnp.where(m_c, jnp.exp(s - jnp.max(s, axis=-1, keepdims=True)), 0.0)
        den = jnp.sum(e, axis=-1, keepdims=True)
        p = e / jnp.where(den > 0.0, den, 1.0)
        oc_ref[:, h * NSA_HD:(h + 1) * NSA_HD] = _dot(p, vc)
        psum = psum + p
    ov = ov_ref[...]
    p_hi = psum.astype(BF16)
    p_r = psum - p_hi.astype(F32)
    p_mid = p_r.astype(BF16)
    p_lo = (p_r - p_mid.astype(F32)).astype(BF16)
    imp = (jnp.dot(p_hi, ov, preferred_element_type=F32) + jnp.dot(p_mid, ov, preferred_element_type=F32)
           + jnp.dot(p_lo, ov, preferred_element_type=F32))
    cur = jnp.right_shift(qpos, 6)
    jj = lax.broadcasted_iota(jnp.int32, (1, SEL_LANES), 1)
    forced = (jj == 0) | (jj == cur) | (jj == cur - 1)
    bias = jnp.where(forced, 0.0, NEG)
    imp = jnp.where((jj <= cur) & jnp.logical_not(forced), imp, NEG)
    jf = jj.astype(F32)
    for _ in range(n_top - 3):
        mx = jnp.max(imp, axis=-1, keepdims=True)
        idx = jnp.min(jnp.where(imp == mx, jf, float(SEL_LANES)), axis=-1, keepdims=True)
        hit = jf == idx
        bias = jnp.where(hit & (mx > 0.5 * NEG), 0.0, bias)
        imp = jnp.where(hit, -3e38, imp)
    bias = bias.astype(BF16)
    qs = (q * (scale * LOG2E)).astype(BF16)
    for h in range(NSA_HEADS):
        qaug_ref[h, :, 0:64] = qs[:, h * NSA_HD:(h + 1) * NSA_HD]
        qaug_ref[h, :, 64:128] = jnp.zeros((tq, 64), BF16)
        qaug_ref[h, :, 128:256] = bias


def _cmp_select(qn, kvcmp, ov, B, S):
    tq = min(1024, S)
    nb = S // tq
    nc_pad = kvcmp.shape[1]
    n_top = min(SEL_TOPN, S // SEL_LEN)
    assert n_top >= 3
    return pl.pallas_call(
        functools.partial(_cmp_select_kernel, tq=tq, nc_pad=nc_pad, n_top=n_top),
        grid=(B, nb),
        in_specs=[pl.BlockSpec((tq, 256), lambda b, i: (b * nb + i, 0)),
                  pl.BlockSpec((None, nc_pad, 128), lambda b, i: (b, 0, 0)),
                  pl.BlockSpec(ov.shape, lambda b, i: (0, 0))],
        out_specs=[pl.BlockSpec((tq, 256), lambda b, i: (b * nb + i, 0)),
                   pl.BlockSpec((None, NSA_HEADS, tq, 256), lambda b, i: (b, 0, i, 0))],
        out_shape=[jax.ShapeDtypeStruct((B * S, 256), F32),
                   jax.ShapeDtypeStruct((B, NSA_HEADS, S, 256), BF16)],
        compiler_params=_cparams(("parallel", "parallel")),
        name="nsa_cmp_select",
    )(qn, kvcmp, ov)


def _shift_scan(a, b, t, width):
    row = lax.broadcasted_iota(jnp.int32, (t, width), 0)
    s = 1
    while s < t:
        keep = row >= s
        a_sh = jnp.where(keep, pltpu.roll(a, s, axis=0), 1.0)
        b_sh = jnp.where(keep, pltpu.roll(b, s, axis=0), 0.0)
        b = a * b_sh + b
        a = a * a_sh
        s *= 2
    return a, b


def _causal_conv(xpad_ref, x, cw_ref, cb_ref, t):
    xpad_ref[8:8 + t, :] = x
    y = cb_ref[...] + cw_ref[CONV_W - 1:CONV_W, :] * x
    for k in range(CONV_W - 1):
        off = 8 - (CONV_W - 1) + k
        y = y + cw_ref[k:k + 1, :] * xpad_ref[off:off + t, :]
    xpad_ref[0:8, :] = x[t - 8:t, :]
    return y


def _lru_kernel(u_ref, cw_ref, cb_ref, wg_ref, bg_ref, sp_ref, y_ref, xpad_ref, h_ref, *, t):
    @pl.when(pl.program_id(1) == 0)
    def _init():
        xpad_ref[0:8, :] = jnp.zeros((8, LRU_W), F32)
        h_ref[...] = jnp.zeros(h_ref.shape, F32)

    x = u_ref[:, 0:LRU_W]
    gbr = u_ref[:, LRU_W:2 * LRU_W]
    xb = _causal_conv(xpad_ref, x, cw_ref, cb_ref, t)
    g = _dot(xb, wg_ref[...]) + bg_ref[...]
    r = _sigmoid(g[:, 0:LRU_W])
    ig = _sigmoid(g[:, LRU_W:2 * LRU_W])
    log_a = -LRU_C * r * sp_ref[...]
    a = jnp.exp(log_a)
    one_m = -jnp.tanh(log_a) * (a * a + 1.0)
    b = jnp.sqrt(jnp.maximum(one_m, 0.0)) * (ig * xb)
    a_cum, h = _shift_scan(a, b, t, LRU_W)
    h = h + a_cum * h_ref[0:1, :]
    h_ref[0:1, :] = h[t - 1:t, :]
    gelu = 0.5 * gbr * (1.0 + jnp.tanh(0.7978845608028654 * (gbr + 0.044715 * gbr * gbr * gbr)))
    y_ref[...] = h * gelu


def _lru(uc, cw, cb, wg, bg, sp, B, S):
    t = 512
    nb = S // t
    full = lambda a: pl.BlockSpec(a.shape, lambda b, i: (0,) * a.ndim)
    return pl.pallas_call(
        functools.partial(_lru_kernel, t=t),
        grid=(B, nb),
        in_specs=[pl.BlockSpec((t, 2 * LRU_W), lambda b, i: (b * nb + i, 0)),
                  full(cw), full(cb), full(wg), full(bg), full(sp)],
        out_specs=pl.BlockSpec((t, LRU_W), lambda b, i: (b * nb + i, 0)),
        out_shape=jax.ShapeDtypeStruct((B * S, LRU_W), F32),
        scratch_shapes=[pltpu.VMEM((t + 8, LRU_W), F32), pltpu.VMEM((8, LRU_W), F32)],
        compiler_params=_cparams(("parallel", "arbitrary")),
        name="rglru",
    )(uc, cw, cb, wg, bg, sp)


def _ssd_kernel(u_ref, cw_ref, cb_ref, dtb_ref, a_ref, d_ref, ng_ref, y_ref, xpad_ref, st_ref, *, t):
    @pl.when(pl.program_id(1) == 0)
    def _init():
        xpad_ref[0:8, :] = jnp.zeros((8, SSD_XBC), F32)
        st_ref[...] = jnp.zeros(st_ref.shape, F32)

    L = SSD_CHUNK
    z = u_ref[:, 0:SSD_DI]
    conv = _causal_conv(xpad_ref, u_ref[:, SSD_DI:SSD_DI + SSD_XBC], cw_ref, cb_ref, t)
    xbc = conv * _sigmoid(conv)
    dt = _softplus(u_ref[:, SSD_DI + SSD_XBC:SSD_IN_W] + dtb_ref[...])
    a = dt * a_ref[...]
    row = lax.broadcasted_iota(jnp.int32, (t, 128), 0)
    rin = jnp.bitwise_and(row, L - 1)
    cs = a
    s = 1
    while s < L:
        cs = cs + jnp.where(rin >= s, pltpu.roll(cs, s, axis=0), 0.0)
        s *= 2
    tril = lax.broadcasted_iota(jnp.int32, (L, L), 0) >= lax.broadcasted_iota(jnp.int32, (L, L), 1)
    gn = SSD_GROUPS * SSD_STATE
    rep = SSD_HEADS // SSD_GROUPS
    for c in range(t // L):
        rs = slice(c * L, (c + 1) * L)
        cs_c = cs[rs]
        cs_t = cs_c.T
        cs_last = cs_c[L - 1:L, :]
        ys = []
        for g in range(SSD_GROUPS):
            bg = xbc[rs, SSD_DI + g * SSD_STATE:SSD_DI + (g + 1) * SSD_STATE]
            cg = xbc[rs, SSD_DI + gn + g * SSD_STATE:SSD_DI + gn + (g + 1) * SSD_STATE]
            gmat = _dot_nt(cg, bg)
            bg_t = bg.T
            for hh in range(rep):
                h = g * rep + hh
                cs_col = cs_c[:, h:h + 1]
                lm = jnp.exp(jnp.where(tril, cs_col - cs_t[h:h + 1, :], NEG))
                xh = xbc[rs, h * SSD_HD:(h + 1) * SSD_HD]
                xdt = xh * dt[rs, h:h + 1]
                st_old = st_ref[h]
                y_h = _dot(gmat * lm, xdt) + _dot(cg, st_old) * jnp.exp(cs_col)
                dec = jnp.exp(cs_last[:, h:h + 1] - cs_col)
                st_ref[h] = jnp.exp(cs_last[:, h:h + 1]) * st_old + _dot(bg_t, xdt * dec)
                ys.append(y_h)
        y = jnp.concatenate(ys, axis=-1) + xbc[rs, 0:SSD_DI] * d_ref[...]
        zc = z[rs]
        y = y * (zc * _sigmoid(zc))
        y_ref[rs, :] = _rms(y, SSD_DI) * ng_ref[...]


def _ssd(ud, cw, cb, dtb, a_neg, d_vec, ng, B, S):
    t = 512
    nb = S // t
    full = lambda a: pl.BlockSpec(a.shape, lambda b, i: (0,) * a.ndim)
    return pl.pallas_call(
        functools.partial(_ssd_kernel, t=t),
        grid=(B, nb),
        in_specs=[pl.BlockSpec((t, SSD_IN_W), lambda b, i: (b * nb + i, 0)),
                  full(cw), full(cb), full(dtb), full(a_neg), full(d_vec), full(ng)],
        out_specs=pl.BlockSpec((t, SSD_DI), lambda b, i: (b * nb + i, 0)),
        out_shape=jax.ShapeDtypeStruct((B * S, SSD_DI), F32),
        scratch_shapes=[pltpu.VMEM((t + 8, SSD_XBC), F32), pltpu.VMEM((SSD_HEADS, SSD_STATE, SSD_HD), F32)],
        compiler_params=_cparams(("parallel", "arbitrary")),
        name="ssd",
    )(ud, cw, cb, dtb, a_neg, d_vec, ng)


def _outproj_kernel(*refs, with_router):
    if with_router:
        (ya_ref, oc_ref, os_ref, ow_ref, gate_ref, yc_ref, yd_ref, res_ref, gn_ref, ex_ref, w_ref, nf_ref,
         rw_ref, hres_ref, hn_ref, rg_ref) = refs
    else:
        (ya_ref, oc_ref, os_ref, ow_ref, gate_ref, yc_ref, yd_ref, res_ref, gn_ref, ex_ref, w_ref, nf_ref,
         hres_ref, hn_ref) = refs
    sg_hi, sg_lo = _split2(_sigmoid(gate_ref[...]))
    ex = ex_ref[...]
    gx = jnp.dot(sg_hi, ex, preferred_element_type=F32) + jnp.dot(sg_lo, ex, preferred_element_type=F32)
    yb = gx[:, 0:256] * oc_ref[...] + gx[:, 256:512] * os_ref[...] + gx[:, 512:768] * ow_ref[...]
    y = jnp.concatenate([_rms(ya_ref[...], GROUP_W) * gn_ref[0:1, :],
                         _rms(yb, GROUP_W) * gn_ref[1:2, :],
                         _rms(yc_ref[...], GROUP_W) * gn_ref[2:3, :],
                         yd_ref[...]], axis=-1)
    hres = res_ref[...] + _dot(y, w_ref[...])
    hres_ref[...] = hres
    hn = _rms(hres, D_MODEL) * nf_ref[...]
    hn_ref[...] = hn.astype(hn_ref.dtype)
    if with_router:
        h_hi, h_lo = _split2(hn)
        logits = (jnp.dot(h_hi, rw_ref[0], preferred_element_type=F32)
                  + jnp.dot(h_hi, rw_ref[1], preferred_element_type=F32)
                  + jnp.dot(h_lo, rw_ref[0], preferred_element_type=F32))
        lane = lax.broadcasted_iota(jnp.int32, logits.shape, 1)
        lf = lane.astype(F32)
        logits = jnp.where(lane < N_EXPERTS, logits, NEG)
        m1 = jnp.max(logits, axis=-1, keepdims=True)
        i1 = jnp.min(jnp.where(logits == m1, lf, 128.0), axis=-1, keepdims=True)
        rest = jnp.where(lf == i1, NEG, logits)
        m2 = jnp.max(rest, axis=-1, keepdims=True)
        i2 = jnp.min(jnp.where(rest == m2, lf, 128.0), axis=-1, keepdims=True)
        e2 = jnp.exp(m2 - m1)
        den = 1.0 + e2
        rg_ref[...] = jnp.where(lane == 0, i1, jnp.where(lane == 1, i2, jnp.where(
            lane == 2, 1.0 / den, jnp.where(lane == 3, e2 / den, 0.0))))


def _outproj(ya, oc, osel, ow, gate, yc, yd, res, gn, ex, w, nf, rw):
    T = ya.shape[0]
    tm = 512
    with_router = rw is not None
    row = lambda wd: pl.BlockSpec((tm, wd), lambda i: (i, 0))
    full = lambda a: pl.BlockSpec(a.shape, lambda i: (0,) * a.ndim)
    ins = [ya, oc, osel, ow, gate, yc, yd, res, gn, ex, w, nf]
    in_specs = [row(256), row(256), row(256), row(256), row(128), row(256), row(256), row(D_MODEL),
                full(gn), full(ex), full(w), full(nf)]
    out_specs = [row(D_MODEL), row(D_MODEL)]
    out_shape = [jax.ShapeDtypeStruct((T, D_MODEL), F32),
                 jax.ShapeDtypeStruct((T, D_MODEL), F32 if with_router else BF16)]
    if with_router:
        ins.append(rw)
        in_specs.append(full(rw))
        out_specs.append(row(128))
        out_shape.append(jax.ShapeDtypeStruct((T, 128), F32))
    return pl.pallas_call(
        functools.partial(_outproj_kernel, with_router=with_router),
        grid=(T // tm,),
        in_specs=in_specs,
        out_specs=out_specs,
        out_shape=out_shape,
        compiler_params=_cparams(("parallel",)),
        name="outproj",
    )(*ins)


def _ffn_kernel(h_ref, res_ref, wg_ref, wu_ref, wd_ref, o_ref, acc_ref):
    j = pl.program_id(1)

    @pl.when(j == 0)
    def _init():
        acc_ref[...] = jnp.zeros(acc_ref.shape, F32)

    h = h_ref[...]
    g = jnp.dot(h, wg_ref[...], preferred_element_type=F32)
    u = jnp.dot(h, wu_ref[...], preferred_element_type=F32)
    acc_ref[...] += _dot(g * _sigmoid(g) * u, wd_ref[...])

    @pl.when(j == pl.num_programs(1) - 1)
    def _fin():
        o_ref[...] = res_ref[...] + acc_ref[...]


def _ffn(hn, res, wg, wu, wd):
    T = hn.shape[0]
    tm, tf = 1024, 512
    return pl.pallas_call(
        _ffn_kernel,
        grid=(T // tm, D_FF // tf),
        in_specs=[pl.BlockSpec((tm, D_MODEL), lambda i, j: (i, 0)),
                  pl.BlockSpec((tm, D_MODEL), lambda i, j: (i, 0)),
                  pl.BlockSpec((D_MODEL, tf), lambda i, j: (0, j)),
                  pl.BlockSpec((D_MODEL, tf), lambda i, j: (0, j)),
                  pl.BlockSpec((tf, D_MODEL), lambda i, j: (j, 0))],
        out_specs=pl.BlockSpec((tm, D_MODEL), lambda i, j: (i, 0)),
        out_shape=jax.ShapeDtypeStruct((T, D_MODEL), F32),
        scratch_shapes=[pltpu.VMEM((tm, D_MODEL), F32)],
        compiler_params=_cparams(("parallel", "arbitrary")),
        name="ffn_dense",
    )(hn, res, wg, wu, wd)


MOE_TILE = 1024
MOE_TM = 512


def _route_rank_kernel(route_ref, rank_ref, cnt_ref, carry_ref, *, tm):
    @pl.when(pl.program_id(0) == 0)
    def _init():
        carry_ref[...] = jnp.zeros(carry_ref.shape, F32)

    r = route_ref[...]
    i1 = r[:, 0:1]
    i2 = r[:, 1:2]
    lane = lax.broadcasted_iota(jnp.int32, (tm, 128), 1)
    lf = lane.astype(F32)
    oh = jnp.where((lf == i1) | (lf == i2), 1.0, 0.0)
    row = lax.broadcasted_iota(jnp.int32, (tm, 128), 0)
    cs = oh
    s = 1
    while s < tm:
        cs = cs + jnp.where(row >= s, pltpu.roll(cs, s, axis=0), 0.0)
        s *= 2
    excl = cs - oh + carry_ref[0:1, :]
    rank1 = jnp.sum(jnp.where(lf == i1, excl, 0.0), axis=-1, keepdims=True)
    rank2 = jnp.sum(jnp.where(lf == i2, excl, 0.0), axis=-1, keepdims=True)
    rank_ref[...] = jnp.where(lane == 0, rank1, jnp.where(lane == 1, rank2, 0.0))
    carry_ref[0:1, :] = carry_ref[0:1, :] + cs[tm - 1:tm, :]
    cnt_ref[...] = carry_ref[...]


def _route_rank(route):
    T = route.shape[0]
    tm = MOE_TM
    return pl.pallas_call(
        functools.partial(_route_rank_kernel, tm=tm),
        grid=(T // tm,),
        in_specs=[pl.BlockSpec((tm, 128), lambda i: (i, 0))],
        out_specs=[pl.BlockSpec((tm, 128), lambda i: (i, 0)), pl.BlockSpec((8, 128), lambda i: (0, 0))],
        out_shape=[jax.ShapeDtypeStruct((T, 128), F32), jax.ShapeDtypeStruct((8, 128), F32)],
        scratch_shapes=[pltpu.VMEM((8, 128), F32)],
        compiler_params=_cparams(("arbitrary",)),
        name="moe_rank",
    )(route)


def _row_copies(idx_ref, tm, make_copy):
    def body(r, carry):
        for k in range(2):
            make_copy(k, r, idx_ref[k * tm + r]).start(priority=k)
        return carry

    lax.fori_loop(0, tm, body, 0, unroll=8)


def _dispatch_kernel(ends_ref, slots_ref, h_ref, xs_ref, idx_ref, zero_ref, sem_idx, sem_row, *, tm):
    i = pl.program_id(0)

    @pl.when(i == 0)
    def _zero_tiles():
        zero_ref[...] = jnp.zeros(zero_ref.shape, F32)
        zr = zero_ref.shape[0]
        jobs = []
        for e in range(N_EXPERTS):
            start = ends_ref[e - 1] if e else 0
            jobs.append((ends_ref[e] > start, ends_ref[e] - MOE_TILE))
        for tile in range(N_EXPERTS):
            base = xs_ref.shape[0] - (tile + 1) * MOE_TILE
            jobs.append((base >= ends_ref[N_EXPERTS - 1], base))

        def tile_copies(base):
            rows = [base + q * zr for q in range(MOE_TILE // zr)]
            rows = [r if isinstance(r, int) else pl.multiple_of(r, zr) for r in rows]
            return [pltpu.make_async_copy(zero_ref, xs_ref.at[pl.ds(r, zr)], sem_idx) for r in rows]

        for cond, base in jobs:
            @pl.when(cond)
            def _start(base=base):
                for cp in tile_copies(base):
                    cp.start()
        for cond, base in jobs:
            @pl.when(cond)
            def _wait(base=base):
                for cp in tile_copies(base):
                    cp.wait()

    cp = pltpu.make_async_copy(slots_ref.at[pl.ds(i * 2 * tm, 2 * tm)], idx_ref, sem_idx)
    cp.start()
    cp.wait()
    _row_copies(idx_ref, tm, lambda k, r, s: pltpu.make_async_copy(
        h_ref.at[pl.ds(r, 1)], xs_ref.at[pl.ds(s, 1)], sem_row))
    for _ in range(2):
        pltpu.make_async_copy(h_ref, xs_ref.at[pl.ds(0, tm)], sem_row).wait()


def _dispatch(ends, slots, hn, n_slots):
    T = hn.shape[0]
    tm = MOE_TM
    grid_spec = pltpu.PrefetchScalarGridSpec(
        num_scalar_prefetch=1,
        grid=(T // tm,),
        in_specs=[pl.BlockSpec(memory_space=pl.ANY),
                  pl.BlockSpec((tm, D_MODEL), lambda i, ends: (i, 0))],
        out_specs=pl.BlockSpec(memory_space=pl.ANY),
        scratch_shapes=[pltpu.SMEM((2 * tm,), jnp.int32), pltpu.VMEM((256, D_MODEL), F32),
                        pltpu.SemaphoreType.DMA, pltpu.SemaphoreType.DMA],
    )
    return pl.pallas_call(
        functools.partial(_dispatch_kernel, tm=tm),
        grid_spec=grid_spec,
        out_shape=jax.ShapeDtypeStruct((n_slots, D_MODEL), F32),
        compiler_params=_cparams(("arbitrary",)),
        name="moe_dispatch",
    )(ends, slots, hn)


def _expert_ffn_kernel(te_ref, nu_ref, x_ref, wg_ref, wu_ref, wd_ref, o_ref, acc_ref):
    del te_ref
    i = pl.program_id(0)
    j = pl.program_id(1)
    last = pl.num_programs(1) - 1
    used = i < nu_ref[0]

    @pl.when(used & (j == 0))
    def _init():
        acc_ref[...] = jnp.zeros(acc_ref.shape, F32)

    @pl.when(used)
    def _step():
        x = x_ref[...]
        g = _dot(x, wg_ref[...])
        u = _dot(x, wu_ref[...])
        acc_ref[...] += _dot(g * _sigmoid(g) * u, wd_ref[...])

    @pl.when(used & (j == last))
    def _fin():
        o_ref[...] = acc_ref[...]

    @pl.when(jnp.logical_not(used) & (j == last))
    def _unused():
        o_ref[...] = jnp.zeros(o_ref.shape, F32)


def _expert_ffn(tile_expert, n_used, xs, wg, wu, wd):
    n_slots = xs.shape[0]
    tm, tf = MOE_TILE, 512
    jw = lambda i, j, nu: jnp.where(i < nu[0], j, 0)
    grid_spec = pltpu.PrefetchScalarGridSpec(
        num_scalar_prefetch=2,
        grid=(n_slots // tm, D_FF // tf),
        in_specs=[pl.BlockSpec((tm, D_MODEL), lambda i, j, te, nu: (jnp.minimum(i, nu[0] - 1), 0)),
                  pl.BlockSpec((None, D_MODEL, tf), lambda i, j, te, nu: (te[i], 0, jw(i, j, nu))),
                  pl.BlockSpec((None, D_MODEL, tf), lambda i, j, te, nu: (te[i], 0, jw(i, j, nu))),
                  pl.BlockSpec((None, tf, D_MODEL), lambda i, j, te, nu: (te[i], jw(i, j, nu), 0))],
        out_specs=pl.BlockSpec((tm, D_MODEL), lambda i, j, te, nu: (i, 0)),
        scratch_shapes=[pltpu.VMEM((tm, D_MODEL), F32)],
    )
    return pl.pallas_call(
        _expert_ffn_kernel,
        grid_spec=grid_spec,
        out_shape=jax.ShapeDtypeStruct((n_slots, D_MODEL), F32),
        compiler_params=_cparams(("parallel", "arbitrary")),
        name="moe_expert_ffn",
    )(tile_expert, n_used, xs, wg, wu, wd)


def _combine_kernel(slots_ref, ys_ref, route_ref, res_ref, ng_ref, o_ref, idx_ref, buf_ref, sem_idx, sem_row,
                    *, tm, final_norm):
    i = pl.program_id(0)
    cp = pltpu.make_async_copy(slots_ref.at[pl.ds(i * 2 * tm, 2 * tm)], idx_ref, sem_idx)
    cp.start()
    cp.wait()
    _row_copies(idx_ref, tm, lambda k, r, s: pltpu.make_async_copy(
        ys_ref.at[pl.ds(s, 1)], buf_ref.at[k, pl.ds(r, 1)], sem_row))
    for k in range(2):
        pltpu.make_async_copy(ys_ref.at[pl.ds(0, tm)], buf_ref.at[k], sem_row).wait()
    route = route_ref[...]
    out = res_ref[...] + route[:, 2:3] * buf_ref[0] + route[:, 3:4] * buf_ref[1]
    if final_norm:
        out = _rms(out, D_MODEL) * ng_ref[...]
    o_ref[...] = out


def _combine(slots, ys, route, res, norm_g, final_norm):
    T = res.shape[0]
    tm = MOE_TM
    return pl.pallas_call(
        functools.partial(_combine_kernel, tm=tm, final_norm=final_norm),
        grid=(T // tm,),
        in_specs=[pl.BlockSpec(memory_space=pl.ANY),
                  pl.BlockSpec(memory_space=pl.ANY),
                  pl.BlockSpec((tm, 128), lambda i: (i, 0)),
                  pl.BlockSpec((tm, D_MODEL), lambda i: (i, 0)),
                  pl.BlockSpec((1, D_MODEL), lambda i: (0, 0))],
        out_specs=pl.BlockSpec((tm, D_MODEL), lambda i: (i, 0)),
        out_shape=jax.ShapeDtypeStruct((T, D_MODEL), F32),
        scratch_shapes=[pltpu.SMEM((2 * tm,), jnp.int32), pltpu.VMEM((2, tm, D_MODEL), F32),
                        pltpu.SemaphoreType.DMA, pltpu.SemaphoreType.DMA],
        compiler_params=_cparams(("arbitrary",)),
        name="moe_combine",
    )(slots, ys, route, res, norm_g)


def _moe(hn, res, route, wg, wu, wd, norm_g, final_norm):
    T = hn.shape[0]
    n_slots = 2 * T + N_EXPERTS * MOE_TILE
    n_tiles = n_slots // MOE_TILE
    rank, cnt = _route_rank(route)
    counts = cnt[0, :N_EXPERTS].astype(jnp.int32)
    padded = (counts + MOE_TILE - 1) // MOE_TILE * MOE_TILE
    ends = jnp.cumsum(padded)
    starts = ends - padded
    ids = route[:, 0:2].astype(jnp.int32)
    slot = jnp.take(starts, ids) + rank[:, 0:2].astype(jnp.int32)
    slots = slot.reshape(T // MOE_TM, MOE_TM, 2).transpose(0, 2, 1).reshape(-1)
    tile_start = jnp.arange(n_tiles, dtype=jnp.int32) * MOE_TILE
    tile_expert = jnp.minimum(jnp.sum(tile_start[:, None] >= ends[None, :], axis=1), N_EXPERTS - 1).astype(jnp.int32)
    n_used = (ends[-1:] // MOE_TILE).astype(jnp.int32)
    xs = _dispatch(ends.astype(jnp.int32), slots, hn, n_slots)
    ys = _expert_ffn(tile_expert, n_used, xs, wg, wu, wd)
    return _combine(slots, ys, route, res, norm_g, final_norm)


def _final_norm_kernel(x_ref, g_ref, o_ref):
    o_ref[...] = _rms(x_ref[...], D_MODEL) * g_ref[...]


def _final_norm(x2d, g):
    T = x2d.shape[0]
    tm = 1024
    return pl.pallas_call(
        _final_norm_kernel,
        grid=(T // tm,),
        in_specs=[pl.BlockSpec((tm, D_MODEL), lambda i: (i, 0)), pl.BlockSpec(g.shape, lambda i: (0, 0))],
        out_specs=pl.BlockSpec((tm, D_MODEL), lambda i: (i, 0)),
        out_shape=jax.ShapeDtypeStruct((T, D_MODEL), F32),
        compiler_params=_cparams(("parallel",)),
        name="final_norm",
    )(x2d, g)


def _rot_cols(w, half):
    return jnp.concatenate([-w[:, half:2 * half], w[:, 0:half]], axis=1)


def _aligned_w_in(w):
    z = lambda n: jnp.zeros((w.shape[0], n), F32)
    return jnp.concatenate([w[:, 0:O_NSA], z(P_NSA - O_NSA),
                            w[:, O_NSA:O_LRU], z(P_LRU - P_NSA - (O_LRU - O_NSA)),
                            w[:, O_LRU:], z(W_RAW - P_LRU - (w.shape[1] - O_LRU))], axis=1).astype(BF16)


def _rope_tables(positions):
    pos = positions.astype(F32)[:, None]
    S = positions.shape[0]

    def cs(rot_dim):
        inv = ROPE_THETA ** (-jnp.arange(0, rot_dim, 2, dtype=F32) / rot_dim)
        ang = pos * inv[None, :]
        return jnp.cos(ang), jnp.sin(ang)

    cm, sm = cs(MLA_ROPE)
    cn, sn = cs(NSA_ROT)
    one = lambda n: jnp.ones((S, n), F32)
    zero = lambda n: jnp.zeros((S, n), F32)
    cat = lambda parts: jnp.concatenate(parts, axis=1)
    cos_m = cat([one(64), cm, cm, one(32)])
    sin_m = cat([zero(64), sm, sm, zero(32)])
    kpe = [cos_m, cat([zero(80), sm, zero(32)]), cat([zero(64), -sm, zero(48)])]
    rest = NSA_HD - NSA_ROT
    c64, p64, m64 = cat([cn, cn, one(rest)]), cat([zero(8), sn, zero(rest)]), cat([-sn, zero(8), zero(rest)])
    q = [cat([c64, c64]), cat([p64, p64]), cat([m64, m64])]
    k = [cat([c64, one(64)]), cat([p64, zero(64)]), cat([m64, zero(64)])]
    return cat(kpe + q + k), cos_m, sin_m


def _mla_weights(w_uq, w_ukv, q_norm, kv_norm):
    hw = MLA_NOPE + MLA_ROPE
    pad_r = lambda m: jnp.concatenate([m, jnp.zeros((256 - MLA_Q_RANK, m.shape[1]), F32)], axis=0)
    z = lambda n: jnp.zeros((MLA_Q_RANK, n), F32)
    zk = lambda n: jnp.zeros((MLA_KV_RANK, n), F32)
    w1, w2, wk, wv = [], [], [], []
    for h in range(MLA_HEADS):
        qh = w_uq[:, h * hw:(h + 1) * hw]
        w1 += [qh, z(128 - hw)]
        w2 += [z(MLA_NOPE), _rot_cols(qh[:, MLA_NOPE:], MLA_ROPE // 2), z(128 - hw)]
        kvh = w_ukv[:, h * (MLA_NOPE + MLA_V):(h + 1) * (MLA_NOPE + MLA_V)]
        wk += [kvh[:, :MLA_NOPE], zk(128 - MLA_NOPE)]
        wv += [kvh[:, MLA_NOPE:], zk(128 - MLA_V)]
    qg = jnp.concatenate([q_norm, jnp.zeros((256 - MLA_Q_RANK,), F32)])[None, :]
    return (qg, pad_r(jnp.concatenate(w1, axis=1)).astype(BF16), pad_r(jnp.concatenate(w2, axis=1)).astype(BF16),
            kv_norm[None, :], jnp.concatenate(wk, axis=1).astype(BF16), jnp.concatenate(wv, axis=1).astype(BF16))


def _compress_weights(pe, w_cmp):
    half = CMP_LEN // 2
    wk = w_cmp[0].reshape(CMP_LEN, NSA_HD, NSA_HD)
    wv = w_cmp[1].reshape(CMP_LEN, NSA_HD, NSA_HD)
    z = jnp.zeros((half, NSA_HD, NSA_HD), F32)

    def interleave(ks, vs):
        top = jnp.concatenate([ks, z], axis=-1)
        bot = jnp.concatenate([z, vs], axis=-1)
        return jnp.concatenate([top, bot], axis=1).reshape(half * 2 * NSA_HD, 2 * NSA_HD)

    pad8 = lambda p: jnp.concatenate([p.reshape(1, -1), jnp.zeros((7, CMP_LEN * NSA_HD), F32)], axis=0)
    return (interleave(wk[:half], wv[:half]), interleave(wk[half:], wv[half:]),
            pad8(pe[0]), pad8(pe[1]), w_cmp[0], w_cmp[1])


def _overlap_matrix(S):
    nc_pad = S // CMP_STRIDE
    n = np.arange(nc_pad)[:, None]
    j = np.arange(SEL_LANES)[None, :]
    ov = ((n * CMP_STRIDE <= j * SEL_LEN + SEL_LEN - 1) & (n * CMP_STRIDE + CMP_LEN - 1 >= j * SEL_LEN)
          & (n < nc_pad - 1) & (j < S // SEL_LEN))
    return jnp.asarray(ov.astype(np.float32)).astype(BF16)


def _gate_expand():
    gw = NSA_HEADS * NSA_HD
    ex = np.zeros((128, 3 * gw), np.float32)
    for br in range(3):
        for h in range(NSA_HEADS):
            ex[h * 3 + br, br * gw + h * NSA_HD:br * gw + (h + 1) * NSA_HD] = 1.0
    return jnp.asarray(ex).astype(BF16)


def _lru_gate_weights(w_gate, b_gate):
    wg = jnp.zeros((LRU_W, 2 * LRU_W), F32)
    bw = LRU_W // LRU_BLOCKS
    for g in range(2):
        for n in range(LRU_BLOCKS):
            wg = wg.at[n * bw:(n + 1) * bw, g * LRU_W + n * bw:g * LRU_W + (n + 1) * bw].set(w_gate[g, n])
    return wg.astype(BF16), b_gate.reshape(1, 2 * LRU_W)


def _pad_lanes(v, n=128):
    return jnp.concatenate([v, jnp.zeros((n - v.shape[0],), F32)])[None, :]


def kernel(x, positions, norm_mix, w_in, mla_q_norm, mla_w_uq, mla_kv_norm, mla_w_ukv, nsa_cmp_pe, nsa_w_cmp,
           lru_conv_w, lru_conv_b, lru_w_gate, lru_b_gate, lru_lambda, ssd_conv_w, ssd_conv_b, ssd_dt_bias,
           ssd_a_log, ssd_d, group_norm, w_out, norm_ffn, ffn_w_gate, ffn_w_up, ffn_w_down, moe_router,
           moe_w_gate, moe_w_up, moe_w_down, norm_final):
    B, S, D = x.shape
    T = B * S
    depth = w_in.shape[0]
    assert S // SEL_LEN <= SEL_LANES and S % 512 == 0
    rope_tab, cos_m, sin_m = _rope_tables(positions)
    ov = _overlap_matrix(S)
    ex = _gate_expand()
    h_res = x.reshape(T, D)
    for l in range(depth):
        w_raw = _aligned_w_in(w_in[l])
        ua, qn, kvc, ksel, kwin, vsel, vwin, gate, uc, ud = _inproj(
            h_res, norm_mix[l][None, :], w_raw, rope_tab, S)

        q_m, k_m, v_m = _mla_prep(ua, *_mla_weights(mla_w_uq[l], mla_w_ukv[l], mla_q_norm[l], mla_kv_norm[l]),
                                  cos_m, sin_m, B, S)
        y_a = _flash(q_m, k_m, v_m, window=None, tq=1024).reshape(T, GROUP_W)

        kvcmp = _compress(kvc.reshape(B, S // CMP_STRIDE, CMP_STRIDE * 128),
                          *_compress_weights(nsa_cmp_pe[l], nsa_w_cmp[l]))
        o_c, q_aug = _cmp_select(qn, kvcmp, ov, B, S)
        o_s = _flash(q_aug, ksel.reshape(B, 1, S, 256), vsel.reshape(B, 1, S, 128), window=None,
                     tq=512).reshape(T, GROUP_W)
        o_w = _flash(q_aug, kwin.reshape(B, 1, S, 256), vwin.reshape(B, 1, S, 128), window=WINDOW,
                     tq=512).reshape(T, GROUP_W)

        wg_l, bg_l = _lru_gate_weights(lru_w_gate[l], lru_b_gate[l])
        y_c = _lru(uc, lru_conv_w[l], lru_conv_b[l][None, :], wg_l, bg_l,
                   jax.nn.softplus(-lru_lambda[l])[None, :], B, S)

        y_d = _ssd(ud, ssd_conv_w[l], ssd_conv_b[l][None, :], _pad_lanes(ssd_dt_bias[l]),
                   _pad_lanes(-jnp.exp(ssd_a_log[l])), jnp.repeat(ssd_d[l], SSD_HD)[None, :],
                   group_norm[l, 3][None, :], B, S)

        moe_layer = l % 2 == 1
        rw = None
        if moe_layer:
            rw = jnp.concatenate([moe_router[l // 2], jnp.zeros((D, 128 - N_EXPERTS), F32)], axis=1)
            rw = jnp.stack(_split2(rw))
        outs = _outproj(y_a, o_c, o_s, o_w, gate, y_c, y_d, h_res, group_norm[l], ex, w_out[l].astype(BF16),
                        norm_ffn[l][None, :], rw)
        if moe_layer:
            h_res, hn, rg = outs
            normed = l == depth - 1
            h_res = _moe(hn, h_res, rg, moe_w_gate[l // 2], moe_w_up[l // 2], moe_w_down[l // 2],
                         norm_final[None, :], normed)
        else:
            normed = False
            h_res, hn = outs
            h_res = _ffn(hn, h_res, ffn_w_gate[l // 2].astype(BF16), ffn_w_up[l // 2].astype(BF16),
                         ffn_w_down[l // 2].astype(BF16))
    if not normed:
        h_res = _final_norm(h_res, norm_final[None, :])
    return h_res.reshape(B, S, D).astype(x.dtype)
```

```python
import functools

import numpy as np
import jax
import jax.numpy as jnp
from jax import lax
from jax.experimental import pallas as pl
from jax.experimental.pallas import tpu as pltpu

F32 = jnp.float32
BF16 = jnp.bfloat16

D_MODEL = 1024
GROUP_W = 256
ROPE_THETA = 500000.0
NORM_EPS = 1e-6
NEG = -1e30
FORCE = 1e4
MLA_HEADS, MLA_NOPE, MLA_ROPE, MLA_V = 4, 64, 32, 64
MLA_Q_RANK, MLA_KV_RANK = 192, 128
NSA_HEADS, NSA_HD, NSA_ROT = 4, 64, 16
CMP_STRIDE, CMP_LEN, SEL_LEN, SEL_TOPN, WINDOW = 16, 32, 64, 16, 512
LRU_W, LRU_BLOCKS, LRU_C, CONV_W = 256, 4, 8.0, 4
SSD_HEADS, SSD_HD, SSD_GROUPS, SSD_STATE, SSD_CHUNK = 4, 64, 2, 128, 128
SSD_DI = 256
SSD_XBC = SSD_DI + 2 * SSD_GROUPS * SSD_STATE
D_FF = 3584
N_EXPERTS = 8

O_NSA = MLA_Q_RANK + MLA_KV_RANK + MLA_ROPE
O_LRU = O_NSA + NSA_HEADS * NSA_HD + 6 * NSA_HD + 3 * NSA_HEADS
O_SSD = O_LRU + 2 * LRU_W
SSD_IN_W = 1152
P_NSA, P_LRU, W_RAW = 384, 1152, 2816
LOG2E = 1.4426950408889634
SEL_LANES = 128

VMEM_LIMIT = 56 * 1024 * 1024


def _cparams(sem):
    return pltpu.CompilerParams(dimension_semantics=sem, vmem_limit_bytes=VMEM_LIMIT)


def _dot(a, b):
    return jnp.dot(a.astype(BF16), b.astype(BF16), preferred_element_type=F32)


def _dot_nt(a, b):
    return lax.dot_general(a.astype(BF16), b.astype(BF16), (((1,), (1,)), ((), ())),
                           preferred_element_type=F32)


def _dot_f32(a, b):
    return jnp.dot(a, b, precision=lax.Precision.HIGHEST, preferred_element_type=F32)


def _dot_f32_nt(a, b):
    return lax.dot_general(a, b, (((1,), (1,)), ((), ())), precision=lax.Precision.HIGHEST,
                           preferred_element_type=F32)


def _split2(x):
    hi = x.astype(BF16)
    return hi, (x - hi.astype(F32)).astype(BF16)


def _sigmoid(x):
    return 1.0 / (1.0 + jnp.exp(-x))


def _softplus(x):
    return jnp.maximum(x, 0.0) + jnp.log(1.0 + jnp.exp(-jnp.abs(x)))


def _rms(x, width):
    return x * lax.rsqrt(jnp.sum(x * x, axis=-1, keepdims=True) * (1.0 / width) + NORM_EPS)


def _rope_tile(x, tab_ref, slot, half):
    t0 = 3 * 128 * slot
    return (x * tab_ref[:, t0:t0 + 128] + pltpu.roll(x, half, axis=1) * tab_ref[:, t0 + 128:t0 + 256]
            + pltpu.roll(x, 128 - half, axis=1) * tab_ref[:, t0 + 256:t0 + 384])


def _inproj_kernel(x_ref, g_ref, w_ref, tab_ref,
                   ua_ref, qn_ref, kvc_ref, ksel_ref, kwin_ref, vsel_ref, vwin_ref, gate_ref, uc_ref, ud_ref,
                   *, tm, seq_blocks):
    x = x_ref[...]
    h = _rms(x, D_MODEL) * g_ref[...]
    y = _dot(h, w_ref[...])
    lane = lax.broadcasted_iota(jnp.int32, (tm, 128), 1)
    lo64 = lane < 64
    ua_ref[:, 0:128] = y[:, 0:128]
    ua_ref[:, 128:256] = jnp.where(lo64, y[:, 128:256], 0.0)
    ua_ref[:, 256:384] = y[:, MLA_Q_RANK:MLA_Q_RANK + MLA_KV_RANK]
    kpe = jnp.where((lane >= 64) & (lane < 64 + MLA_ROPE), y[:, 256:384], 0.0)
    ua_ref[:, 384:512] = _rope_tile(kpe, tab_ref, 0, MLA_ROPE // 2)
    yn = y[:, P_NSA:P_NSA + 768]
    qn_ref[:, 0:128] = _rope_tile(yn[:, 0:128], tab_ref, 1, NSA_ROT // 2)
    qn_ref[:, 128:256] = _rope_tile(yn[:, 128:256], tab_ref, 1, NSA_ROT // 2)
    kvc_ref[...] = _rope_tile(yn[:, 256:384], tab_ref, 2, NSA_ROT // 2)
    s0 = (pl.program_id(0) % seq_blocks) * tm
    pos = s0 + lax.broadcasted_iota(jnp.int32, (tm, SEL_LANES), 0)
    onehot = jnp.where(jnp.right_shift(pos, 6) == lane, 1.0, 0.0)
    for k_ref, v_ref, c0, extra in ((ksel_ref, vsel_ref, 384, onehot), (kwin_ref, vwin_ref, 512, None)):
        kv = yn[:, c0:c0 + 128]
        k_ref[:, 0:128] = jnp.where(lo64, _rope_tile(kv, tab_ref, 2, NSA_ROT // 2), 0.0).astype(BF16)
        k_ref[:, 128:256] = (jnp.zeros((tm, 128), F32) if extra is None else extra).astype(BF16)
        v_ref[...] = jnp.where(lo64, pltpu.roll(kv, 64, axis=1), 1.0).astype(BF16)
    gate_ref[...] = jnp.where(lane < 3 * NSA_HEADS, yn[:, 640:768], 0.0)
    yl = y[:, P_LRU:P_LRU + 512 + SSD_IN_W]
    uc_ref[...] = yl[:, 0:512]
    ud_ref[:, 0:SSD_IN_W - 128] = yl[:, 512:512 + SSD_IN_W - 128]
    ud_ref[:, SSD_IN_W - 128:SSD_IN_W] = jnp.where(lane < SSD_HEADS, yl[:, 512 + SSD_IN_W - 128:512 + SSD_IN_W], 0.0)


def _inproj(x2d, g, w_raw, tab, seq):
    T = x2d.shape[0]
    tm = 512
    seq_blocks = seq // tm
    row = lambda w: pl.BlockSpec((tm, w), lambda i: (i, 0))
    full = lambda a: pl.BlockSpec(a.shape, lambda i: (0,) * a.ndim)
    outs = [(512, F32), (256, F32), (128, F32), (256, BF16), (256, BF16), (128, BF16), (128, BF16), (128, F32),
            (512, F32), (SSD_IN_W, F32)]
    return pl.pallas_call(
        functools.partial(_inproj_kernel, tm=tm, seq_blocks=seq_blocks),
        grid=(T // tm,),
        in_specs=[row(D_MODEL), full(g), full(w_raw), pl.BlockSpec((tm, tab.shape[1]), lambda i: (i % seq_blocks, 0))],
        out_specs=[row(w) for w, _ in outs],
        out_shape=[jax.ShapeDtypeStruct((T, w), dt) for w, dt in outs],
        compiler_params=_cparams(("parallel",)),
        name="inproj",
    )(x2d, g, w_raw, tab)


def _mla_prep_kernel(ua_ref, qg_ref, w1_ref, w2_ref, kvg_ref, wk_ref, wv_ref, cos_ref, sin_ref,
                     q_ref, k_ref, v_ref):
    ua = ua_ref[...]
    cq = _rms(ua[:, 0:256], MLA_Q_RANK) * qg_ref[...]
    y1 = _dot(cq, w1_ref[...])
    y2 = _dot(cq, w2_ref[...])
    ckv = _rms(ua[:, 256:384], MLA_KV_RANK) * kvg_ref[...]
    kn = _dot(ckv, wk_ref[...])
    vv = _dot(ckv, wv_ref[...])
    kpe = ua[:, 384:512]
    cos = cos_ref[...]
    sin = sin_ref[...]
    scale = LOG2E * (MLA_NOPE + MLA_ROPE) ** -0.5
    ones_hi = jnp.where(lax.broadcasted_iota(jnp.int32, cos.shape, 1) >= 64, 1.0, 0.0)
    for h in range(MLA_HEADS):
        sl = slice(h * 128, (h + 1) * 128)
        q_ref[h] = ((y1[:, sl] * cos + y2[:, sl] * sin) * scale).astype(BF16)
        k_ref[h] = (kn[:, sl] + kpe).astype(BF16)
        v_ref[h] = (vv[:, sl] + ones_hi).astype(BF16)


def _mla_prep(ua, qg, w1, w2, kvg, wk, wv, cos_m, sin_m, B, S):
    tm = 1024
    nb = S // tm
    full = lambda a: pl.BlockSpec(a.shape, lambda b, i: (0,) * a.ndim)
    tab = pl.BlockSpec((tm, 128), lambda b, i: (i, 0))
    hd = lambda w: pl.BlockSpec((None, MLA_HEADS, tm, w), lambda b, i: (b, 0, i, 0))
    return pl.pallas_call(
        _mla_prep_kernel,
        grid=(B, nb),
        in_specs=[pl.BlockSpec((tm, 512), lambda b, i: (b * nb + i, 0)),
                  full(qg), full(w1), full(w2), full(kvg), full(wk), full(wv), tab, tab],
        out_specs=[hd(128), hd(128), hd(128)],
        out_shape=[jax.ShapeDtypeStruct((B, MLA_HEADS, S, 128), BF16)] * 3,
        compiler_params=_cparams(("parallel", "parallel")),
        name="mla_prep",
    )(ua, qg, w1, w2, kvg, wk, wv, cos_m, sin_m)


def _flash_kernel(qi_ref, ki_ref, flag_ref, q_ref, k_ref, v_ref, o_ref, m_ref, acc_ref,
                  *, heads, kv_heads, tq, tk, window):
    p_idx = pl.program_id(1)
    qi = qi_ref[p_idx]
    ki = ki_ref[p_idx]
    flags = flag_ref[p_idx]

    @pl.when(jnp.bitwise_and(flags, 1) != 0)
    def _init():
        m_ref[...] = jnp.full(m_ref.shape, NEG, F32)
        acc_ref[...] = jnp.zeros(acc_ref.shape, F32)

    def step(masked):
        if masked:
            qpos = qi * tq + lax.broadcasted_iota(jnp.int32, (tq, tk), 0)
            kpos = ki * tk + lax.broadcasted_iota(jnp.int32, (tq, tk), 1)
            mask = kpos <= qpos
            if window is not None:
                mask = mask & (kpos > qpos - window)
        nt = (((1,), (1,)), ((), ()))
        shared = kv_heads == 1
        if shared:
            dk = q_ref.shape[-1]
            s_all = lax.dot_general(q_ref[...].reshape(heads * tq, dk), k_ref[0], nt, preferred_element_type=F32)
        ps, alphas = [], []
        for h in range(heads):
            if shared:
                s = s_all[h * tq:(h + 1) * tq]
            else:
                s = lax.dot_general(q_ref[h], k_ref[h], nt, preferred_element_type=F32)
            if masked:
                s = jnp.where(mask, s, NEG)
            m_old = m_ref[h]
            m_new = jnp.maximum(m_old, jnp.max(s, axis=-1, keepdims=True))
            alpha = jnp.exp2(m_old - m_new)
            p = jnp.concatenate([jnp.exp2(s[:, c * 128:(c + 1) * 128] - m_new) for c in range(tk // 128)],
                                axis=1).astype(BF16)
            m_ref[h] = m_new
            if shared:
                ps.append(p)
                alphas.append(alpha)
            else:
                acc_ref[h] = alpha * acc_ref[h] + jnp.dot(p, v_ref[h], preferred_element_type=F32)
        if shared:
            pv = jnp.dot(jnp.concatenate(ps, axis=0), v_ref[0], preferred_element_type=F32)
            for h in range(heads):
                acc_ref[h] = alphas[h] * acc_ref[h] + pv[h * tq:(h + 1) * tq]

    @pl.when(jnp.bitwise_and(flags, 4) != 0)
    def _masked():
        step(True)

    @pl.when(jnp.bitwise_and(flags, 4) == 0)
    def _plain():
        step(False)

    @pl.when(jnp.bitwise_and(flags, 2) != 0)
    def _fin():
        for h in range(heads):
            acc = acc_ref[h]
            o_ref[:, h * 64:(h + 1) * 64] = acc[:, 0:64] / acc[:, 64:128]


def _pair_tables(S, tq, tk, window):
    qi, ki, flags = [], [], []
    for i in range(S // tq):
        q_lo, q_hi = i * tq, i * tq + tq - 1
        lo = 0 if window is None else max(0, q_lo - window + 1) // tk
        hi = q_hi // tk
        for j in range(lo, hi + 1):
            k_lo, k_hi = j * tk, j * tk + tk - 1
            masked = k_hi > q_lo or (window is not None and k_lo <= q_hi - window)
            qi.append(i)
            ki.append(j)
            flags.append((1 if j == lo else 0) | (2 if j == hi else 0) | (4 if masked else 0))
    mk = lambda a: jnp.asarray(np.asarray(a, np.int32))
    return mk(qi), mk(ki), mk(flags)


def _flash(q, k, v, *, window, tq, tk=512):
    B, H, S, dk = q.shape
    Hk = k.shape[1]
    tq, tk = min(tq, S), min(tk, S)
    tabs = _pair_tables(S, tq, tk, window)
    npairs = int(tabs[0].shape[0])
    grid_spec = pltpu.PrefetchScalarGridSpec(
        num_scalar_prefetch=3,
        grid=(B, npairs),
        in_specs=[pl.BlockSpec((None, H, tq, dk), lambda b, p, qi, ki, fl: (b, 0, qi[p], 0)),
                  pl.BlockSpec((None, Hk, tk, dk), lambda b, p, qi, ki, fl: (b, 0, ki[p], 0)),
                  pl.BlockSpec((None, Hk, tk, 128), lambda b, p, qi, ki, fl: (b, 0, ki[p], 0))],
        out_specs=pl.BlockSpec((None, tq, H * 64), lambda b, p, qi, ki, fl: (b, qi[p], 0)),
        scratch_shapes=[pltpu.VMEM((H, tq, 128), F32), pltpu.VMEM((H, tq, 128), F32)],
    )
    return pl.pallas_call(
        functools.partial(_flash_kernel, heads=H, kv_heads=Hk, tq=tq, tk=tk, window=window),
        grid_spec=grid_spec,
        out_shape=jax.ShapeDtypeStruct((B, S, H * 64), F32),
        compiler_params=_cparams(("parallel", "arbitrary")),
        name="flash_attn",
    )(*tabs, q, k, v)


def _compress_kernel(x_ref, wa_ref, wb_ref, pek_ref, pev_ref, wkf_ref, wvf_ref, o_ref, *, nc_pad):
    x = x_ref[...]
    a = _dot_f32(x, wa_ref[...])
    b = _dot_f32(x, wb_ref[...])
    ck = _dot_f32(pek_ref[...], wkf_ref[...])[0:1]
    cv = _dot_f32(pev_ref[...], wvf_ref[...])[0:1]
    const = jnp.concatenate([ck, cv], axis=-1)
    b_next = pltpu.roll(b, nc_pad - 1, axis=0)
    row = lax.broadcasted_iota(jnp.int32, (nc_pad, 128), 0)
    o_ref[...] = jnp.where(row < nc_pad - 1, a + b_next + const, 0.0)


def _compress(x, wa, wb, pek, pev, wkf, wvf):
    B, nc_pad, w = x.shape
    full = lambda a: pl.BlockSpec(a.shape, lambda b: (0,) * a.ndim)
    return pl.pallas_call(
        functools.partial(_compress_kernel, nc_pad=nc_pad),
        grid=(B,),
        in_specs=[pl.BlockSpec((None, nc_pad, w), lambda b: (b, 0, 0)),
                  full(wa), full(wb), full(pek), full(pev), full(wkf), full(wvf)],
        out_specs=pl.BlockSpec((None, nc_pad, 128), lambda b: (b, 0, 0)),
        out_shape=jax.ShapeDtypeStruct((B, nc_pad, 128), F32),
        compiler_params=_cparams(("parallel",)),
        name="nsa_compress",
    )(x, wa, wb, pek, pev, wkf, wvf)


def _cmp_select_kernel(q_ref, kvc_ref, ov_ref, oc_ref, qaug_ref, *, tq, nc_pad, n_top):
    i = pl.program_id(1)
    q = q_ref[...]
    kc = kvc_ref[:, 0:64]
    vc = kvc_ref[:, 64:128]
    scale = NSA_HD ** -0.5
    qpos = i * tq + lax.broadcasted_iota(jnp.int32, (tq, 1), 0)
    n_idx = lax.broadcasted_iota(jnp.int32, (1, nc_pad), 1)
    m_c = (n_idx * CMP_STRIDE + (CMP_LEN - 1) <= qpos) & (n_idx < nc_pad - 1)
    kc_hi, kc_lo = _split2(kc)
    kc3 = jnp.concatenate([kc_hi, kc_lo, kc_hi], axis=1)
    psum = jnp.zeros((tq, nc_pad), F32)
    for h in range(NSA_HEADS):
        q_hi, q_lo = _split2(q[:, h * NSA_HD:(h + 1) * NSA_HD])
        q3 = jnp.concatenate([q_hi, q_hi, q_lo], axis=1)
        s = lax.dot_general(q3, kc3, (((1,), (1,)), ((), ())), preferred_element_type=F32)
        s = jnp.where(m_c, s * scale, NEG)
        e = jnp.where(m_c, jnp.exp(s - jnp.max(s, axis=-1, keepdims=True)), 0.0)
        den = jnp.sum(e, axis=-1, keepdims=True)
        p = e / jnp.where(den > 0.0, den, 1.0)
        oc_ref[:, h * NSA_HD:(h + 1) * NSA_HD] = _dot(p, vc)
        psum = psum + p
    ov = ov_ref[...]
    p_hi = psum.astype(BF16)
    p_r = psum - p_hi.astype(F32)
    p_mid = p_r.astype(BF16)
    p_lo = (p_r - p_mid.astype(F32)).astype(BF16)
    imp = (jnp.dot(p_hi, ov, preferred_element_type=F32) + jnp.dot(p_mid, ov, preferred_element_type=F32)
           + jnp.dot(p_lo, ov, preferred_element_type=F32))
    cur = jnp.right_shift(qpos, 6)
    jj = lax.broadcasted_iota(jnp.int32, (1, SEL_LANES), 1)
    forced = (jj == 0) | (jj == cur) | (jj == cur - 1)
    bias = jnp.where(forced, 0.0, NEG)
    imp = jnp.where((jj <= cur) & jnp.logical_not(forced), imp, NEG)
    jf = jj.astype(F32)
    for _ in range(n_top - 3):
        mx = jnp.max(imp, axis=-1, keepdims=True)
        idx = jnp.min(jnp.where(imp == mx, jf, float(SEL_LANES)), axis=-1, keepdims=True)
        hit = jf == idx
        bias = jnp.where(hit & (mx > 0.5 * NEG), 0.0, bias)
        imp = jnp.where(hit, -3e38, imp)
    bias = bias.astype(BF16)
    qs = (q * (scale * LOG2E)).astype(BF16)
    for h in range(NSA_HEADS):
        qaug_ref[h, :, 0:64] = qs[:, h * NSA_HD:(h + 1) * NSA_HD]
        qaug_ref[h, :, 64:128] = jnp.zeros((tq, 64), BF16)
        qaug_ref[h, :, 128:256] = bias


def _cmp_select(qn, kvcmp, ov, B, S):
    tq = min(1024, S)
    nb = S // tq
    nc_pad = kvcmp.shape[1]
    n_top = min(SEL_TOPN, S // SEL_LEN)
    assert n_top >= 3
    return pl.pallas_call(
        functools.partial(_cmp_select_kernel, tq=tq, nc_pad=nc_pad, n_top=n_top),
        grid=(B, nb),
        in_specs=[pl.BlockSpec((tq, 256), lambda b, i: (b * nb + i, 0)),
                  pl.BlockSpec((None, nc_pad, 128), lambda b, i: (b, 0, 0)),
                  pl.BlockSpec(ov.shape, lambda b, i: (0, 0))],
        out_specs=[pl.BlockSpec((tq, 256), lambda b, i: (b * nb + i, 0)),
                   pl.BlockSpec((None, NSA_HEADS, tq, 256), lambda b, i: (b, 0, i, 0))],
        out_shape=[jax.ShapeDtypeStruct((B * S, 256), F32),
                   jax.ShapeDtypeStruct((B, NSA_HEADS, S, 256), BF16)],
        compiler_params=_cparams(("parallel", "parallel")),
        name="nsa_cmp_select",
    )(qn, kvcmp, ov)


def _shift_scan(a, b, t, width):
    row = lax.broadcasted_iota(jnp.int32, (t, width), 0)
    s = 1
    while s < t:
        keep = row >= s
        a_sh = jnp.where(keep, pltpu.roll(a, s, axis=0), 1.0)
        b_sh = jnp.where(keep, pltpu.roll(b, s, axis=0), 0.0)
        b = a * b_sh + b
        a = a * a_sh
        s *= 2
    return a, b


def _causal_conv(xpad_ref, x, cw_ref, cb_ref, t):
    xpad_ref[8:8 + t, :] = x
    y = cb_ref[...] + cw_ref[CONV_W - 1:CONV_W, :] * x
    for k in range(CONV_W - 1):
        off = 8 - (CONV_W - 1) + k
        y = y + cw_ref[k:k + 1, :] * xpad_ref[off:off + t, :]
    xpad_ref[0:8, :] = x[t - 8:t, :]
    return y


def _lru_kernel(u_ref, cw_ref, cb_ref, wg_ref, bg_ref, sp_ref, y_ref, xpad_ref, h_ref, *, t):
    @pl.when(pl.program_id(1) == 0)
    def _init():
        xpad_ref[0:8, :] = jnp.zeros((8, LRU_W), F32)
        h_ref[...] = jnp.zeros(h_ref.shape, F32)

    x = u_ref[:, 0:LRU_W]
    gbr = u_ref[:, LRU_W:2 * LRU_W]
    xb = _causal_conv(xpad_ref, x, cw_ref, cb_ref, t)
    g = _dot(xb, wg_ref[...]) + bg_ref[...]
    r = _sigmoid(g[:, 0:LRU_W])
    ig = _sigmoid(g[:, LRU_W:2 * LRU_W])
    log_a = -LRU_C * r * sp_ref[...]
    a = jnp.exp(log_a)
    one_m = -jnp.tanh(log_a) * (a * a + 1.0)
    b = jnp.sqrt(jnp.maximum(one_m, 0.0)) * (ig * xb)
    a_cum, h = _shift_scan(a, b, t, LRU_W)
    h = h + a_cum * h_ref[0:1, :]
    h_ref[0:1, :] = h[t - 1:t, :]
    gelu = 0.5 * gbr * (1.0 + jnp.tanh(0.7978845608028654 * (gbr + 0.044715 * gbr * gbr * gbr)))
    y_ref[...] = h * gelu


def _lru(uc, cw, cb, wg, bg, sp, B, S):
    t = 512
    nb = S // t
    full = lambda a: pl.BlockSpec(a.shape, lambda b, i: (0,) * a.ndim)
    return pl.pallas_call(
        functools.partial(_lru_kernel, t=t),
        grid=(B, nb),
        in_specs=[pl.BlockSpec((t, 2 * LRU_W), lambda b, i: (b * nb + i, 0)),
                  full(cw), full(cb), full(wg), full(bg), full(sp)],
        out_specs=pl.BlockSpec((t, LRU_W), lambda b, i: (b * nb + i, 0)),
        out_shape=jax.ShapeDtypeStruct((B * S, LRU_W), F32),
        scratch_shapes=[pltpu.VMEM((t + 8, LRU_W), F32), pltpu.VMEM((8, LRU_W), F32)],
        compiler_params=_cparams(("parallel", "arbitrary")),
        name="rglru",
    )(uc, cw, cb, wg, bg, sp)


def _ssd_kernel(u_ref, cw_ref, cb_ref, dtb_ref, a_ref, d_ref, ng_ref, y_ref, xpad_ref, st_ref, *, t):
    @pl.when(pl.program_id(1) == 0)
    def _init():
        xpad_ref[0:8, :] = jnp.zeros((8, SSD_XBC), F32)
        st_ref[...] = jnp.zeros(st_ref.shape, F32)

    L = SSD_CHUNK
    z = u_ref[:, 0:SSD_DI]
    conv = _causal_conv(xpad_ref, u_ref[:, SSD_DI:SSD_DI + SSD_XBC], cw_ref, cb_ref, t)
    xbc = conv * _sigmoid(conv)
    dt = _softplus(u_ref[:, SSD_DI + SSD_XBC:SSD_IN_W] + dtb_ref[...])
    a = dt * a_ref[...]
    row = lax.broadcasted_iota(jnp.int32, (t, 128), 0)
    rin = jnp.bitwise_and(row, L - 1)
    cs = a
    s = 1
    while s < L:
        cs = cs + jnp.where(rin >= s, pltpu.roll(cs, s, axis=0), 0.0)
        s *= 2
    tril = lax.broadcasted_iota(jnp.int32, (L, L), 0) >= lax.broadcasted_iota(jnp.int32, (L, L), 1)
    gn = SSD_GROUPS * SSD_STATE
    rep = SSD_HEADS // SSD_GROUPS
    for c in range(t // L):
        rs = slice(c * L, (c + 1) * L)
        cs_c = cs[rs]
        cs_t = cs_c.T
        cs_last = cs_c[L - 1:L, :]
        ys = []
        for g in range(SSD_GROUPS):
            bg = xbc[rs, SSD_DI + g * SSD_STATE:SSD_DI + (g + 1) * SSD_STATE]
            cg = xbc[rs, SSD_DI + gn + g * SSD_STATE:SSD_DI + gn + (g + 1) * SSD_STATE]
            gmat = _dot_nt(cg, bg)
            bg_t = bg.T
            for hh in range(rep):
                h = g * rep + hh
                cs_col = cs_c[:, h:h + 1]
                lm = jnp.exp(jnp.where(tril, cs_col - cs_t[h:h + 1, :], NEG))
                xh = xbc[rs, h * SSD_HD:(h + 1) * SSD_HD]
                xdt = xh * dt[rs, h:h + 1]
                st_old = st_ref[h]
                y_h = _dot(gmat * lm, xdt) + _dot(cg, st_old) * jnp.exp(cs_col)
                dec = jnp.exp(cs_last[:, h:h + 1] - cs_col)
                st_ref[h] = jnp.exp(cs_last[:, h:h + 1]) * st_old + _dot(bg_t, xdt * dec)
                ys.append(y_h)
        y = jnp.concatenate(ys, axis=-1) + xbc[rs, 0:SSD_DI] * d_ref[...]
        zc = z[rs]
        y = y * (zc * _sigmoid(zc))
        y_ref[rs, :] = _rms(y, SSD_DI) * ng_ref[...]


def _ssd(ud, cw, cb, dtb, a_neg, d_vec, ng, B, S):
    t = 512
    nb = S // t
    full = lambda a: pl.BlockSpec(a.shape, lambda b, i: (0,) * a.ndim)
    return pl.pallas_call(
        functools.partial(_ssd_kernel, t=t),
        grid=(B, nb),
        in_specs=[pl.BlockSpec((t, SSD_IN_W), lambda b, i: (b * nb + i, 0)),
                  full(cw), full(cb), full(dtb), full(a_neg), full(d_vec), full(ng)],
        out_specs=pl.BlockSpec((t, SSD_DI), lambda b, i: (b * nb + i, 0)),
        out_shape=jax.ShapeDtypeStruct((B * S, SSD_DI), F32),
        scratch_shapes=[pltpu.VMEM((t + 8, SSD_XBC), F32), pltpu.VMEM((SSD_HEADS, SSD_STATE, SSD_HD), F32)],
        compiler_params=_cparams(("parallel", "arbitrary")),
        name="ssd",
    )(ud, cw, cb, dtb, a_neg, d_vec, ng)


def _outproj_kernel(*refs, with_router):
    if with_router:
        (ya_ref, oc_ref, os_ref, ow_ref, gate_ref, yc_ref, yd_ref, res_ref, gn_ref, ex_ref, w_ref, nf_ref,
         rw_ref, hres_ref, hn_ref, rg_ref) = refs
    else:
        (ya_ref, oc_ref, os_ref, ow_ref, gate_ref, yc_ref, yd_ref, res_ref, gn_ref, ex_ref, w_ref, nf_ref,
         hres_ref, hn_ref) = refs
    sg_hi, sg_lo = _split2(_sigmoid(gate_ref[...]))
    ex = ex_ref[...]
    gx = jnp.dot(sg_hi, ex, preferred_element_type=F32) + jnp.dot(sg_lo, ex, preferred_element_type=F32)
    yb = gx[:, 0:256] * oc_ref[...] + gx[:, 256:512] * os_ref[...] + gx[:, 512:768] * ow_ref[...]
    y = jnp.concatenate([_rms(ya_ref[...], GROUP_W) * gn_ref[0:1, :],
                         _rms(yb, GROUP_W) * gn_ref[1:2, :],
                         _rms(yc_ref[...], GROUP_W) * gn_ref[2:3, :],
                         yd_ref[...]], axis=-1)
    hres = res_ref[...] + _dot(y, w_ref[...])
    hres_ref[...] = hres
    hn = _rms(hres, D_MODEL) * nf_ref[...]
    hn_ref[...] = hn.astype(hn_ref.dtype)
    if with_router:
        h_hi, h_lo = _split2(hn)
        logits = (jnp.dot(h_hi, rw_ref[0], preferred_element_type=F32)
                  + jnp.dot(h_hi, rw_ref[1], preferred_element_type=F32)
                  + jnp.dot(h_lo, rw_ref[0], preferred_element_type=F32))
        lane = lax.broadcasted_iota(jnp.int32, logits.shape, 1)
        lf = lane.astype(F32)
        logits = jnp.where(lane < N_EXPERTS, logits, NEG)
        m1 = jnp.max(logits, axis=-1, keepdims=True)
        i1 = jnp.min(jnp.where(logits == m1, lf, 128.0), axis=-1, keepdims=True)
        rest = jnp.where(lf == i1, NEG, logits)
        m2 = jnp.max(rest, axis=-1, keepdims=True)
        i2 = jnp.min(jnp.where(rest == m2, lf, 128.0), axis=-1, keepdims=True)
        e2 = jnp.exp(m2 - m1)
        den = 1.0 + e2
        rg_ref[...] = jnp.where(lane == 0, i1, jnp.where(lane == 1, i2, jnp.where(
            lane == 2, 1.0 / den, jnp.where(lane == 3, e2 / den, 0.0))))


def _outproj(ya, oc, osel, ow, gate, yc, yd, res, gn, ex, w, nf, rw):
    T = ya.shape[0]
    tm = 512
    with_router = rw is not None
    row = lambda wd: pl.BlockSpec((tm, wd), lambda i: (i, 0))
    full = lambda a: pl.BlockSpec(a.shape, lambda i: (0,) * a.ndim)
    ins = [ya, oc, osel, ow, gate, yc, yd, res, gn, ex, w, nf]
    in_specs = [row(256), row(256), row(256), row(256), row(128), row(256), row(256), row(D_MODEL),
                full(gn), full(ex), full(w), full(nf)]
    out_specs = [row(D_MODEL), row(D_MODEL)]
    out_shape = [jax.ShapeDtypeStruct((T, D_MODEL), F32),
                 jax.ShapeDtypeStruct((T, D_MODEL), F32 if with_router else BF16)]
    if with_router:
        ins.append(rw)
        in_specs.append(full(rw))
        out_specs.append(row(128))
        out_shape.append(jax.ShapeDtypeStruct((T, 128), F32))
    return pl.pallas_call(
        functools.partial(_outproj_kernel, with_router=with_router),
        grid=(T // tm,),
        in_specs=in_specs,
        out_specs=out_specs,
        out_shape=out_shape,
        compiler_params=_cparams(("parallel",)),
        name="outproj",
    )(*ins)


def _ffn_kernel(h_ref, res_ref, wg_ref, wu_ref, wd_ref, o_ref, acc_ref):
    j = pl.program_id(1)

    @pl.when(j == 0)
    def _init():
        acc_ref[...] = jnp.zeros(acc_ref.shape, F32)

    h = h_ref[...]
    g = jnp.dot(h, wg_ref[...], preferred_element_type=F32)
    u = jnp.dot(h, wu_ref[...], preferred_element_type=F32)
    acc_ref[...] += _dot(g * _sigmoid(g) * u, wd_ref[...])

    @pl.when(j == pl.num_programs(1) - 1)
    def _fin():
        o_ref[...] = res_ref[...] + acc_ref[...]


def _ffn(hn, res, wg, wu, wd):
    T = hn.shape[0]
    tm, tf = 1024, 512
    return pl.pallas_call(
        _ffn_kernel,
        grid=(T // tm, D_FF // tf),
        in_specs=[pl.BlockSpec((tm, D_MODEL), lambda i, j: (i, 0)),
                  pl.BlockSpec((tm, D_MODEL), lambda i, j: (i, 0)),
                  pl.BlockSpec((D_MODEL, tf), lambda i, j: (0, j)),
                  pl.BlockSpec((D_MODEL, tf), lambda i, j: (0, j)),
                  pl.BlockSpec((tf, D_MODEL), lambda i, j: (j, 0))],
        out_specs=pl.BlockSpec((tm, D_MODEL), lambda i, j: (i, 0)),
        out_shape=jax.ShapeDtypeStruct((T, D_MODEL), F32),
        scratch_shapes=[pltpu.VMEM((tm, D_MODEL), F32)],
        compiler_params=_cparams(("parallel", "arbitrary")),
        name="ffn_dense",
    )(hn, res, wg, wu, wd)


MOE_TILE = 1024
MOE_TM = 512


def _route_rank_kernel(route_ref, rank_ref, cnt_ref, carry_ref, *, tm):
    @pl.when(pl.program_id(0) == 0)
    def _init():
        carry_ref[...] = jnp.zeros(carry_ref.shape, F32)

    r = route_ref[...]
    i1 = r[:, 0:1]
    i2 = r[:, 1:2]
    lane = lax.broadcasted_iota(jnp.int32, (tm, 128), 1)
    lf = lane.astype(F32)
    oh = jnp.where((lf == i1) | (lf == i2), 1.0, 0.0)
    row = lax.broadcasted_iota(jnp.int32, (tm, 128), 0)
    cs = oh
    s = 1
    while s < tm:
        cs = cs + jnp.where(row >= s, pltpu.roll(cs, s, axis=0), 0.0)
        s *= 2
    excl = cs - oh + carry_ref[0:1, :]
    rank1 = jnp.sum(jnp.where(lf == i1, excl, 0.0), axis=-1, keepdims=True)
    rank2 = jnp.sum(jnp.where(lf == i2, excl, 0.0), axis=-1, keepdims=True)
    rank_ref[...] = jnp.where(lane == 0, rank1, jnp.where(lane == 1, rank2, 0.0))
    carry_ref[0:1, :] = carry_ref[0:1, :] + cs[tm - 1:tm, :]
    cnt_ref[...] = carry_ref[...]


def _route_rank(route):
    T = route.shape[0]
    tm = MOE_TM
    return pl.pallas_call(
        functools.partial(_route_rank_kernel, tm=tm),
        grid=(T // tm,),
        in_specs=[pl.BlockSpec((tm, 128), lambda i: (i, 0))],
        out_specs=[pl.BlockSpec((tm, 128), lambda i: (i, 0)), pl.BlockSpec((8, 128), lambda i: (0, 0))],
        out_shape=[jax.ShapeDtypeStruct((T, 128), F32), jax.ShapeDtypeStruct((8, 128), F32)],
        scratch_shapes=[pltpu.VMEM((8, 128), F32)],
        compiler_params=_cparams(("arbitrary",)),
        name="moe_rank",
    )(route)


def _row_copies(idx_ref, tm, make_copy):
    def body(r, carry):
        for k in range(2):
            make_copy(k, r, idx_ref[k * tm + r]).start(priority=k)
        return carry

    lax.fori_loop(0, tm, body, 0, unroll=8)


def _dispatch_kernel(ends_ref, slots_ref, h_ref, xs_ref, idx_ref, zero_ref, sem_idx, sem_row, *, tm):
    i = pl.program_id(0)

    @pl.when(i == 0)
    def _zero_tiles():
        zero_ref[...] = jnp.zeros(zero_ref.shape, F32)
        zr = zero_ref.shape[0]
        jobs = []
        for e in range(N_EXPERTS):
            start = ends_ref[e - 1] if e else 0
            jobs.append((ends_ref[e] > start, ends_ref[e] - MOE_TILE))
        for tile in range(N_EXPERTS):
            base = xs_ref.shape[0] - (tile + 1) * MOE_TILE
            jobs.append((base >= ends_ref[N_EXPERTS - 1], base))

        def tile_copies(base):
            rows = [base + q * zr for q in range(MOE_TILE // zr)]
            rows = [r if isinstance(r, int) else pl.multiple_of(r, zr) for r in rows]
            return [pltpu.make_async_copy(zero_ref, xs_ref.at[pl.ds(r, zr)], sem_idx) for r in rows]

        for cond, base in jobs:
            @pl.when(cond)
            def _start(base=base):
                for cp in tile_copies(base):
                    cp.start()
        for cond, base in jobs:
            @pl.when(cond)
            def _wait(base=base):
                for cp in tile_copies(base):
                    cp.wait()

    cp = pltpu.make_async_copy(slots_ref.at[pl.ds(i * 2 * tm, 2 * tm)], idx_ref, sem_idx)
    cp.start()
    cp.wait()
    cur = lax.rem(i, 2)
    _row_copies(idx_ref, tm, lambda k, r, s: pltpu.make_async_copy(
        h_ref.at[pl.ds(i * tm + r, 1)], xs_ref.at[pl.ds(s, 1)], sem_row.at[cur]))

    def wait_tile(b):
        for _ in range(2):
            pltpu.make_async_copy(h_ref.at[pl.ds(0, tm)], xs_ref.at[pl.ds(0, tm)], sem_row.at[b]).wait()

    @pl.when(i > 0)
    def _prev():
        wait_tile(1 - cur)

    @pl.when(i == pl.num_programs(0) - 1)
    def _last():
        wait_tile(cur)


def _dispatch(ends, slots, hn, n_slots):
    T = hn.shape[0]
    tm = MOE_TM
    grid_spec = pltpu.PrefetchScalarGridSpec(
        num_scalar_prefetch=1,
        grid=(T // tm,),
        in_specs=[pl.BlockSpec(memory_space=pl.ANY), pl.BlockSpec(memory_space=pl.ANY)],
        out_specs=pl.BlockSpec(memory_space=pl.ANY),
        scratch_shapes=[pltpu.SMEM((2 * tm,), jnp.int32), pltpu.VMEM((256, D_MODEL), F32),
                        pltpu.SemaphoreType.DMA, pltpu.SemaphoreType.DMA((2,))],
    )
    return pl.pallas_call(
        functools.partial(_dispatch_kernel, tm=tm),
        grid_spec=grid_spec,
        out_shape=jax.ShapeDtypeStruct((n_slots, D_MODEL), F32),
        compiler_params=_cparams(("arbitrary",)),
        name="moe_dispatch",
    )(ends, slots, hn)


def _expert_ffn_kernel(te_ref, nu_ref, x_ref, wg_ref, wu_ref, wd_ref, o_ref, acc_ref):
    del te_ref
    i = pl.program_id(0)
    j = pl.program_id(1)
    last = pl.num_programs(1) - 1
    used = i < nu_ref[0]

    @pl.when(used & (j == 0))
    def _init():
        acc_ref[...] = jnp.zeros(acc_ref.shape, F32)

    @pl.when(used)
    def _step():
        x = x_ref[...]
        g = _dot(x, wg_ref[...])
        u = _dot(x, wu_ref[...])
        acc_ref[...] += _dot(g * _sigmoid(g) * u, wd_ref[...])

    @pl.when(used & (j == last))
    def _fin():
        o_ref[...] = acc_ref[...]

    @pl.when(jnp.logical_not(used) & (j == last))
    def _unused():
        o_ref[...] = jnp.zeros(o_ref.shape, F32)


def _expert_ffn(tile_expert, n_used, xs, wg, wu, wd):
    n_slots = xs.shape[0]
    tm, tf = MOE_TILE, 512
    jw = lambda i, j, nu: jnp.where(i < nu[0], j, 0)
    grid_spec = pltpu.PrefetchScalarGridSpec(
        num_scalar_prefetch=2,
        grid=(n_slots // tm, D_FF // tf),
        in_specs=[pl.BlockSpec((tm, D_MODEL), lambda i, j, te, nu: (jnp.minimum(i, nu[0] - 1), 0)),
                  pl.BlockSpec((None, D_MODEL, tf), lambda i, j, te, nu: (te[i], 0, jw(i, j, nu))),
                  pl.BlockSpec((None, D_MODEL, tf), lambda i, j, te, nu: (te[i], 0, jw(i, j, nu))),
                  pl.BlockSpec((None, tf, D_MODEL), lambda i, j, te, nu: (te[i], jw(i, j, nu), 0))],
        out_specs=pl.BlockSpec((tm, D_MODEL), lambda i, j, te, nu: (i, 0)),
        scratch_shapes=[pltpu.VMEM((tm, D_MODEL), F32)],
    )
    return pl.pallas_call(
        _expert_ffn_kernel,
        grid_spec=grid_spec,
        out_shape=jax.ShapeDtypeStruct((n_slots, D_MODEL), F32),
        compiler_params=_cparams(("parallel", "arbitrary")),
        name="moe_expert_ffn",
    )(tile_expert, n_used, xs, wg, wu, wd)


def _combine_kernel(slots_ref, ys_ref, route_ref, res_ref, ng_ref, o_ref, idx_ref, buf_ref, sem_idx, sem_row,
                    *, tm, final_norm):
    i = pl.program_id(0)
    cur = lax.rem(i, 2)

    def gather(step, b):
        cp = pltpu.make_async_copy(slots_ref.at[pl.ds(step * 2 * tm, 2 * tm)], idx_ref, sem_idx)
        cp.start()
        cp.wait()
        _row_copies(idx_ref, tm, lambda k, r, s: pltpu.make_async_copy(
            ys_ref.at[pl.ds(s, 1)], buf_ref.at[b, k, pl.ds(r, 1)], sem_row.at[b]))

    @pl.when(i == 0)
    def _first():
        gather(0, 0)

    @pl.when(i + 1 < pl.num_programs(0))
    def _next():
        gather(i + 1, 1 - cur)

    for k in range(2):
        pltpu.make_async_copy(ys_ref.at[pl.ds(0, tm)], buf_ref.at[cur, k], sem_row.at[cur]).wait()
    route = route_ref[...]
    out = res_ref[...] + route[:, 2:3] * buf_ref[cur, 0] + route[:, 3:4] * buf_ref[cur, 1]
    if final_norm:
        out = _rms(out, D_MODEL) * ng_ref[...]
    o_ref[...] = out


def _combine(slots, ys, route, res, norm_g, final_norm):
    T = res.shape[0]
    tm = MOE_TM
    return pl.pallas_call(
        functools.partial(_combine_kernel, tm=tm, final_norm=final_norm),
        grid=(T // tm,),
        in_specs=[pl.BlockSpec(memory_space=pl.ANY),
                  pl.BlockSpec(memory_space=pl.ANY),
                  pl.BlockSpec((tm, 128), lambda i: (i, 0)),
                  pl.BlockSpec((tm, D_MODEL), lambda i: (i, 0)),
                  pl.BlockSpec((1, D_MODEL), lambda i: (0, 0))],
        out_specs=pl.BlockSpec((tm, D_MODEL), lambda i: (i, 0)),
        out_shape=jax.ShapeDtypeStruct((T, D_MODEL), F32),
        scratch_shapes=[pltpu.SMEM((2 * tm,), jnp.int32), pltpu.VMEM((2, 2, tm, D_MODEL), F32),
                        pltpu.SemaphoreType.DMA, pltpu.SemaphoreType.DMA((2,))],
        compiler_params=_cparams(("arbitrary",)),
        name="moe_combine",
    )(slots, ys, route, res, norm_g)


def _moe(hn, res, route, wg, wu, wd, norm_g, final_norm):
    T = hn.shape[0]
    n_slots = 2 * T + N_EXPERTS * MOE_TILE
    n_tiles = n_slots // MOE_TILE
    rank, cnt = _route_rank(route)
    counts = cnt[0, :N_EXPERTS].astype(jnp.int32)
    padded = (counts + MOE_TILE - 1) // MOE_TILE * MOE_TILE
    ends = jnp.cumsum(padded)
    starts = ends - padded
    ids = route[:, 0:2].astype(jnp.int32)
    slot = jnp.take(starts, ids) + rank[:, 0:2].astype(jnp.int32)
    slots = slot.reshape(T // MOE_TM, MOE_TM, 2).transpose(0, 2, 1).reshape(-1)
    tile_start = jnp.arange(n_tiles, dtype=jnp.int32) * MOE_TILE
    tile_expert = jnp.minimum(jnp.sum(tile_start[:, None] >= ends[None, :], axis=1), N_EXPERTS - 1).astype(jnp.int32)
    n_used = (ends[-1:] // MOE_TILE).astype(jnp.int32)
    xs = _dispatch(ends.astype(jnp.int32), slots, hn, n_slots)
    ys = _expert_ffn(tile_expert, n_used, xs, wg, wu, wd)
    return _combine(slots, ys, route, res, norm_g, final_norm)


def _final_norm_kernel(x_ref, g_ref, o_ref):
    o_ref[...] = _rms(x_ref[...], D_MODEL) * g_ref[...]


def _final_norm(x2d, g):
    T = x2d.shape[0]
    tm = 1024
    return pl.pallas_call(
        _final_norm_kernel,
        grid=(T // tm,),
        in_specs=[pl.BlockSpec((tm, D_MODEL), lambda i: (i, 0)), pl.BlockSpec(g.shape, lambda i: (0, 0))],
        out_specs=pl.BlockSpec((tm, D_MODEL), lambda i: (i, 0)),
        out_shape=jax.ShapeDtypeStruct((T, D_MODEL), F32),
        compiler_params=_cparams(("parallel",)),
        name="final_norm",
    )(x2d, g)


def _rot_cols(w, half):
    return jnp.concatenate([-w[:, half:2 * half], w[:, 0:half]], axis=1)


def _aligned_w_in(w):
    z = lambda n: jnp.zeros((w.shape[0], n), F32)
    return jnp.concatenate([w[:, 0:O_NSA], z(P_NSA - O_NSA),
                            w[:, O_NSA:O_LRU], z(P_LRU - P_NSA - (O_LRU - O_NSA)),
                            w[:, O_LRU:], z(W_RAW - P_LRU - (w.shape[1] - O_LRU))], axis=1).astype(BF16)


def _rope_tables(positions):
    pos = positions.astype(F32)[:, None]
    S = positions.shape[0]

    def cs(rot_dim):
        inv = ROPE_THETA ** (-jnp.arange(0, rot_dim, 2, dtype=F32) / rot_dim)
        ang = pos * inv[None, :]
        return jnp.cos(ang), jnp.sin(ang)

    cm, sm = cs(MLA_ROPE)
    cn, sn = cs(NSA_ROT)
    hm, hn = MLA_ROPE // 2, NSA_ROT // 2
    base = jnp.concatenate([cm, sm, cn, sn, jnp.ones((S, 1), F32)], axis=1)
    c_cm, c_sm, c_cn, c_sn, c_one = 0, hm, 2 * hm, 2 * hm + hn, 2 * hm + 2 * hn

    place = np.zeros((base.shape[1], 11 * 128), np.float32)

    def put(tile, lane0, col0, n, sign=1.0):
        for j in range(n):
            place[col0 + j, tile * 128 + lane0 + j] = sign

    def ones(tile, lane0, n):
        place[c_one, tile * 128 + lane0:tile * 128 + lane0 + n] = 1.0

    ones(0, 0, 64), put(0, 64, c_cm, hm), put(0, 64 + hm, c_cm, hm), ones(0, 96, 32)
    put(1, 64 + hm, c_sm, hm)
    put(2, 64, c_sm, hm, -1.0)
    for t0, groups in ((3, (0, 64)), (6, (0,))):
        for g in groups:
            put(t0, g, c_cn, hn), put(t0, g + hn, c_cn, hn), ones(t0, g + 2 * hn, NSA_HD - 2 * hn)
            put(t0 + 1, g + hn, c_sn, hn)
            put(t0 + 2, g, c_sn, hn, -1.0)
    ones(6, 64, 64)
    ones(9, 0, 64), put(9, 64, c_cm, hm), put(9, 64 + hm, c_cm, hm), ones(9, 96, 32)
    put(10, 64, c_sm, hm), put(10, 64 + hm, c_sm, hm)
    tab = jnp.dot(base, jnp.asarray(place), precision=lax.Precision.HIGHEST)
    return tab[:, 0:9 * 128], tab[:, 9 * 128:10 * 128], tab[:, 10 * 128:11 * 128]


def _mla_weights(w_uq, w_ukv, q_norm, kv_norm):
    hw = MLA_NOPE + MLA_ROPE
    pad_r = lambda m: jnp.concatenate([m, jnp.zeros((256 - MLA_Q_RANK, m.shape[1]), F32)], axis=0)
    z = lambda n: jnp.zeros((MLA_Q_RANK, n), F32)
    zk = lambda n: jnp.zeros((MLA_KV_RANK, n), F32)
    w1, w2, wk, wv = [], [], [], []
    for h in range(MLA_HEADS):
        qh = w_uq[:, h * hw:(h + 1) * hw]
        w1 += [qh, z(128 - hw)]
        w2 += [z(MLA_NOPE), _rot_cols(qh[:, MLA_NOPE:], MLA_ROPE // 2), z(128 - hw)]
        kvh = w_ukv[:, h * (MLA_NOPE + MLA_V):(h + 1) * (MLA_NOPE + MLA_V)]
        wk += [kvh[:, :MLA_NOPE], zk(128 - MLA_NOPE)]
        wv += [kvh[:, MLA_NOPE:], zk(128 - MLA_V)]
    qg = jnp.concatenate([q_norm, jnp.zeros((256 - MLA_Q_RANK,), F32)])[None, :]
    return (qg, pad_r(jnp.concatenate(w1, axis=1)).astype(BF16), pad_r(jnp.concatenate(w2, axis=1)).astype(BF16),
            kv_norm[None, :], jnp.concatenate(wk, axis=1).astype(BF16), jnp.concatenate(wv, axis=1).astype(BF16))


def _compress_weights(pe, w_cmp):
    half = CMP_LEN // 2
    wk = w_cmp[0].reshape(CMP_LEN, NSA_HD, NSA_HD)
    wv = w_cmp[1].reshape(CMP_LEN, NSA_HD, NSA_HD)
    z = jnp.zeros((half, NSA_HD, NSA_HD), F32)

    def interleave(ks, vs):
        top = jnp.concatenate([ks, z], axis=-1)
        bot = jnp.concatenate([z, vs], axis=-1)
        return jnp.concatenate([top, bot], axis=1).reshape(half * 2 * NSA_HD, 2 * NSA_HD)

    pad8 = lambda p: jnp.concatenate([p.reshape(1, -1), jnp.zeros((7, CMP_LEN * NSA_HD), F32)], axis=0)
    return (interleave(wk[:half], wv[:half]), interleave(wk[half:], wv[half:]),
            pad8(pe[0]), pad8(pe[1]), w_cmp[0], w_cmp[1])


def _overlap_matrix(S):
    nc_pad = S // CMP_STRIDE
    n = np.arange(nc_pad)[:, None]
    j = np.arange(SEL_LANES)[None, :]
    ov = ((n * CMP_STRIDE <= j * SEL_LEN + SEL_LEN - 1) & (n * CMP_STRIDE + CMP_LEN - 1 >= j * SEL_LEN)
          & (n < nc_pad - 1) & (j < S // SEL_LEN))
    return jnp.asarray(ov.astype(np.float32)).astype(BF16)


def _gate_expand():
    gw = NSA_HEADS * NSA_HD
    ex = np.zeros((128, 3 * gw), np.float32)
    for br in range(3):
        for h in range(NSA_HEADS):
            ex[h * 3 + br, br * gw + h * NSA_HD:br * gw + (h + 1) * NSA_HD] = 1.0
    return jnp.asarray(ex).astype(BF16)


def _lru_gate_weights(w_gate, b_gate):
    wg = jnp.zeros((LRU_W, 2 * LRU_W), F32)
    bw = LRU_W // LRU_BLOCKS
    for g in range(2):
        for n in range(LRU_BLOCKS):
            wg = wg.at[n * bw:(n + 1) * bw, g * LRU_W + n * bw:g * LRU_W + (n + 1) * bw].set(w_gate[g, n])
    return wg.astype(BF16), b_gate.reshape(1, 2 * LRU_W)


def _pad_lanes(v, n=128):
    return jnp.concatenate([v, jnp.zeros((n - v.shape[0],), F32)])[None, :]


def kernel(x, positions, norm_mix, w_in, mla_q_norm, mla_w_uq, mla_kv_norm, mla_w_ukv, nsa_cmp_pe, nsa_w_cmp,
           lru_conv_w, lru_conv_b, lru_w_gate, lru_b_gate, lru_lambda, ssd_conv_w, ssd_conv_b, ssd_dt_bias,
           ssd_a_log, ssd_d, group_norm, w_out, norm_ffn, ffn_w_gate, ffn_w_up, ffn_w_down, moe_router,
           moe_w_gate, moe_w_up, moe_w_down, norm_final):
    B, S, D = x.shape
    T = B * S
    depth = w_in.shape[0]
    assert S // SEL_LEN <= SEL_LANES and S % 512 == 0
    rope_tab, cos_m, sin_m = _rope_tables(positions)
    ov = _overlap_matrix(S)
    ex = _gate_expand()
    h_res = x.reshape(T, D)
    for l in range(depth):
        w_raw = _aligned_w_in(w_in[l])
        ua, qn, kvc, ksel, kwin, vsel, vwin, gate, uc, ud = _inproj(
            h_res, norm_mix[l][None, :], w_raw, rope_tab, S)

        q_m, k_m, v_m = _mla_prep(ua, *_mla_weights(mla_w_uq[l], mla_w_ukv[l], mla_q_norm[l], mla_kv_norm[l]),
                                  cos_m, sin_m, B, S)
        y_a = _flash(q_m, k_m, v_m, window=None, tq=1024).reshape(T, GROUP_W)

        kvcmp = _compress(kvc.reshape(B, S // CMP_STRIDE, CMP_STRIDE * 128),
                          *_compress_weights(nsa_cmp_pe[l], nsa_w_cmp[l]))
        o_c, q_aug = _cmp_select(qn, kvcmp, ov, B, S)
        o_s = _flash(q_aug, ksel.reshape(B, 1, S, 256), vsel.reshape(B, 1, S, 128), window=None,
                     tq=512).reshape(T, GROUP_W)
        o_w = _flash(q_aug, kwin.reshape(B, 1, S, 256), vwin.reshape(B, 1, S, 128), window=WINDOW,
                     tq=512).reshape(T, GROUP_W)

        wg_l, bg_l = _lru_gate_weights(lru_w_gate[l], lru_b_gate[l])
        y_c = _lru(uc, lru_conv_w[l], lru_conv_b[l][None, :], wg_l, bg_l,
                   jax.nn.softplus(-lru_lambda[l])[None, :], B, S)

        y_d = _ssd(ud, ssd_conv_w[l], ssd_conv_b[l][None, :], _pad_lanes(ssd_dt_bias[l]),
                   _pad_lanes(-jnp.exp(ssd_a_log[l])), jnp.repeat(ssd_d[l], SSD_HD)[None, :],
                   group_norm[l, 3][None, :], B, S)

        moe_layer = l % 2 == 1
        rw = None
        if moe_layer:
            rw = jnp.concatenate([moe_router[l // 2], jnp.zeros((D, 128 - N_EXPERTS), F32)], axis=1)
            rw = jnp.stack(_split2(rw))
        outs = _outproj(y_a, o_c, o_s, o_w, gate, y_c, y_d, h_res, group_norm[l], ex, w_out[l].astype(BF16),
                        norm_ffn[l][None, :], rw)
        if moe_layer:
            h_res, hn, rg = outs
            normed = l == depth - 1
            h_res = _moe(hn, h_res, rg, moe_w_gate[l // 2], moe_w_up[l // 2], moe_w_down[l // 2],
                         norm_final[None, :], normed)
        else:
            normed = False
            h_res, hn = outs
            h_res = _ffn(hn, h_res, ffn_w_gate[l // 2].astype(BF16), ffn_w_up[l // 2].astype(BF16),
                         ffn_w_down[l // 2].astype(BF16))
    if not normed:
        h_res = _final_norm(h_res, norm_final[None, :])
    return h_res.reshape(B, S, D).astype(x.dtype)
```

```python
import functools

import numpy as np
import jax
import jax.numpy as jnp
from jax import lax
from jax.experimental import pallas as pl
from jax.experimental.pallas import tpu as pltpu

F32 = jnp.float32
BF16 = jnp.bfloat16

D_MODEL = 1024
GROUP_W = 256
ROPE_THETA = 500000.0
NORM_EPS = 1e-6
NEG = -1e30
FORCE = 1e4
MLA_HEADS, MLA_NOPE, MLA_ROPE, MLA_V = 4, 64, 32, 64
MLA_Q_RANK, MLA_KV_RANK = 192, 128
NSA_HEADS, NSA_HD, NSA_ROT = 4, 64, 16
CMP_STRIDE, CMP_LEN, SEL_LEN, SEL_TOPN, WINDOW = 16, 32, 64, 16, 512
LRU_W, LRU_BLOCKS, LRU_C, CONV_W = 256, 4, 8.0, 4
SSD_HEADS, SSD_HD, SSD_GROUPS, SSD_STATE, SSD_CHUNK = 4, 64, 2, 128, 128
SSD_DI = 256
SSD_XBC = SSD_DI + 2 * SSD_GROUPS * SSD_STATE
D_FF = 3584
N_EXPERTS = 8

O_NSA = MLA_Q_RANK + MLA_KV_RANK + MLA_ROPE
O_LRU = O_NSA + NSA_HEADS * NSA_HD + 6 * NSA_HD + 3 * NSA_HEADS
O_SSD = O_LRU + 2 * LRU_W
SSD_IN_W = 1152
P_NSA, P_LRU, W_RAW = 384, 1152, 2816
LOG2E = 1.4426950408889634
SEL_LANES = 128

VMEM_LIMIT = 56 * 1024 * 1024


def _cparams(sem):
    return pltpu.CompilerParams(dimension_semantics=sem, vmem_limit_bytes=VMEM_LIMIT)


def _dot(a, b):
    return jnp.dot(a.astype(BF16), b.astype(BF16), preferred_element_type=F32)


def _dot_nt(a, b):
    return lax.dot_general(a.astype(BF16), b.astype(BF16), (((1,), (1,)), ((), ())),
                           preferred_element_type=F32)


def _dot_f32(a, b):
    return jnp.dot(a, b, precision=lax.Precision.HIGHEST, preferred_element_type=F32)


def _dot_f32_nt(a, b):
    return lax.dot_general(a, b, (((1,), (1,)), ((), ())), precision=lax.Precision.HIGHEST,
                           preferred_element_type=F32)


def _split2(x):
    hi = x.astype(BF16)
    return hi, (x - hi.astype(F32)).astype(BF16)


def _sigmoid(x):
    return 1.0 / (1.0 + jnp.exp(-x))


def _softplus(x):
    return jnp.maximum(x, 0.0) + jnp.log(1.0 + jnp.exp(-jnp.abs(x)))


def _rms(x, width):
    return x * lax.rsqrt(jnp.sum(x * x, axis=-1, keepdims=True) * (1.0 / width) + NORM_EPS)


def _rope_tile(x, tab_ref, slot, half):
    t0 = 3 * 128 * slot
    return (x * tab_ref[:, t0:t0 + 128] + pltpu.roll(x, half, axis=1) * tab_ref[:, t0 + 128:t0 + 256]
            + pltpu.roll(x, 128 - half, axis=1) * tab_ref[:, t0 + 256:t0 + 384])


def _inproj_kernel(x_ref, g_ref, w_ref, tab_ref,
                   ua_ref, qn_ref, kvc_ref, ksel_ref, kwin_ref, vsel_ref, vwin_ref, gate_ref, uc_ref, ud_ref,
                   *, tm, seq_blocks):
    x = x_ref[...]
    h = _rms(x, D_MODEL) * g_ref[...]
    y = _dot(h, w_ref[...])
    lane = lax.broadcasted_iota(jnp.int32, (tm, 128), 1)
    lo64 = lane < 64
    ua_ref[:, 0:128] = y[:, 0:128]
    ua_ref[:, 128:256] = jnp.where(lo64, y[:, 128:256], 0.0)
    ua_ref[:, 256:384] = y[:, MLA_Q_RANK:MLA_Q_RANK + MLA_KV_RANK]
    kpe = jnp.where((lane >= 64) & (lane < 64 + MLA_ROPE), y[:, 256:384], 0.0)
    ua_ref[:, 384:512] = _rope_tile(kpe, tab_ref, 0, MLA_ROPE // 2)
    yn = y[:, P_NSA:P_NSA + 768]
    qn_ref[:, 0:128] = _rope_tile(yn[:, 0:128], tab_ref, 1, NSA_ROT // 2)
    qn_ref[:, 128:256] = _rope_tile(yn[:, 128:256], tab_ref, 1, NSA_ROT // 2)
    kvc_ref[...] = jnp.where(lo64, _rope_tile(yn[:, 256:384], tab_ref, 1, NSA_ROT // 2), yn[:, 256:384])
    s0 = (pl.program_id(0) % seq_blocks) * tm
    pos = s0 + lax.broadcasted_iota(jnp.int32, (tm, SEL_LANES), 0)
    onehot = jnp.where(jnp.right_shift(pos, 6) == lane, 1.0, 0.0)
    for k_ref, v_ref, c0, extra in ((ksel_ref, vsel_ref, 384, onehot), (kwin_ref, vwin_ref, 512, None)):
        kv = yn[:, c0:c0 + 128]
        k_ref[:, 0:128] = jnp.where(lo64, _rope_tile(kv, tab_ref, 1, NSA_ROT // 2), 0.0).astype(BF16)
        k_ref[:, 128:256] = (jnp.zeros((tm, 128), F32) if extra is None else extra).astype(BF16)
        v_ref[...] = jnp.where(lo64, pltpu.roll(kv, 64, axis=1), 1.0).astype(BF16)
    gate_ref[...] = jnp.where(lane < 3 * NSA_HEADS, yn[:, 640:768], 0.0)
    yl = y[:, P_LRU:P_LRU + 512 + SSD_IN_W]
    uc_ref[...] = yl[:, 0:512]
    ud_ref[:, 0:SSD_IN_W - 128] = yl[:, 512:512 + SSD_IN_W - 128]
    ud_ref[:, SSD_IN_W - 128:SSD_IN_W] = jnp.where(lane < SSD_HEADS, yl[:, 512 + SSD_IN_W - 128:512 + SSD_IN_W], 0.0)


def _inproj(x2d, g, w_raw, tab, seq):
    T = x2d.shape[0]
    tm = 512
    seq_blocks = seq // tm
    row = lambda w: pl.BlockSpec((tm, w), lambda i: (i, 0))
    full = lambda a: pl.BlockSpec(a.shape, lambda i: (0,) * a.ndim)
    outs = [(512, F32), (256, F32), (128, F32), (256, BF16), (256, BF16), (128, BF16), (128, BF16), (128, F32),
            (512, F32), (SSD_IN_W, F32)]
    return pl.pallas_call(
        functools.partial(_inproj_kernel, tm=tm, seq_blocks=seq_blocks),
        grid=(T // tm,),
        in_specs=[row(D_MODEL), full(g), full(w_raw), pl.BlockSpec((tm, tab.shape[1]), lambda i: (i % seq_blocks, 0))],
        out_specs=[row(w) for w, _ in outs],
        out_shape=[jax.ShapeDtypeStruct((T, w), dt) for w, dt in outs],
        compiler_params=_cparams(("parallel",)),
        name="inproj",
    )(x2d, g, w_raw, tab)


def _mla_prep_kernel(ua_ref, qg_ref, w1_ref, w2_ref, kvg_ref, wk_ref, wv_ref, tab_ref,
                     q_ref, k_ref, v_ref):
    ua = ua_ref[...]
    cq = _rms(ua[:, 0:256], MLA_Q_RANK) * qg_ref[...]
    y1 = _dot(cq, w1_ref[...])
    y2 = _dot(cq, w2_ref[...])
    ckv = _rms(ua[:, 256:384], MLA_KV_RANK) * kvg_ref[...]
    kn = _dot(ckv, wk_ref[...])
    vv = _dot(ckv, wv_ref[...])
    kpe = ua[:, 384:512]
    cos = tab_ref[:, 0:128]
    sin = tab_ref[:, 128:256] - tab_ref[:, 256:384]
    scale = LOG2E * (MLA_NOPE + MLA_ROPE) ** -0.5
    ones_hi = jnp.where(lax.broadcasted_iota(jnp.int32, cos.shape, 1) >= 64, 1.0, 0.0)
    for h in range(MLA_HEADS):
        sl = slice(h * 128, (h + 1) * 128)
        q_ref[h] = ((y1[:, sl] * cos + y2[:, sl] * sin) * scale).astype(BF16)
        k_ref[h] = (kn[:, sl] + kpe).astype(BF16)
        v_ref[h] = (vv[:, sl] + ones_hi).astype(BF16)


def _mla_prep(ua, qg, w1, w2, kvg, wk, wv, rope_tab, B, S):
    tm = 1024
    nb = S // tm
    full = lambda a: pl.BlockSpec(a.shape, lambda b, i: (0,) * a.ndim)
    tab = pl.BlockSpec((tm, 384), lambda b, i: (i, 0))
    hd = lambda w: pl.BlockSpec((None, MLA_HEADS, tm, w), lambda b, i: (b, 0, i, 0))
    return pl.pallas_call(
        _mla_prep_kernel,
        grid=(B, nb),
        in_specs=[pl.BlockSpec((tm, 512), lambda b, i: (b * nb + i, 0)),
                  full(qg), full(w1), full(w2), full(kvg), full(wk), full(wv), tab],
        out_specs=[hd(128), hd(128), hd(128)],
        out_shape=[jax.ShapeDtypeStruct((B, MLA_HEADS, S, 128), BF16)] * 3,
        compiler_params=_cparams(("parallel", "parallel")),
        name="mla_prep",
    )(ua, qg, w1, w2, kvg, wk, wv, rope_tab)


def _flash_kernel(qi_ref, ki_ref, flag_ref, q_ref, k_ref, v_ref, o_ref, m_ref, acc_ref,
                  *, heads, kv_heads, tq, tk, window):
    p_idx = pl.program_id(1)
    qi = qi_ref[p_idx]
    ki = ki_ref[p_idx]
    flags = flag_ref[p_idx]

    @pl.when(jnp.bitwise_and(flags, 1) != 0)
    def _init():
        m_ref[...] = jnp.full(m_ref.shape, NEG, F32)
        acc_ref[...] = jnp.zeros(acc_ref.shape, F32)

    def step(masked):
        if masked:
            qpos = qi * tq + lax.broadcasted_iota(jnp.int32, (tq, tk), 0)
            kpos = ki * tk + lax.broadcasted_iota(jnp.int32, (tq, tk), 1)
            mask = kpos <= qpos
            if window is not None:
                mask = mask & (kpos > qpos - window)
        nt = (((1,), (1,)), ((), ()))
        shared = kv_heads == 1
        if shared:
            dk = q_ref.shape[-1]
            s_all = lax.dot_general(q_ref[...].reshape(heads * tq, dk), k_ref[0], nt, preferred_element_type=F32)
        ps, alphas = [], []
        for h in range(heads):
            if shared:
                s = s_all[h * tq:(h + 1) * tq]
            else:
                s = lax.dot_general(q_ref[h], k_ref[h], nt, preferred_element_type=F32)
            if masked:
                s = jnp.where(mask, s, NEG)
            m_old = m_ref[h]
            m_new = jnp.maximum(m_old, jnp.max(s, axis=-1, keepdims=True))
            alpha = jnp.exp2(m_old - m_new)
            p = jnp.concatenate([jnp.exp2((s[:, c * 128:(c + 1) * 128] - m_new).astype(BF16))
                                 for c in range(tk // 128)], axis=1)
            m_ref[h] = m_new
            if shared:
                ps.append(p)
                alphas.append(alpha)
            else:
                acc_ref[h] = alpha * acc_ref[h] + jnp.dot(p, v_ref[h], preferred_element_type=F32)
        if shared:
            pv = jnp.dot(jnp.concatenate(ps, axis=0), v_ref[0], preferred_element_type=F32)
            for h in range(heads):
                acc_ref[h] = alphas[h] * acc_ref[h] + pv[h * tq:(h + 1) * tq]

    @pl.when(jnp.bitwise_and(flags, 4) != 0)
    def _masked():
        step(True)

    @pl.when(jnp.bitwise_and(flags, 4) == 0)
    def _plain():
        step(False)

    @pl.when(jnp.bitwise_and(flags, 2) != 0)
    def _fin():
        for h in range(heads):
            acc = acc_ref[h]
            o_ref[:, h * 64:(h + 1) * 64] = acc[:, 0:64] / acc[:, 64:128]


def _pair_tables(S, tq, tk, window):
    qi, ki, flags = [], [], []
    for i in range(S // tq):
        q_lo, q_hi = i * tq, i * tq + tq - 1
        lo = 0 if window is None else max(0, q_lo - window + 1) // tk
        hi = q_hi // tk
        for j in range(lo, hi + 1):
            k_lo, k_hi = j * tk, j * tk + tk - 1
            masked = k_hi > q_lo or (window is not None and k_lo <= q_hi - window)
            qi.append(i)
            ki.append(j)
            flags.append((1 if j == lo else 0) | (2 if j == hi else 0) | (4 if masked else 0))
    mk = lambda a: jnp.asarray(np.asarray(a, np.int32))
    return mk(qi), mk(ki), mk(flags)


def _flash(q, k, v, *, window, tq, tk=512):
    B, H, S, dk = q.shape
    Hk = k.shape[1]
    tq, tk = min(tq, S), min(tk, S)
    tabs = _pair_tables(S, tq, tk, window)
    npairs = int(tabs[0].shape[0])
    grid_spec = pltpu.PrefetchScalarGridSpec(
        num_scalar_prefetch=3,
        grid=(B, npairs),
        in_specs=[pl.BlockSpec((None, H, tq, dk), lambda b, p, qi, ki, fl: (b, 0, qi[p], 0)),
                  pl.BlockSpec((None, Hk, tk, dk), lambda b, p, qi, ki, fl: (b, 0, ki[p], 0)),
                  pl.BlockSpec((None, Hk, tk, 128), lambda b, p, qi, ki, fl: (b, 0, ki[p], 0))],
        out_specs=pl.BlockSpec((None, tq, H * 64), lambda b, p, qi, ki, fl: (b, qi[p], 0)),
        scratch_shapes=[pltpu.VMEM((H, tq, 128), F32), pltpu.VMEM((H, tq, 128), F32)],
    )
    return pl.pallas_call(
        functools.partial(_flash_kernel, heads=H, kv_heads=Hk, tq=tq, tk=tk, window=window),
        grid_spec=grid_spec,
        out_shape=jax.ShapeDtypeStruct((B, S, H * 64), F32),
        compiler_params=_cparams(("parallel", "arbitrary")),
        name="flash_attn",
    )(*tabs, q, k, v)


def _compress_kernel(x_ref, wa_ref, wb_ref, pek_ref, pev_ref, wkf_ref, wvf_ref, o_ref, *, nc_pad):
    x = x_ref[...]
    a = _dot_f32(x, wa_ref[...])
    b = _dot_f32(x, wb_ref[...])
    ck = _dot_f32(pek_ref[...], wkf_ref[...])[0:1]
    cv = _dot_f32(pev_ref[...], wvf_ref[...])[0:1]
    const = jnp.concatenate([ck, cv], axis=-1)
    b_next = pltpu.roll(b, nc_pad - 1, axis=0)
    row = lax.broadcasted_iota(jnp.int32, (nc_pad, 128), 0)
    o_ref[...] = jnp.where(row < nc_pad - 1, a + b_next + const, 0.0)


def _compress(x, wa, wb, pek, pev, wkf, wvf):
    B, nc_pad, w = x.shape
    full = lambda a: pl.BlockSpec(a.shape, lambda b: (0,) * a.ndim)
    return pl.pallas_call(
        functools.partial(_compress_kernel, nc_pad=nc_pad),
        grid=(B,),
        in_specs=[pl.BlockSpec((None, nc_pad, w), lambda b: (b, 0, 0)),
                  full(wa), full(wb), full(pek), full(pev), full(wkf), full(wvf)],
        out_specs=pl.BlockSpec((None, nc_pad, 128), lambda b: (b, 0, 0)),
        out_shape=jax.ShapeDtypeStruct((B, nc_pad, 128), F32),
        compiler_params=_cparams(("parallel",)),
        name="nsa_compress",
    )(x, wa, wb, pek, pev, wkf, wvf)


def _cmp_select_kernel(q_ref, kvc_ref, ov_ref, oc_ref, qaug_ref, *, tq, nc_pad, n_top):
    i = pl.program_id(1)
    q = q_ref[...]
    kc = kvc_ref[:, 0:64]
    vc = kvc_ref[:, 64:128]
    scale = NSA_HD ** -0.5
    qpos = i * tq + lax.broadcasted_iota(jnp.int32, (tq, 1), 0)
    n_idx = lax.broadcasted_iota(jnp.int32, (1, nc_pad), 1)
    m_c = (n_idx * CMP_STRIDE + (CMP_LEN - 1) <= qpos) & (n_idx < nc_pad - 1)
    kc_hi, kc_lo = _split2(kc)
    kc3 = jnp.concatenate([kc_hi, kc_lo, kc_hi], axis=1)
    psum = jnp.zeros((tq, nc_pad), F32)
    for h in range(NSA_HEADS):
        q_hi, q_lo = _split2(q[:, h * NSA_HD:(h + 1) * NSA_HD])
        q3 = jnp.concatenate([q_hi, q_hi, q_lo], axis=1)
        s = lax.dot_general(q3, kc3, (((1,), (1,)), ((), ())), preferred_element_type=F32)
        s = jnp.where(m_c, s * scale, NEG)
        e = jnp.where(m_c, jnp.exp(s - jnp.max(s, axis=-1, keepdims=True)), 0.0)
        den = jnp.sum(e, axis=-1, keepdims=True)
        p = e / jnp.where(den > 0.0, den, 1.0)
        oc_ref[:, h * NSA_HD:(h + 1) * NSA_HD] = _dot(p, vc)
        psum = psum + p
    ov = ov_ref[...]
    p_hi = psum.astype(BF16)
    p_r = psum - p_hi.astype(F32)
    p_mid = p_r.astype(BF16)
    p_lo = (p_r - p_mid.astype(F32)).astype(BF16)
    imp = (jnp.dot(p_hi, ov, preferred_element_type=F32) + jnp.dot(p_mid, ov, preferred_element_type=F32)
           + jnp.dot(p_lo, ov, preferred_element_type=F32))
    cur = jnp.right_shift(qpos, 6)
    jj = lax.broadcasted_iota(jnp.int32, (1, SEL_LANES), 1)
    forced = (jj == 0) | (jj == cur) | (jj == cur - 1)
    bias = jnp.where(forced, 0.0, NEG)
    imp = jnp.where((jj <= cur) & jnp.logical_not(forced), imp, NEG)
    jf = jj.astype(F32)
    for _ in range(n_top - 3):
        mx = jnp.max(imp, axis=-1, keepdims=True)
        idx = jnp.min(jnp.where(imp == mx, jf, float(SEL_LANES)), axis=-1, keepdims=True)
        hit = jf == idx
        bias = jnp.where(hit & (mx > 0.5 * NEG), 0.0, bias)
        imp = jnp.where(hit, -3e38, imp)
    bias = bias.astype(BF16)
    qs = (q * (scale * LOG2E)).astype(BF16)
    for h in range(NSA_HEADS):
        qaug_ref[h, :, 0:64] = qs[:, h * NSA_HD:(h + 1) * NSA_HD]
        qaug_ref[h, :, 64:128] = jnp.zeros((tq, 64), BF16)
        qaug_ref[h, :, 128:256] = bias


def _cmp_select(qn, kvcmp, ov, B, S):
    tq = min(1024, S)
    nb = S // tq
    nc_pad = kvcmp.shape[1]
    n_top = min(SEL_TOPN, S // SEL_LEN)
    assert n_top >= 3
    return pl.pallas_call(
        functools.partial(_cmp_select_kernel, tq=tq, nc_pad=nc_pad, n_top=n_top),
        grid=(B, nb),
        in_specs=[pl.BlockSpec((tq, 256), lambda b, i: (b * nb + i, 0)),
                  pl.BlockSpec((None, nc_pad, 128), lambda b, i: (b, 0, 0)),
                  pl.BlockSpec(ov.shape, lambda b, i: (0, 0))],
        out_specs=[pl.BlockSpec((tq, 256), lambda b, i: (b * nb + i, 0)),
                   pl.BlockSpec((None, NSA_HEADS, tq, 256), lambda b, i: (b, 0, i, 0))],
        out_shape=[jax.ShapeDtypeStruct((B * S, 256), F32),
                   jax.ShapeDtypeStruct((B, NSA_HEADS, S, 256), BF16)],
        compiler_params=_cparams(("parallel", "parallel")),
        name="nsa_cmp_select",
    )(qn, kvcmp, ov)


def _shift_scan(a, b, t, width):
    row = lax.broadcasted_iota(jnp.int32, (t, width), 0)
    s = 1
    while s < t:
        keep = row >= s
        a_sh = jnp.where(keep, pltpu.roll(a, s, axis=0), 1.0)
        b_sh = jnp.where(keep, pltpu.roll(b, s, axis=0), 0.0)
        b = a * b_sh + b
        a = a * a_sh
        s *= 2
    return a, b


def _causal_conv(xpad_ref, x, cw_ref, cb_ref, t):
    xpad_ref[8:8 + t, :] = x
    y = cb_ref[...] + cw_ref[CONV_W - 1:CONV_W, :] * x
    for k in range(CONV_W - 1):
        off = 8 - (CONV_W - 1) + k
        y = y + cw_ref[k:k + 1, :] * xpad_ref[off:off + t, :]
    xpad_ref[0:8, :] = x[t - 8:t, :]
    return y


def _lru_kernel(u_ref, cw_ref, cb_ref, wg_ref, bg_ref, sp_ref, y_ref, xpad_ref, h_ref, *, t):
    @pl.when(pl.program_id(1) == 0)
    def _init():
        xpad_ref[0:8, :] = jnp.zeros((8, LRU_W), F32)
        h_ref[...] = jnp.zeros(h_ref.shape, F32)

    x = u_ref[:, 0:LRU_W]
    gbr = u_ref[:, LRU_W:2 * LRU_W]
    xb = _causal_conv(xpad_ref, x, cw_ref, cb_ref, t)
    g = _dot(xb, wg_ref[...]) + bg_ref[...]
    r = _sigmoid(g[:, 0:LRU_W])
    ig = _sigmoid(g[:, LRU_W:2 * LRU_W])
    log_a = -LRU_C * r * sp_ref[...]
    a = jnp.exp(log_a)
    one_m = -jnp.tanh(log_a) * (a * a + 1.0)
    b = jnp.sqrt(jnp.maximum(one_m, 0.0)) * (ig * xb)
    a_cum, h = _shift_scan(a, b, t, LRU_W)
    h = h + a_cum * h_ref[0:1, :]
    h_ref[0:1, :] = h[t - 1:t, :]
    gelu = 0.5 * gbr * (1.0 + jnp.tanh(0.7978845608028654 * (gbr + 0.044715 * gbr * gbr * gbr)))
    y_ref[...] = h * gelu


def _lru(uc, cw, cb, wg, bg, sp, B, S):
    t = 512
    nb = S // t
    full = lambda a: pl.BlockSpec(a.shape, lambda b, i: (0,) * a.ndim)
    return pl.pallas_call(
        functools.partial(_lru_kernel, t=t),
        grid=(B, nb),
        in_specs=[pl.BlockSpec((t, 2 * LRU_W), lambda b, i: (b * nb + i, 0)),
                  full(cw), full(cb), full(wg), full(bg), full(sp)],
        out_specs=pl.BlockSpec((t, LRU_W), lambda b, i: (b * nb + i, 0)),
        out_shape=jax.ShapeDtypeStruct((B * S, LRU_W), F32),
        scratch_shapes=[pltpu.VMEM((t + 8, LRU_W), F32), pltpu.VMEM((8, LRU_W), F32)],
        compiler_params=_cparams(("parallel", "arbitrary")),
        name="rglru",
    )(uc, cw, cb, wg, bg, sp)


def _ssd_kernel(u_ref, cw_ref, cb_ref, dtb_ref, a_ref, d_ref, ng_ref, y_ref, xpad_ref, st_ref, *, t):
    @pl.when(pl.program_id(1) == 0)
    def _init():
        xpad_ref[0:8, :] = jnp.zeros((8, SSD_XBC), F32)
        st_ref[...] = jnp.zeros(st_ref.shape, F32)

    L = SSD_CHUNK
    z = u_ref[:, 0:SSD_DI]
    conv = _causal_conv(xpad_ref, u_ref[:, SSD_DI:SSD_DI + SSD_XBC], cw_ref, cb_ref, t)
    xbc = conv * _sigmoid(conv)
    dt = _softplus(u_ref[:, SSD_DI + SSD_XBC:SSD_IN_W] + dtb_ref[...])
    a = dt * a_ref[...]
    row = lax.broadcasted_iota(jnp.int32, (t, 128), 0)
    rin = jnp.bitwise_and(row, L - 1)
    cs = a
    s = 1
    while s < L:
        cs = cs + jnp.where(rin >= s, pltpu.roll(cs, s, axis=0), 0.0)
        s *= 2
    tril = lax.broadcasted_iota(jnp.int32, (L, L), 0) >= lax.broadcasted_iota(jnp.int32, (L, L), 1)
    gn = SSD_GROUPS * SSD_STATE
    rep = SSD_HEADS // SSD_GROUPS
    for c in range(t // L):
        rs = slice(c * L, (c + 1) * L)
        cs_c = cs[rs]
        cs_t = cs_c.T
        cs_last = cs_c[L - 1:L, :]
        ys = []
        for g in range(SSD_GROUPS):
            bg = xbc[rs, SSD_DI + g * SSD_STATE:SSD_DI + (g + 1) * SSD_STATE]
            cg = xbc[rs, SSD_DI + gn + g * SSD_STATE:SSD_DI + gn + (g + 1) * SSD_STATE]
            gmat = _dot_nt(cg, bg)
            bg_t = bg.T
            for hh in range(rep):
                h = g * rep + hh
                cs_col = cs_c[:, h:h + 1]
                lm = jnp.exp(jnp.where(tril, cs_col - cs_t[h:h + 1, :], NEG))
                xh = xbc[rs, h * SSD_HD:(h + 1) * SSD_HD]
                xdt = xh * dt[rs, h:h + 1]
                st_old = st_ref[h]
                y_h = _dot(gmat * lm, xdt) + _dot(cg, st_old) * jnp.exp(cs_col)
                dec = jnp.exp(cs_last[:, h:h + 1] - cs_col)
                st_ref[h] = jnp.exp(cs_last[:, h:h + 1]) * st_old + _dot(bg_t, xdt * dec)
                ys.append(y_h)
        y = jnp.concatenate(ys, axis=-1) + xbc[rs, 0:SSD_DI] * d_ref[...]
        zc = z[rs]
        y = y * (zc * _sigmoid(zc))
        y_ref[rs, :] = _rms(y, SSD_DI) * ng_ref[...]


def _ssd(ud, cw, cb, dtb, a_neg, d_vec, ng, B, S):
    t = 512
    nb = S // t
    full = lambda a: pl.BlockSpec(a.shape, lambda b, i: (0,) * a.ndim)
    return pl.pallas_call(
        functools.partial(_ssd_kernel, t=t),
        grid=(B, nb),
        in_specs=[pl.BlockSpec((t, SSD_IN_W), lambda b, i: (b * nb + i, 0)),
                  full(cw), full(cb), full(dtb), full(a_neg), full(d_vec), full(ng)],
        out_specs=pl.BlockSpec((t, SSD_DI), lambda b, i: (b * nb + i, 0)),
        out_shape=jax.ShapeDtypeStruct((B * S, SSD_DI), F32),
        scratch_shapes=[pltpu.VMEM((t + 8, SSD_XBC), F32), pltpu.VMEM((SSD_HEADS, SSD_STATE, SSD_HD), F32)],
        compiler_params=_cparams(("parallel", "arbitrary")),
        name="ssd",
    )(ud, cw, cb, dtb, a_neg, d_vec, ng)


def _outproj_kernel(*refs, with_router):
    if with_router:
        (ya_ref, oc_ref, os_ref, ow_ref, gate_ref, yc_ref, yd_ref, res_ref, gn_ref, ex_ref, w_ref, nf_ref,
         rw_ref, hres_ref, hn_ref, rg_ref) = refs
    else:
        (ya_ref, oc_ref, os_ref, ow_ref, gate_ref, yc_ref, yd_ref, res_ref, gn_ref, ex_ref, w_ref, nf_ref,
         hres_ref, hn_ref) = refs
    sg_hi, sg_lo = _split2(_sigmoid(gate_ref[...]))
    ex = ex_ref[...]
    gx = jnp.dot(sg_hi, ex, preferred_element_type=F32) + jnp.dot(sg_lo, ex, preferred_element_type=F32)
    yb = gx[:, 0:256] * oc_ref[...] + gx[:, 256:512] * os_ref[...] + gx[:, 512:768] * ow_ref[...]
    y = jnp.concatenate([_rms(ya_ref[...], GROUP_W) * gn_ref[0:1, :],
                         _rms(yb, GROUP_W) * gn_ref[1:2, :],
                         _rms(yc_ref[...], GROUP_W) * gn_ref[2:3, :],
                         yd_ref[...]], axis=-1)
    hres = res_ref[...] + _dot(y, w_ref[...])
    hres_ref[...] = hres
    hn = _rms(hres, D_MODEL) * nf_ref[...]
    hn_ref[...] = hn.astype(hn_ref.dtype)
    if with_router:
        h_hi, h_lo = _split2(hn)
        logits = (jnp.dot(h_hi, rw_ref[0], preferred_element_type=F32)
                  + jnp.dot(h_hi, rw_ref[1], preferred_element_type=F32)
                  + jnp.dot(h_lo, rw_ref[0], preferred_element_type=F32))
        lane = lax.broadcasted_iota(jnp.int32, logits.shape, 1)
        lf = lane.astype(F32)
        logits = jnp.where(lane < N_EXPERTS, logits, NEG)
        m1 = jnp.max(logits, axis=-1, keepdims=True)
        i1 = jnp.min(jnp.where(logits == m1, lf, 128.0), axis=-1, keepdims=True)
        rest = jnp.where(lf == i1, NEG, logits)
        m2 = jnp.max(rest, axis=-1, keepdims=True)
        i2 = jnp.min(jnp.where(rest == m2, lf, 128.0), axis=-1, keepdims=True)
        e2 = jnp.exp(m2 - m1)
        den = 1.0 + e2
        rg_ref[...] = jnp.where(lane == 0, i1, jnp.where(lane == 1, i2, jnp.where(
            lane == 2, 1.0 / den, jnp.where(lane == 3, e2 / den, 0.0))))


def _outproj(ya, oc, osel, ow, gate, yc, yd, res, gn, ex, w, nf, rw):
    T = ya.shape[0]
    tm = 512
    with_router = rw is not None
    row = lambda wd: pl.BlockSpec((tm, wd), lambda i: (i, 0))
    full = lambda a: pl.BlockSpec(a.shape, lambda i: (0,) * a.ndim)
    ins = [ya, oc, osel, ow, gate, yc, yd, res, gn, ex, w, nf]
    in_specs = [row(256), row(256), row(256), row(256), row(128), row(256), row(256), row(D_MODEL),
                full(gn), full(ex), full(w), full(nf)]
    out_specs = [row(D_MODEL), row(D_MODEL)]
    out_shape = [jax.ShapeDtypeStruct((T, D_MODEL), F32),
                 jax.ShapeDtypeStruct((T, D_MODEL), F32 if with_router else BF16)]
    if with_router:
        ins.append(rw)
        in_specs.append(full(rw))
        out_specs.append(row(128))
        out_shape.append(jax.ShapeDtypeStruct((T, 128), F32))
    return pl.pallas_call(
        functools.partial(_outproj_kernel, with_router=with_router),
        grid=(T // tm,),
        in_specs=in_specs,
        out_specs=out_specs,
        out_shape=out_shape,
        compiler_params=_cparams(("parallel",)),
        name="outproj",
    )(*ins)


def _ffn_kernel(h_ref, res_ref, wg_ref, wu_ref, wd_ref, o_ref, acc_ref):
    j = pl.program_id(1)

    @pl.when(j == 0)
    def _init():
        acc_ref[...] = jnp.zeros(acc_ref.shape, F32)

    h = h_ref[...]
    g = jnp.dot(h, wg_ref[...], preferred_element_type=F32)
    u = jnp.dot(h, wu_ref[...], preferred_element_type=F32)
    acc_ref[...] += _dot(g * _sigmoid(g) * u, wd_ref[...])

    @pl.when(j == pl.num_programs(1) - 1)
    def _fin():
        o_ref[...] = res_ref[...] + acc_ref[...]


def _ffn(hn, res, wg, wu, wd):
    T = hn.shape[0]
    tm, tf = 1024, 512
    return pl.pallas_call(
        _ffn_kernel,
        grid=(T // tm, D_FF // tf),
        in_specs=[pl.BlockSpec((tm, D_MODEL), lambda i, j: (i, 0)),
                  pl.BlockSpec((tm, D_MODEL), lambda i, j: (i, 0)),
                  pl.BlockSpec((D_MODEL, tf), lambda i, j: (0, j)),
                  pl.BlockSpec((D_MODEL, tf), lambda i, j: (0, j)),
                  pl.BlockSpec((tf, D_MODEL), lambda i, j: (j, 0))],
        out_specs=pl.BlockSpec((tm, D_MODEL), lambda i, j: (i, 0)),
        out_shape=jax.ShapeDtypeStruct((T, D_MODEL), F32),
        scratch_shapes=[pltpu.VMEM((tm, D_MODEL), F32)],
        compiler_params=_cparams(("parallel", "arbitrary")),
        name="ffn_dense",
    )(hn, res, wg, wu, wd)


MOE_TILE = 1024
MOE_TM = 512


def _route_rank_kernel(route_ref, rank_ref, cnt_ref, carry_ref, *, tm):
    @pl.when(pl.program_id(0) == 0)
    def _init():
        carry_ref[...] = jnp.zeros(carry_ref.shape, F32)

    r = route_ref[...]
    i1 = r[:, 0:1]
    i2 = r[:, 1:2]
    lane = lax.broadcasted_iota(jnp.int32, (tm, 128), 1)
    lf = lane.astype(F32)
    oh = jnp.where((lf == i1) | (lf == i2), 1.0, 0.0)
    row = lax.broadcasted_iota(jnp.int32, (tm, 128), 0)
    cs = oh
    s = 1
    while s < tm:
        cs = cs + jnp.where(row >= s, pltpu.roll(cs, s, axis=0), 0.0)
        s *= 2
    excl = cs - oh + carry_ref[0:1, :]
    rank1 = jnp.sum(jnp.where(lf == i1, excl, 0.0), axis=-1, keepdims=True)
    rank2 = jnp.sum(jnp.where(lf == i2, excl, 0.0), axis=-1, keepdims=True)
    rank_ref[...] = jnp.where(lane == 0, rank1, jnp.where(lane == 1, rank2, 0.0))
    carry_ref[0:1, :] = carry_ref[0:1, :] + cs[tm - 1:tm, :]
    cnt_ref[...] = carry_ref[...]


def _route_rank(route):
    T = route.shape[0]
    tm = MOE_TM
    return pl.pallas_call(
        functools.partial(_route_rank_kernel, tm=tm),
        grid=(T // tm,),
        in_specs=[pl.BlockSpec((tm, 128), lambda i: (i, 0))],
        out_specs=[pl.BlockSpec((tm, 128), lambda i: (i, 0)), pl.BlockSpec((8, 128), lambda i: (0, 0))],
        out_shape=[jax.ShapeDtypeStruct((T, 128), F32), jax.ShapeDtypeStruct((8, 128), F32)],
        scratch_shapes=[pltpu.VMEM((8, 128), F32)],
        compiler_params=_cparams(("arbitrary",)),
        name="moe_rank",
    )(route)


def _row_copies(idx_ref, tm, make_copy):
    def body(r, carry):
        for k in range(2):
            make_copy(k, r, idx_ref[k * tm + r]).start(priority=k)
        return carry

    lax.fori_loop(0, tm, body, 0, unroll=8)


def _dispatch_kernel(ends_ref, slots_ref, h_ref, xs_ref, idx_ref, zero_ref, stage_ref, sem_idx, sem_row, *, tm):
    i = pl.program_id(0)

    @pl.when(i == 0)
    def _zero_tiles():
        zero_ref[...] = jnp.zeros(zero_ref.shape, F32)
        zr = zero_ref.shape[0]
        jobs = []
        for e in range(N_EXPERTS):
            start = ends_ref[e - 1] if e else 0
            jobs.append((ends_ref[e] > start, ends_ref[e] - MOE_TILE))
        for tile in range(N_EXPERTS):
            base = xs_ref.shape[0] - (tile + 1) * MOE_TILE
            jobs.append((base >= ends_ref[N_EXPERTS - 1], base))

        def tile_copies(base):
            rows = [base + q * zr for q in range(MOE_TILE // zr)]
            rows = [r if isinstance(r, int) else pl.multiple_of(r, zr) for r in rows]
            return [pltpu.make_async_copy(zero_ref, xs_ref.at[pl.ds(r, zr)], sem_idx) for r in rows]

        for cond, base in jobs:
            @pl.when(cond)
            def _start(base=base):
                for cp in tile_copies(base):
                    cp.start()
        for cond, base in jobs:
            @pl.when(cond)
            def _wait(base=base):
                for cp in tile_copies(base):
                    cp.wait()

    cp = pltpu.make_async_copy(slots_ref.at[pl.ds(i * 2 * tm, 2 * tm)], idx_ref, sem_idx)
    cp.start()
    cp.wait()
    cur = lax.rem(i, 2)

    def wait_tile(b):
        for _ in range(2):
            pltpu.make_async_copy(stage_ref.at[b], xs_ref.at[pl.ds(0, tm)], sem_row.at[b]).wait()

    @pl.when(i > 1)
    def _reuse():
        wait_tile(cur)

    stage_ref[cur] = h_ref[...]
    _row_copies(idx_ref, tm, lambda k, r, s: pltpu.make_async_copy(
        stage_ref.at[cur, pl.ds(r, 1)], xs_ref.at[pl.ds(s, 1)], sem_row.at[cur]))

    @pl.when(i == pl.num_programs(0) - 1)
    def _drain():
        @pl.when(i > 0)
        def _prev():
            wait_tile(1 - cur)
        wait_tile(cur)


def _dispatch(ends, slots, hn, n_slots):
    T = hn.shape[0]
    tm = MOE_TM
    grid_spec = pltpu.PrefetchScalarGridSpec(
        num_scalar_prefetch=1,
        grid=(T // tm,),
        in_specs=[pl.BlockSpec(memory_space=pl.ANY),
                  pl.BlockSpec((tm, D_MODEL), lambda i, ends: (i, 0))],
        out_specs=pl.BlockSpec(memory_space=pl.ANY),
        scratch_shapes=[pltpu.SMEM((2 * tm,), jnp.int32), pltpu.VMEM((256, D_MODEL), F32),
                        pltpu.VMEM((2, tm, D_MODEL), F32),
                        pltpu.SemaphoreType.DMA, pltpu.SemaphoreType.DMA((2,))],
    )
    return pl.pallas_call(
        functools.partial(_dispatch_kernel, tm=tm),
        grid_spec=grid_spec,
        out_shape=jax.ShapeDtypeStruct((n_slots, D_MODEL), F32),
        compiler_params=_cparams(("arbitrary",)),
        name="moe_dispatch",
    )(ends, slots, hn)


def _expert_ffn_kernel(te_ref, nu_ref, x_ref, wg_ref, wu_ref, wd_ref, o_ref, acc_ref):
    del te_ref
    i = pl.program_id(0)
    j = pl.program_id(1)
    last = pl.num_programs(1) - 1
    used = i < nu_ref[0]

    @pl.when(used & (j == 0))
    def _init():
        acc_ref[...] = jnp.zeros(acc_ref.shape, F32)

    @pl.when(used)
    def _step():
        x = x_ref[...]
        g = _dot(x, wg_ref[...])
        u = _dot(x, wu_ref[...])
        acc_ref[...] += _dot(g * _sigmoid(g) * u, wd_ref[...])

    @pl.when(used & (j == last))
    def _fin():
        o_ref[...] = acc_ref[...]

    @pl.when(jnp.logical_not(used) & (j == last))
    def _unused():
        o_ref[...] = jnp.zeros(o_ref.shape, F32)


def _expert_ffn(tile_expert, n_used, xs, wg, wu, wd):
    n_slots = xs.shape[0]
    tm, tf = MOE_TILE, 512
    jw = lambda i, j, nu: jnp.where(i < nu[0], j, 0)
    grid_spec = pltpu.PrefetchScalarGridSpec(
        num_scalar_prefetch=2,
        grid=(n_slots // tm, D_FF // tf),
        in_specs=[pl.BlockSpec((tm, D_MODEL), lambda i, j, te, nu: (jnp.minimum(i, nu[0] - 1), 0)),
                  pl.BlockSpec((None, D_MODEL, tf), lambda i, j, te, nu: (te[i], 0, jw(i, j, nu))),
                  pl.BlockSpec((None, D_MODEL, tf), lambda i, j, te, nu: (te[i], 0, jw(i, j, nu))),
                  pl.BlockSpec((None, tf, D_MODEL), lambda i, j, te, nu: (te[i], jw(i, j, nu), 0))],
        out_specs=pl.BlockSpec((tm, D_MODEL), lambda i, j, te, nu: (i, 0)),
        scratch_shapes=[pltpu.VMEM((tm, D_MODEL), F32)],
    )
    return pl.pallas_call(
        _expert_ffn_kernel,
        grid_spec=grid_spec,
        out_shape=jax.ShapeDtypeStruct((n_slots, D_MODEL), F32),
        compiler_params=_cparams(("parallel", "arbitrary")),
        name="moe_expert_ffn",
    )(tile_expert, n_used, xs, wg, wu, wd)


def _combine_kernel(slots_ref, ys_ref, route_ref, res_ref, ng_ref, o_ref, idx_ref, buf_ref, sem_idx, sem_row,
                    *, tm, final_norm):
    i = pl.program_id(0)
    cur = lax.rem(i, 2)

    def gather(step, b):
        cp = pltpu.make_async_copy(slots_ref.at[pl.ds(step * 2 * tm, 2 * tm)], idx_ref, sem_idx)
        cp.start()
        cp.wait()
        _row_copies(idx_ref, tm, lambda k, r, s: pltpu.make_async_copy(
            ys_ref.at[pl.ds(s, 1)], buf_ref.at[b, k, pl.ds(r, 1)], sem_row.at[b]))

    @pl.when(i == 0)
    def _first():
        gather(0, 0)

    @pl.when(i + 1 < pl.num_programs(0))
    def _next():
        gather(i + 1, 1 - cur)

    for k in range(2):
        pltpu.make_async_copy(ys_ref.at[pl.ds(0, tm)], buf_ref.at[cur, k], sem_row.at[cur]).wait()
    route = route_ref[...]
    out = res_ref[...] + route[:, 2:3] * buf_ref[cur, 0] + route[:, 3:4] * buf_ref[cur, 1]
    if final_norm:
        out = _rms(out, D_MODEL) * ng_ref[...]
    o_ref[...] = out


def _combine(slots, ys, route, res, norm_g, final_norm):
    T = res.shape[0]
    tm = MOE_TM
    return pl.pallas_call(
        functools.partial(_combine_kernel, tm=tm, final_norm=final_norm),
        grid=(T // tm,),
        in_specs=[pl.BlockSpec(memory_space=pl.ANY),
                  pl.BlockSpec(memory_space=pl.ANY),
                  pl.BlockSpec((tm, 128), lambda i: (i, 0)),
                  pl.BlockSpec((tm, D_MODEL), lambda i: (i, 0)),
                  pl.BlockSpec((1, D_MODEL), lambda i: (0, 0))],
        out_specs=pl.BlockSpec((tm, D_MODEL), lambda i: (i, 0)),
        out_shape=jax.ShapeDtypeStruct((T, D_MODEL), F32),
        scratch_shapes=[pltpu.SMEM((2 * tm,), jnp.int32), pltpu.VMEM((2, 2, tm, D_MODEL), F32),
                        pltpu.SemaphoreType.DMA, pltpu.SemaphoreType.DMA((2,))],
        compiler_params=_cparams(("arbitrary",)),
        name="moe_combine",
    )(slots, ys, route, res, norm_g)


def _moe(hn, res, route, wg, wu, wd, norm_g, final_norm):
    T = hn.shape[0]
    n_slots = 2 * T + N_EXPERTS * MOE_TILE
    n_tiles = n_slots // MOE_TILE
    rank, cnt = _route_rank(route)
    counts = cnt[0, :N_EXPERTS].astype(jnp.int32)
    padded = (counts + MOE_TILE - 1) // MOE_TILE * MOE_TILE
    ends = jnp.cumsum(padded)
    starts = ends - padded
    ids = route[:, 0:2].astype(jnp.int32)
    slot = jnp.take(starts, ids) + rank[:, 0:2].astype(jnp.int32)
    slots = slot.reshape(T // MOE_TM, MOE_TM, 2).transpose(0, 2, 1).reshape(-1)
    tile_start = jnp.arange(n_tiles, dtype=jnp.int32) * MOE_TILE
    tile_expert = jnp.minimum(jnp.sum(tile_start[:, None] >= ends[None, :], axis=1), N_EXPERTS - 1).astype(jnp.int32)
    n_used = (ends[-1:] // MOE_TILE).astype(jnp.int32)
    xs = _dispatch(ends.astype(jnp.int32), slots, hn, n_slots)
    ys = _expert_ffn(tile_expert, n_used, xs, wg, wu, wd)
    return _combine(slots, ys, route, res, norm_g, final_norm)


def _final_norm_kernel(x_ref, g_ref, o_ref):
    o_ref[...] = _rms(x_ref[...], D_MODEL) * g_ref[...]


def _final_norm(x2d, g):
    T = x2d.shape[0]
    tm = 1024
    return pl.pallas_call(
        _final_norm_kernel,
        grid=(T // tm,),
        in_specs=[pl.BlockSpec((tm, D_MODEL), lambda i: (i, 0)), pl.BlockSpec(g.shape, lambda i: (0, 0))],
        out_specs=pl.BlockSpec((tm, D_MODEL), lambda i: (i, 0)),
        out_shape=jax.ShapeDtypeStruct((T, D_MODEL), F32),
        compiler_params=_cparams(("parallel",)),
        name="final_norm",
    )(x2d, g)


def _rot_cols(w, half):
    return jnp.concatenate([-w[:, half:2 * half], w[:, 0:half]], axis=1)


def _aligned_w_in(w):
    z = lambda n: jnp.zeros((w.shape[0], n), F32)
    return jnp.concatenate([w[:, 0:O_NSA], z(P_NSA - O_NSA),
                            w[:, O_NSA:O_LRU], z(P_LRU - P_NSA - (O_LRU - O_NSA)),
                            w[:, O_LRU:], z(W_RAW - P_LRU - (w.shape[1] - O_LRU))], axis=1).astype(BF16)


def _rope_tables(positions):
    pos = positions.astype(F32)[:, None]
    S = positions.shape[0]

    def cs(rot_dim):
        inv = ROPE_THETA ** (-jnp.arange(0, rot_dim, 2, dtype=F32) / rot_dim)
        ang = pos * inv[None, :]
        return jnp.cos(ang), jnp.sin(ang)

    cm, sm = cs(MLA_ROPE)
    cn, sn = cs(NSA_ROT)
    hm, hn = MLA_ROPE // 2, NSA_ROT // 2
    base = jnp.concatenate([cm, sm, cn, sn, jnp.ones((S, 1), F32)], axis=1)
    c_cm, c_sm, c_cn, c_sn, c_one = 0, hm, 2 * hm, 2 * hm + hn, 2 * hm + 2 * hn

    place = np.zeros((base.shape[1], 6 * 128), np.float32)

    def put(tile, lane0, col0, n, sign=1.0):
        for j in range(n):
            place[col0 + j, tile * 128 + lane0 + j] = sign

    def ones(tile, lane0, n):
        place[c_one, tile * 128 + lane0:tile * 128 + lane0 + n] = 1.0

    ones(0, 0, 64), put(0, 64, c_cm, hm), put(0, 64 + hm, c_cm, hm), ones(0, 96, 32)
    put(1, 64 + hm, c_sm, hm)
    put(2, 64, c_sm, hm, -1.0)
    for g in (0, 64):
        put(3, g, c_cn, hn), put(3, g + hn, c_cn, hn), ones(3, g + 2 * hn, NSA_HD - 2 * hn)
        put(4, g + hn, c_sn, hn)
        put(5, g, c_sn, hn, -1.0)
    return jnp.dot(base, jnp.asarray(place), precision=lax.Precision.HIGHEST)


def _mla_weights(w_uq, w_ukv, q_norm, kv_norm):
    hw = MLA_NOPE + MLA_ROPE
    pad_r = lambda m: jnp.concatenate([m, jnp.zeros((256 - MLA_Q_RANK, m.shape[1]), F32)], axis=0)
    z = lambda n: jnp.zeros((MLA_Q_RANK, n), F32)
    zk = lambda n: jnp.zeros((MLA_KV_RANK, n), F32)
    w1, w2, wk, wv = [], [], [], []
    for h in range(MLA_HEADS):
        qh = w_uq[:, h * hw:(h + 1) * hw]
        w1 += [qh, z(128 - hw)]
        w2 += [z(MLA_NOPE), _rot_cols(qh[:, MLA_NOPE:], MLA_ROPE // 2), z(128 - hw)]
        kvh = w_ukv[:, h * (MLA_NOPE + MLA_V):(h + 1) * (MLA_NOPE + MLA_V)]
        wk += [kvh[:, :MLA_NOPE], zk(128 - MLA_NOPE)]
        wv += [kvh[:, MLA_NOPE:], zk(128 - MLA_V)]
    qg = jnp.concatenate([q_norm, jnp.zeros((256 - MLA_Q_RANK,), F32)])[None, :]
    return (qg, pad_r(jnp.concatenate(w1, axis=1)).astype(BF16), pad_r(jnp.concatenate(w2, axis=1)).astype(BF16),
            kv_norm[None, :], jnp.concatenate(wk, axis=1).astype(BF16), jnp.concatenate(wv, axis=1).astype(BF16))


def _compress_weights(pe, w_cmp):
    half = CMP_LEN // 2
    wk = w_cmp[0].reshape(CMP_LEN, NSA_HD, NSA_HD)
    wv = w_cmp[1].reshape(CMP_LEN, NSA_HD, NSA_HD)
    z = jnp.zeros((half, NSA_HD, NSA_HD), F32)

    def interleave(ks, vs):
        top = jnp.concatenate([ks, z], axis=-1)
        bot = jnp.concatenate([z, vs], axis=-1)
        return jnp.concatenate([top, bot], axis=1).reshape(half * 2 * NSA_HD, 2 * NSA_HD)

    pad8 = lambda p: jnp.concatenate([p.reshape(1, -1), jnp.zeros((7, CMP_LEN * NSA_HD), F32)], axis=0)
    return (interleave(wk[:half], wv[:half]), interleave(wk[half:], wv[half:]),
            pad8(pe[0]), pad8(pe[1]), w_cmp[0], w_cmp[1])


def _overlap_matrix(S):
    nc_pad = S // CMP_STRIDE
    n = np.arange(nc_pad)[:, None]
    j = np.arange(SEL_LANES)[None, :]
    ov = ((n * CMP_STRIDE <= j * SEL_LEN + SEL_LEN - 1) & (n * CMP_STRIDE + CMP_LEN - 1 >= j * SEL_LEN)
          & (n < nc_pad - 1) & (j < S // SEL_LEN))
    return jnp.asarray(ov.astype(np.float32)).astype(BF16)


def _gate_expand():
    gw = NSA_HEADS * NSA_HD
    ex = np.zeros((128, 3 * gw), np.float32)
    for br in range(3):
        for h in range(NSA_HEADS):
            ex[h * 3 + br, br * gw + h * NSA_HD:br * gw + (h + 1) * NSA_HD] = 1.0
    return jnp.asarray(ex).astype(BF16)


def _lru_gate_weights(w_gate, b_gate):
    wg = jnp.zeros((LRU_W, 2 * LRU_W), F32)
    bw = LRU_W // LRU_BLOCKS
    for g in range(2):
        for n in range(LRU_BLOCKS):
            wg = wg.at[n * bw:(n + 1) * bw, g * LRU_W + n * bw:g * LRU_W + (n + 1) * bw].set(w_gate[g, n])
    return wg.astype(BF16), b_gate.reshape(1, 2 * LRU_W)


def _pad_lanes(v, n=128):
    return jnp.concatenate([v, jnp.zeros((n - v.shape[0],), F32)])[None, :]


def kernel(x, positions, norm_mix, w_in, mla_q_norm, mla_w_uq, mla_kv_norm, mla_w_ukv, nsa_cmp_pe, nsa_w_cmp,
           lru_conv_w, lru_conv_b, lru_w_gate, lru_b_gate, lru_lambda, ssd_conv_w, ssd_conv_b, ssd_dt_bias,
           ssd_a_log, ssd_d, group_norm, w_out, norm_ffn, ffn_w_gate, ffn_w_up, ffn_w_down, moe_router,
           moe_w_gate, moe_w_up, moe_w_down, norm_final):
    B, S, D = x.shape
    T = B * S
    depth = w_in.shape[0]
    assert S // SEL_LEN <= SEL_LANES and S % 512 == 0
    rope_tab = _rope_tables(positions)
    ov = _overlap_matrix(S)
    ex = _gate_expand()
    h_res = x.reshape(T, D)
    for l in range(depth):
        w_raw = _aligned_w_in(w_in[l])
        ua, qn, kvc, ksel, kwin, vsel, vwin, gate, uc, ud = _inproj(
            h_res, norm_mix[l][None, :], w_raw, rope_tab, S)

        q_m, k_m, v_m = _mla_prep(ua, *_mla_weights(mla_w_uq[l], mla_w_ukv[l], mla_q_norm[l], mla_kv_norm[l]),
                                  rope_tab, B, S)
        y_a = _flash(q_m, k_m, v_m, window=None, tq=1024).reshape(T, GROUP_W)

        kvcmp = _compress(kvc.reshape(B, S // CMP_STRIDE, CMP_STRIDE * 128),
                          *_compress_weights(nsa_cmp_pe[l], nsa_w_cmp[l]))
        o_c, q_aug = _cmp_select(qn, kvcmp, ov, B, S)
        o_s = _flash(q_aug, ksel.reshape(B, 1, S, 256), vsel.reshape(B, 1, S, 128), window=None,
                     tq=512).reshape(T, GROUP_W)
        o_w = _flash(q_aug, kwin.reshape(B, 1, S, 256), vwin.reshape(B, 1, S, 128), window=WINDOW,
                     tq=512).reshape(T, GROUP_W)

        wg_l, bg_l = _lru_gate_weights(lru_w_gate[l], lru_b_gate[l])
        y_c = _lru(uc, lru_conv_w[l], lru_conv_b[l][None, :], wg_l, bg_l,
                   jax.nn.softplus(-lru_lambda[l])[None, :], B, S)

        y_d = _ssd(ud, ssd_conv_w[l], ssd_conv_b[l][None, :], _pad_lanes(ssd_dt_bias[l]),
                   _pad_lanes(-jnp.exp(ssd_a_log[l])), jnp.repeat(ssd_d[l], SSD_HD)[None, :],
                   group_norm[l, 3][None, :], B, S)

        moe_layer = l % 2 == 1
        rw = None
        if moe_layer:
            rw = jnp.concatenate([moe_router[l // 2], jnp.zeros((D, 128 - N_EXPERTS), F32)], axis=1)
            rw = jnp.stack(_split2(rw))
        outs = _outproj(y_a, o_c, o_s, o_w, gate, y_c, y_d, h_res, group_norm[l], ex, w_out[l].astype(BF16),
                        norm_ffn[l][None, :], rw)
        if moe_layer:
            h_res, hn, rg = outs
            normed = l == depth - 1
            h_res = _moe(hn, h_res, rg, moe_w_gate[l // 2], moe_w_up[l // 2], moe_w_down[l // 2],
                         norm_final[None, :], normed)
        else:
            normed = False
            h_res, hn = outs
            h_res = _ffn(hn, h_res, ffn_w_gate[l // 2].astype(BF16), ffn_w_up[l // 2].astype(BF16),
                         ffn_w_down[l // 2].astype(BF16))
    if not normed:
        h_res = _final_norm(h_res, norm_final[None, :])
    return h_res.reshape(B, S, D).astype(x.dtype)
```

```python
import functools

import numpy as np
import jax
import jax.numpy as jnp
from jax import lax
from jax.experimental import pallas as pl
from jax.experimental.pallas import tpu as pltpu

F32 = jnp.float32
BF16 = jnp.bfloat16

D_MODEL = 1024
GROUP_W = 256
ROPE_THETA = 500000.0
NORM_EPS = 1e-6
NEG = -1e30
FORCE = 1e4
MLA_HEADS, MLA_NOPE, MLA_ROPE, MLA_V = 4, 64, 32, 64
MLA_Q_RANK, MLA_KV_RANK = 192, 128
NSA_HEADS, NSA_HD, NSA_ROT = 4, 64, 16
CMP_STRIDE, CMP_LEN, SEL_LEN, SEL_TOPN, WINDOW = 16, 32, 64, 16, 512
LRU_W, LRU_BLOCKS, LRU_C, CONV_W = 256, 4, 8.0, 4
SSD_HEADS, SSD_HD, SSD_GROUPS, SSD_STATE, SSD_CHUNK = 4, 64, 2, 128, 128
SSD_DI = 256
SSD_XBC = SSD_DI + 2 * SSD_GROUPS * SSD_STATE
D_FF = 3584
N_EXPERTS = 8

O_NSA = MLA_Q_RANK + MLA_KV_RANK + MLA_ROPE
O_LRU = O_NSA + NSA_HEADS * NSA_HD + 6 * NSA_HD + 3 * NSA_HEADS
O_SSD = O_LRU + 2 * LRU_W
SSD_IN_W = 1152
P_NSA, P_LRU, W_RAW = 384, 1152, 2816
LOG2E = 1.4426950408889634
SEL_LANES = 128

VMEM_LIMIT = 56 * 1024 * 1024


def _cparams(sem):
    return pltpu.CompilerParams(dimension_semantics=sem, vmem_limit_bytes=VMEM_LIMIT)


def _dot(a, b):
    return jnp.dot(a.astype(BF16), b.astype(BF16), preferred_element_type=F32)


def _dot_nt(a, b):
    return lax.dot_general(a.astype(BF16), b.astype(BF16), (((1,), (1,)), ((), ())),
                           preferred_element_type=F32)


def _dot_f32(a, b):
    return jnp.dot(a, b, precision=lax.Precision.HIGHEST, preferred_element_type=F32)


def _dot_f32_nt(a, b):
    return lax.dot_general(a, b, (((1,), (1,)), ((), ())), precision=lax.Precision.HIGHEST,
                           preferred_element_type=F32)


def _split2(x):
    hi = x.astype(BF16)
    return hi, (x - hi.astype(F32)).astype(BF16)


def _sigmoid(x):
    return 1.0 / (1.0 + jnp.exp(-x))


def _softplus(x):
    return jnp.maximum(x, 0.0) + jnp.log(1.0 + jnp.exp(-jnp.abs(x)))


def _rms(x, width):
    return x * lax.rsqrt(jnp.sum(x * x, axis=-1, keepdims=True) * (1.0 / width) + NORM_EPS)


def _rope_tile(x, tab_ref, slot, half):
    t0 = 3 * 128 * slot
    return (x * tab_ref[:, t0:t0 + 128] + pltpu.roll(x, half, axis=1) * tab_ref[:, t0 + 128:t0 + 256]
            + pltpu.roll(x, 128 - half, axis=1) * tab_ref[:, t0 + 256:t0 + 384])


def _inproj_kernel(x_ref, g_ref, w_ref, tab_ref,
                   ua_ref, qn_ref, kvc_ref, ksel_ref, kwin_ref, vsel_ref, vwin_ref, gate_ref, uc_ref, ud_ref,
                   *, tm, seq_blocks):
    x = x_ref[...]
    h = _rms(x, D_MODEL) * g_ref[...]
    y = _dot(h, w_ref[...])
    lane = lax.broadcasted_iota(jnp.int32, (tm, 128), 1)
    lo64 = lane < 64
    ua_ref[:, 0:128] = y[:, 0:128]
    ua_ref[:, 128:256] = jnp.where(lo64, y[:, 128:256], 0.0)
    ua_ref[:, 256:384] = y[:, MLA_Q_RANK:MLA_Q_RANK + MLA_KV_RANK]
    kpe = jnp.where((lane >= 64) & (lane < 64 + MLA_ROPE), y[:, 256:384], 0.0)
    ua_ref[:, 384:512] = _rope_tile(kpe, tab_ref, 0, MLA_ROPE // 2)
    yn = y[:, P_NSA:P_NSA + 768]
    qn_ref[:, 0:128] = _rope_tile(yn[:, 0:128], tab_ref, 1, NSA_ROT // 2)
    qn_ref[:, 128:256] = _rope_tile(yn[:, 128:256], tab_ref, 1, NSA_ROT // 2)
    kvc_ref[...] = jnp.where(lo64, _rope_tile(yn[:, 256:384], tab_ref, 1, NSA_ROT // 2), yn[:, 256:384])
    s0 = (pl.program_id(0) % seq_blocks) * tm
    pos = s0 + lax.broadcasted_iota(jnp.int32, (tm, SEL_LANES), 0)
    onehot = jnp.where(jnp.right_shift(pos, 6) == lane, 1.0, 0.0)
    for k_ref, v_ref, c0, extra in ((ksel_ref, vsel_ref, 384, onehot), (kwin_ref, vwin_ref, 512, None)):
        kv = yn[:, c0:c0 + 128]
        k_ref[:, 0:128] = jnp.where(lo64, _rope_tile(kv, tab_ref, 1, NSA_ROT // 2), 0.0).astype(BF16)
        k_ref[:, 128:256] = (jnp.zeros((tm, 128), F32) if extra is None else extra).astype(BF16)
        v_ref[...] = jnp.where(lo64, pltpu.roll(kv, 64, axis=1), 1.0).astype(BF16)
    gate_ref[...] = jnp.where(lane < 3 * NSA_HEADS, yn[:, 640:768], 0.0)
    yl = y[:, P_LRU:P_LRU + 512 + SSD_IN_W]
    uc_ref[...] = yl[:, 0:512]
    ud_ref[:, 0:SSD_IN_W - 128] = yl[:, 512:512 + SSD_IN_W - 128]
    ud_ref[:, SSD_IN_W - 128:SSD_IN_W] = jnp.where(lane < SSD_HEADS, yl[:, 512 + SSD_IN_W - 128:512 + SSD_IN_W], 0.0)


def _inproj(x2d, g, w_raw, tab, seq):
    T = x2d.shape[0]
    tm = 512
    seq_blocks = seq // tm
    row = lambda w: pl.BlockSpec((tm, w), lambda i: (i, 0))
    full = lambda a: pl.BlockSpec(a.shape, lambda i: (0,) * a.ndim)
    outs = [(512, F32), (256, F32), (128, F32), (256, BF16), (256, BF16), (128, BF16), (128, BF16), (128, F32),
            (512, F32), (SSD_IN_W, F32)]
    return pl.pallas_call(
        functools.partial(_inproj_kernel, tm=tm, seq_blocks=seq_blocks),
        grid=(T // tm,),
        in_specs=[row(D_MODEL), full(g), full(w_raw), pl.BlockSpec((tm, tab.shape[1]), lambda i: (i % seq_blocks, 0))],
        out_specs=[row(w) for w, _ in outs],
        out_shape=[jax.ShapeDtypeStruct((T, w), dt) for w, dt in outs],
        compiler_params=_cparams(("parallel",)),
        name="inproj",
    )(x2d, g, w_raw, tab)


def _mla_prep_kernel(ua_ref, qg_ref, w1_ref, w2_ref, kvg_ref, wk_ref, wv_ref, tab_ref,
                     q_ref, k_ref, v_ref):
    ua = ua_ref[...]
    cq = _rms(ua[:, 0:256], MLA_Q_RANK) * qg_ref[...]
    y1 = _dot(cq, w1_ref[...])
    y2 = _dot(cq, w2_ref[...])
    ckv = _rms(ua[:, 256:384], MLA_KV_RANK) * kvg_ref[...]
    kn = _dot(ckv, wk_ref[...])
    vv = _dot(ckv, wv_ref[...])
    kpe = ua[:, 384:512]
    cos = tab_ref[:, 0:128]
    sin = tab_ref[:, 128:256] - tab_ref[:, 256:384]
    scale = LOG2E * (MLA_NOPE + MLA_ROPE) ** -0.5
    ones_hi = jnp.where(lax.broadcasted_iota(jnp.int32, cos.shape, 1) >= 64, 1.0, 0.0)
    for h in range(MLA_HEADS):
        sl = slice(h * 128, (h + 1) * 128)
        q_ref[h] = ((y1[:, sl] * cos + y2[:, sl] * sin) * scale).astype(BF16)
        k_ref[h] = (kn[:, sl] + kpe).astype(BF16)
        v_ref[h] = (vv[:, sl] + ones_hi).astype(BF16)


def _mla_prep(ua, qg, w1, w2, kvg, wk, wv, rope_tab, B, S):
    tm = 1024
    nb = S // tm
    full = lambda a: pl.BlockSpec(a.shape, lambda b, i: (0,) * a.ndim)
    tab = pl.BlockSpec((tm, 384), lambda b, i: (i, 0))
    hd = lambda w: pl.BlockSpec((None, MLA_HEADS, tm, w), lambda b, i: (b, 0, i, 0))
    return pl.pallas_call(
        _mla_prep_kernel,
        grid=(B, nb),
        in_specs=[pl.BlockSpec((tm, 512), lambda b, i: (b * nb + i, 0)),
                  full(qg), full(w1), full(w2), full(kvg), full(wk), full(wv), tab],
        out_specs=[hd(128), hd(128), hd(128)],
        out_shape=[jax.ShapeDtypeStruct((B, MLA_HEADS, S, 128), BF16)] * 3,
        compiler_params=_cparams(("parallel", "parallel")),
        name="mla_prep",
    )(ua, qg, w1, w2, kvg, wk, wv, rope_tab)


def _flash_kernel(qi_ref, ki_ref, flag_ref, q_ref, k_ref, v_ref, o_ref, m_ref, acc_ref,
                  *, heads, kv_heads, tq, tk, window):
    p_idx = pl.program_id(1)
    qi = qi_ref[p_idx]
    ki = ki_ref[p_idx]
    flags = flag_ref[p_idx]

    @pl.when(jnp.bitwise_and(flags, 1) != 0)
    def _init():
        m_ref[...] = jnp.full(m_ref.shape, NEG, F32)
        acc_ref[...] = jnp.zeros(acc_ref.shape, F32)

    def step(masked):
        if masked:
            qpos = qi * tq + lax.broadcasted_iota(jnp.int32, (tq, tk), 0)
            kpos = ki * tk + lax.broadcasted_iota(jnp.int32, (tq, tk), 1)
            mask = kpos <= qpos
            if window is not None:
                mask = mask & (kpos > qpos - window)
        nt = (((1,), (1,)), ((), ()))
        shared = kv_heads == 1
        if shared:
            dk = q_ref.shape[-1]
            s_all = lax.dot_general(q_ref[...].reshape(heads * tq, dk), k_ref[0], nt, preferred_element_type=F32)
        ps, alphas = [], []
        for h in range(heads):
            if shared:
                s = s_all[h * tq:(h + 1) * tq]
            else:
                s = lax.dot_general(q_ref[h], k_ref[h], nt, preferred_element_type=F32)
            if masked:
                s = jnp.where(mask, s, NEG)
            m_old = m_ref[h]
            m_new = jnp.maximum(m_old, jnp.max(s, axis=-1, keepdims=True))
            alpha = jnp.exp2(m_old - m_new)
            d = [s[:, c * 128:(c + 1) * 128] - m_new for c in range(tk // 128)]
            if shared:
                p = jnp.concatenate([jnp.exp2(x) for x in d], axis=1).astype(BF16)
            else:
                p = jnp.concatenate([jnp.exp2(x.astype(BF16)) for x in d], axis=1)
            m_ref[h] = m_new
            if shared:
                ps.append(p)
                alphas.append(alpha)
            else:
                acc_ref[h] = alpha * acc_ref[h] + jnp.dot(p, v_ref[h], preferred_element_type=F32)
        if shared:
            pv = jnp.dot(jnp.concatenate(ps, axis=0), v_ref[0], preferred_element_type=F32)
            for h in range(heads):
                acc_ref[h] = alphas[h] * acc_ref[h] + pv[h * tq:(h + 1) * tq]

    @pl.when(jnp.bitwise_and(flags, 4) != 0)
    def _masked():
        step(True)

    @pl.when(jnp.bitwise_and(flags, 4) == 0)
    def _plain():
        step(False)

    @pl.when(jnp.bitwise_and(flags, 2) != 0)
    def _fin():
        for h in range(heads):
            acc = acc_ref[h]
            o_ref[:, h * 64:(h + 1) * 64] = acc[:, 0:64] / acc[:, 64:128]


def _pair_tables(S, tq, tk, window):
    qi, ki, flags = [], [], []
    for i in range(S // tq):
        q_lo, q_hi = i * tq, i * tq + tq - 1
        lo = 0 if window is None else max(0, q_lo - window + 1) // tk
        hi = q_hi // tk
        for j in range(lo, hi + 1):
            k_lo, k_hi = j * tk, j * tk + tk - 1
            masked = k_hi > q_lo or (window is not None and k_lo <= q_hi - window)
            qi.append(i)
            ki.append(j)
            flags.append((1 if j == lo else 0) | (2 if j == hi else 0) | (4 if masked else 0))
    mk = lambda a: jnp.asarray(np.asarray(a, np.int32))
    return mk(qi), mk(ki), mk(flags)


def _flash(q, k, v, *, window, tq, tk=512):
    B, H, S, dk = q.shape
    Hk = k.shape[1]
    tq, tk = min(tq, S), min(tk, S)
    tabs = _pair_tables(S, tq, tk, window)
    npairs = int(tabs[0].shape[0])
    grid_spec = pltpu.PrefetchScalarGridSpec(
        num_scalar_prefetch=3,
        grid=(B, npairs),
        in_specs=[pl.BlockSpec((None, H, tq, dk), lambda b, p, qi, ki, fl: (b, 0, qi[p], 0)),
                  pl.BlockSpec((None, Hk, tk, dk), lambda b, p, qi, ki, fl: (b, 0, ki[p], 0)),
                  pl.BlockSpec((None, Hk, tk, 128), lambda b, p, qi, ki, fl: (b, 0, ki[p], 0))],
        out_specs=pl.BlockSpec((None, tq, H * 64), lambda b, p, qi, ki, fl: (b, qi[p], 0)),
        scratch_shapes=[pltpu.VMEM((H, tq, 128), F32), pltpu.VMEM((H, tq, 128), F32)],
    )
    return pl.pallas_call(
        functools.partial(_flash_kernel, heads=H, kv_heads=Hk, tq=tq, tk=tk, window=window),
        grid_spec=grid_spec,
        out_shape=jax.ShapeDtypeStruct((B, S, H * 64), F32),
        compiler_params=_cparams(("parallel", "arbitrary")),
        name="flash_attn",
    )(*tabs, q, k, v)


def _compress_kernel(x_ref, wa_ref, wb_ref, pek_ref, pev_ref, wkf_ref, wvf_ref, o_ref, *, nc_pad):
    x = x_ref[...]
    a = _dot_f32(x, wa_ref[...])
    b = _dot_f32(x, wb_ref[...])
    ck = _dot_f32(pek_ref[...], wkf_ref[...])[0:1]
    cv = _dot_f32(pev_ref[...], wvf_ref[...])[0:1]
    const = jnp.concatenate([ck, cv], axis=-1)
    b_next = pltpu.roll(b, nc_pad - 1, axis=0)
    row = lax.broadcasted_iota(jnp.int32, (nc_pad, 128), 0)
    o_ref[...] = jnp.where(row < nc_pad - 1, a + b_next + const, 0.0)


def _compress(x, wa, wb, pek, pev, wkf, wvf):
    B, nc_pad, w = x.shape
    full = lambda a: pl.BlockSpec(a.shape, lambda b: (0,) * a.ndim)
    return pl.pallas_call(
        functools.partial(_compress_kernel, nc_pad=nc_pad),
        grid=(B,),
        in_specs=[pl.BlockSpec((None, nc_pad, w), lambda b: (b, 0, 0)),
                  full(wa), full(wb), full(pek), full(pev), full(wkf), full(wvf)],
        out_specs=pl.BlockSpec((None, nc_pad, 128), lambda b: (b, 0, 0)),
        out_shape=jax.ShapeDtypeStruct((B, nc_pad, 128), F32),
        compiler_params=_cparams(("parallel",)),
        name="nsa_compress",
    )(x, wa, wb, pek, pev, wkf, wvf)


def _cmp_select_kernel(q_ref, kvc_ref, ov_ref, oc_ref, qaug_ref, *, tq, nc_pad, n_top):
    i = pl.program_id(1)
    q = q_ref[...]
    kc = kvc_ref[:, 0:64]
    vc = kvc_ref[:, 64:128]
    scale = NSA_HD ** -0.5
    qpos = i * tq + lax.broadcasted_iota(jnp.int32, (tq, 1), 0)
    n_idx = lax.broadcasted_iota(jnp.int32, (1, nc_pad), 1)
    m_c = (n_idx * CMP_STRIDE + (CMP_LEN - 1) <= qpos) & (n_idx < nc_pad - 1)
    kc_hi, kc_lo = _split2(kc)
    kc3 = jnp.concatenate([kc_hi, kc_lo, kc_hi], axis=1)
    psum = jnp.zeros((tq, nc_pad), F32)
    for h in range(NSA_HEADS):
        q_hi, q_lo = _split2(q[:, h * NSA_HD:(h + 1) * NSA_HD])
        q3 = jnp.concatenate([q_hi, q_hi, q_lo], axis=1)
        s = lax.dot_general(q3, kc3, (((1,), (1,)), ((), ())), preferred_element_type=F32)
        s = jnp.where(m_c, s * scale, NEG)
        e = jnp.where(m_c, jnp.exp(s - jnp.max(s, axis=-1, keepdims=True)), 0.0)
        den = jnp.sum(e, axis=-1, keepdims=True)
        p = e / jnp.where(den > 0.0, den, 1.0)
        oc_ref[:, h * NSA_HD:(h + 1) * NSA_HD] = _dot(p, vc)
        psum = psum + p
    ov = ov_ref[...]
    p_hi = psum.astype(BF16)
    p_r = psum - p_hi.astype(F32)
    p_mid = p_r.astype(BF16)
    p_lo = (p_r - p_mid.astype(F32)).astype(BF16)
    imp = (jnp.dot(p_hi, ov, preferred_element_type=F32) + jnp.dot(p_mid, ov, preferred_element_type=F32)
           + jnp.dot(p_lo, ov, preferred_element_type=F32))
    cur = jnp.right_shift(qpos, 6)
    jj = lax.broadcasted_iota(jnp.int32, (1, SEL_LANES), 1)
    forced = (jj == 0) | (jj == cur) | (jj == cur - 1)
    bias = jnp.where(forced, 0.0, NEG)
    imp = jnp.where((jj <= cur) & jnp.logical_not(forced), imp, NEG)
    jf = jj.astype(F32)
    for _ in range(n_top - 3):
        mx = jnp.max(imp, axis=-1, keepdims=True)
        idx = jnp.min(jnp.where(imp == mx, jf, float(SEL_LANES)), axis=-1, keepdims=True)
        hit = jf == idx
        bias = jnp.where(hit & (mx > 0.5 * NEG), 0.0, bias)
        imp = jnp.where(hit, -3e38, imp)
    bias = bias.astype(BF16)
    qs = (q * (scale * LOG2E)).astype(BF16)
    for h in range(NSA_HEADS):
        qaug_ref[h, :, 0:64] = qs[:, h * NSA_HD:(h + 1) * NSA_HD]
        qaug_ref[h, :, 64:128] = jnp.zeros((tq, 64), BF16)
        qaug_ref[h, :, 128:256] = bias


def _cmp_select(qn, kvcmp, ov, B, S):
    tq = min(1024, S)
    nb = S // tq
    nc_pad = kvcmp.shape[1]
    n_top = min(SEL_TOPN, S // SEL_LEN)
    assert n_top >= 3
    return pl.pallas_call(
        functools.partial(_cmp_select_kernel, tq=tq, nc_pad=nc_pad, n_top=n_top),
        grid=(B, nb),
        in_specs=[pl.BlockSpec((tq, 256), lambda b, i: (b * nb + i, 0)),
                  pl.BlockSpec((None, nc_pad, 128), lambda b, i: (b, 0, 0)),
                  pl.BlockSpec(ov.shape, lambda b, i: (0, 0))],
        out_specs=[pl.BlockSpec((tq, 256), lambda b, i: (b * nb + i, 0)),
                   pl.BlockSpec((None, NSA_HEADS, tq, 256), lambda b, i: (b, 0, i, 0))],
        out_shape=[jax.ShapeDtypeStruct((B * S, 256), F32),
                   jax.ShapeDtypeStruct((B, NSA_HEADS, S, 256), BF16)],
        compiler_params=_cparams(("parallel", "parallel")),
        name="nsa_cmp_select",
    )(qn, kvcmp, ov)


def _shift_scan(a, b, t, width):
    row = lax.broadcasted_iota(jnp.int32, (t, width), 0)
    s = 1
    while s < t:
        keep = row >= s
        a_sh = jnp.where(keep, pltpu.roll(a, s, axis=0), 1.0)
        b_sh = jnp.where(keep, pltpu.roll(b, s, axis=0), 0.0)
        b = a * b_sh + b
        a = a * a_sh
        s *= 2
    return a, b


def _causal_conv(xpad_ref, x, cw_ref, cb_ref, t):
    xpad_ref[8:8 + t, :] = x
    y = cb_ref[...] + cw_ref[CONV_W - 1:CONV_W, :] * x
    for k in range(CONV_W - 1):
        off = 8 - (CONV_W - 1) + k
        y = y + cw_ref[k:k + 1, :] * xpad_ref[off:off + t, :]
    xpad_ref[0:8, :] = x[t - 8:t, :]
    return y


def _lru_kernel(u_ref, cw_ref, cb_ref, wg_ref, bg_ref, sp_ref, y_ref, xpad_ref, h_ref, *, t):
    @pl.when(pl.program_id(1) == 0)
    def _init():
        xpad_ref[0:8, :] = jnp.zeros((8, LRU_W), F32)
        h_ref[...] = jnp.zeros(h_ref.shape, F32)

    x = u_ref[:, 0:LRU_W]
    gbr = u_ref[:, LRU_W:2 * LRU_W]
    xb = _causal_conv(xpad_ref, x, cw_ref, cb_ref, t)
    g = _dot(xb, wg_ref[...]) + bg_ref[...]
    r = _sigmoid(g[:, 0:LRU_W])
    ig = _sigmoid(g[:, LRU_W:2 * LRU_W])
    log_a = -LRU_C * r * sp_ref[...]
    a = jnp.exp(log_a)
    one_m = -jnp.tanh(log_a) * (a * a + 1.0)
    b = jnp.sqrt(jnp.maximum(one_m, 0.0)) * (ig * xb)
    a_cum, h = _shift_scan(a, b, t, LRU_W)
    h = h + a_cum * h_ref[0:1, :]
    h_ref[0:1, :] = h[t - 1:t, :]
    gelu = 0.5 * gbr * (1.0 + jnp.tanh(0.7978845608028654 * (gbr + 0.044715 * gbr * gbr * gbr)))
    y_ref[...] = h * gelu


def _lru(uc, cw, cb, wg, bg, sp, B, S):
    t = 512
    nb = S // t
    full = lambda a: pl.BlockSpec(a.shape, lambda b, i: (0,) * a.ndim)
    return pl.pallas_call(
        functools.partial(_lru_kernel, t=t),
        grid=(B, nb),
        in_specs=[pl.BlockSpec((t, 2 * LRU_W), lambda b, i: (b * nb + i, 0)),
                  full(cw), full(cb), full(wg), full(bg), full(sp)],
        out_specs=pl.BlockSpec((t, LRU_W), lambda b, i: (b * nb + i, 0)),
        out_shape=jax.ShapeDtypeStruct((B * S, LRU_W), F32),
        scratch_shapes=[pltpu.VMEM((t + 8, LRU_W), F32), pltpu.VMEM((8, LRU_W), F32)],
        compiler_params=_cparams(("parallel", "arbitrary")),
        name="rglru",
    )(uc, cw, cb, wg, bg, sp)


def _ssd_kernel(u_ref, cw_ref, cb_ref, dtb_ref, a_ref, d_ref, ng_ref, y_ref, xpad_ref, st_ref, *, t):
    @pl.when(pl.program_id(1) == 0)
    def _init():
        xpad_ref[0:8, :] = jnp.zeros((8, SSD_XBC), F32)
        st_ref[...] = jnp.zeros(st_ref.shape, F32)

    L = SSD_CHUNK
    z = u_ref[:, 0:SSD_DI]
    conv = _causal_conv(xpad_ref, u_ref[:, SSD_DI:SSD_DI + SSD_XBC], cw_ref, cb_ref, t)
    xbc = conv * _sigmoid(conv)
    dt = _softplus(u_ref[:, SSD_DI + SSD_XBC:SSD_IN_W] + dtb_ref[...])
    a = dt * a_ref[...]
    row = lax.broadcasted_iota(jnp.int32, (t, 128), 0)
    rin = jnp.bitwise_and(row, L - 1)
    cs = a
    s = 1
    while s < L:
        cs = cs + jnp.where(rin >= s, pltpu.roll(cs, s, axis=0), 0.0)
        s *= 2
    tril = lax.broadcasted_iota(jnp.int32, (L, L), 0) >= lax.broadcasted_iota(jnp.int32, (L, L), 1)
    gn = SSD_GROUPS * SSD_STATE
    rep = SSD_HEADS // SSD_GROUPS
    for c in range(t // L):
        rs = slice(c * L, (c + 1) * L)
        cs_c = cs[rs]
        cs_t = cs_c.T
        cs_last = cs_c[L - 1:L, :]
        ys = []
        for g in range(SSD_GROUPS):
            bg = xbc[rs, SSD_DI + g * SSD_STATE:SSD_DI + (g + 1) * SSD_STATE]
            cg = xbc[rs, SSD_DI + gn + g * SSD_STATE:SSD_DI + gn + (g + 1) * SSD_STATE]
            gmat = _dot_nt(cg, bg)
            bg_t = bg.T
            for hh in range(rep):
                h = g * rep + hh
                cs_col = cs_c[:, h:h + 1]
                lm = jnp.exp(jnp.where(tril, cs_col - cs_t[h:h + 1, :], NEG))
                xh = xbc[rs, h * SSD_HD:(h + 1) * SSD_HD]
                xdt = xh * dt[rs, h:h + 1]
                st_old = st_ref[h]
                y_h = _dot(gmat * lm, xdt) + _dot(cg, st_old) * jnp.exp(cs_col)
                dec = jnp.exp(cs_last[:, h:h + 1] - cs_col)
                st_ref[h] = jnp.exp(cs_last[:, h:h + 1]) * st_old + _dot(bg_t, xdt * dec)
                ys.append(y_h)
        y = jnp.concatenate(ys, axis=-1) + xbc[rs, 0:SSD_DI] * d_ref[...]
        zc = z[rs]
        y = y * (zc * _sigmoid(zc))
        y_ref[rs, :] = _rms(y, SSD_DI) * ng_ref[...]


def _ssd(ud, cw, cb, dtb, a_neg, d_vec, ng, B, S):
    t = 512
    nb = S // t
    full = lambda a: pl.BlockSpec(a.shape, lambda b, i: (0,) * a.ndim)
    return pl.pallas_call(
        functools.partial(_ssd_kernel, t=t),
        grid=(B, nb),
        in_specs=[pl.BlockSpec((t, SSD_IN_W), lambda b, i: (b * nb + i, 0)),
                  full(cw), full(cb), full(dtb), full(a_neg), full(d_vec), full(ng)],
        out_specs=pl.BlockSpec((t, SSD_DI), lambda b, i: (b * nb + i, 0)),
        out_shape=jax.ShapeDtypeStruct((B * S, SSD_DI), F32),
        scratch_shapes=[pltpu.VMEM((t + 8, SSD_XBC), F32), pltpu.VMEM((SSD_HEADS, SSD_STATE, SSD_HD), F32)],
        compiler_params=_cparams(("parallel", "arbitrary")),
        name="ssd",
    )(ud, cw, cb, dtb, a_neg, d_vec, ng)


def _outproj_kernel(*refs, with_router):
    if with_router:
        (ya_ref, oc_ref, os_ref, ow_ref, gate_ref, yc_ref, yd_ref, res_ref, gn_ref, ex_ref, w_ref, nf_ref,
         rw_ref, hres_ref, hn_ref, rg_ref) = refs
    else:
        (ya_ref, oc_ref, os_ref, ow_ref, gate_ref, yc_ref, yd_ref, res_ref, gn_ref, ex_ref, w_ref, nf_ref,
         hres_ref, hn_ref) = refs
    sg_hi, sg_lo = _split2(_sigmoid(gate_ref[...]))
    ex = ex_ref[...]
    gx = jnp.dot(sg_hi, ex, preferred_element_type=F32) + jnp.dot(sg_lo, ex, preferred_element_type=F32)
    yb = gx[:, 0:256] * oc_ref[...] + gx[:, 256:512] * os_ref[...] + gx[:, 512:768] * ow_ref[...]
    y = jnp.concatenate([_rms(ya_ref[...], GROUP_W) * gn_ref[0:1, :],
                         _rms(yb, GROUP_W) * gn_ref[1:2, :],
                         _rms(yc_ref[...], GROUP_W) * gn_ref[2:3, :],
                         yd_ref[...]], axis=-1)
    hres = res_ref[...] + _dot(y, w_ref[...])
    hres_ref[...] = hres
    hn = _rms(hres, D_MODEL) * nf_ref[...]
    hn_ref[...] = hn.astype(hn_ref.dtype)
    if with_router:
        h_hi, h_lo = _split2(hn)
        logits = (jnp.dot(h_hi, rw_ref[0], preferred_element_type=F32)
                  + jnp.dot(h_hi, rw_ref[1], preferred_element_type=F32)
                  + jnp.dot(h_lo, rw_ref[0], preferred_element_type=F32))
        lane = lax.broadcasted_iota(jnp.int32, logits.shape, 1)
        lf = lane.astype(F32)
        logits = jnp.where(lane < N_EXPERTS, logits, NEG)
        m1 = jnp.max(logits, axis=-1, keepdims=True)
        i1 = jnp.min(jnp.where(logits == m1, lf, 128.0), axis=-1, keepdims=True)
        rest = jnp.where(lf == i1, NEG, logits)
        m2 = jnp.max(rest, axis=-1, keepdims=True)
        i2 = jnp.min(jnp.where(rest == m2, lf, 128.0), axis=-1, keepdims=True)
        e2 = jnp.exp(m2 - m1)
        den = 1.0 + e2
        rg_ref[...] = jnp.where(lane == 0, i1, jnp.where(lane == 1, i2, jnp.where(
            lane == 2, 1.0 / den, jnp.where(lane == 3, e2 / den, 0.0))))


def _outproj(ya, oc, osel, ow, gate, yc, yd, res, gn, ex, w, nf, rw):
    T = ya.shape[0]
    tm = 512
    with_router = rw is not None
    row = lambda wd: pl.BlockSpec((tm, wd), lambda i: (i, 0))
    full = lambda a: pl.BlockSpec(a.shape, lambda i: (0,) * a.ndim)
    ins = [ya, oc, osel, ow, gate, yc, yd, res, gn, ex, w, nf]
    in_specs = [row(256), row(256), row(256), row(256), row(128), row(256), row(256), row(D_MODEL),
                full(gn), full(ex), full(w), full(nf)]
    out_specs = [row(D_MODEL), row(D_MODEL)]
    out_shape = [jax.ShapeDtypeStruct((T, D_MODEL), F32),
                 jax.ShapeDtypeStruct((T, D_MODEL), F32 if with_router else BF16)]
    if with_router:
        ins.append(rw)
        in_specs.append(full(rw))
        out_specs.append(row(128))
        out_shape.append(jax.ShapeDtypeStruct((T, 128), F32))
    return pl.pallas_call(
        functools.partial(_outproj_kernel, with_router=with_router),
        grid=(T // tm,),
        in_specs=in_specs,
        out_specs=out_specs,
        out_shape=out_shape,
        compiler_params=_cparams(("parallel",)),
        name="outproj",
    )(*ins)


def _ffn_kernel(h_ref, res_ref, wg_ref, wu_ref, wd_ref, o_ref, acc_ref):
    j = pl.program_id(1)

    @pl.when(j == 0)
    def _init():
        acc_ref[...] = jnp.zeros(acc_ref.shape, F32)

    h = h_ref[...]
    g = jnp.dot(h, wg_ref[...], preferred_element_type=F32)
    u = jnp.dot(h, wu_ref[...], preferred_element_type=F32)
    acc_ref[...] += _dot(g * _sigmoid(g) * u, wd_ref[...])

    @pl.when(j == pl.num_programs(1) - 1)
    def _fin():
        o_ref[...] = res_ref[...] + acc_ref[...]


def _ffn(hn, res, wg, wu, wd):
    T = hn.shape[0]
    tm, tf = 1024, 512
    return pl.pallas_call(
        _ffn_kernel,
        grid=(T // tm, D_FF // tf),
        in_specs=[pl.BlockSpec((tm, D_MODEL), lambda i, j: (i, 0)),
                  pl.BlockSpec((tm, D_MODEL), lambda i, j: (i, 0)),
                  pl.BlockSpec((D_MODEL, tf), lambda i, j: (0, j)),
                  pl.BlockSpec((D_MODEL, tf), lambda i, j: (0, j)),
                  pl.BlockSpec((tf, D_MODEL), lambda i, j: (j, 0))],
        out_specs=pl.BlockSpec((tm, D_MODEL), lambda i, j: (i, 0)),
        out_shape=jax.ShapeDtypeStruct((T, D_MODEL), F32),
        scratch_shapes=[pltpu.VMEM((tm, D_MODEL), F32)],
        compiler_params=_cparams(("parallel", "arbitrary")),
        name="ffn_dense",
    )(hn, res, wg, wu, wd)


MOE_TILE = 1024
MOE_TM = 512


def _route_rank_kernel(route_ref, rank_ref, cnt_ref, carry_ref, *, tm):
    @pl.when(pl.program_id(0) == 0)
    def _init():
        carry_ref[...] = jnp.zeros(carry_ref.shape, F32)

    r = route_ref[...]
    i1 = r[:, 0:1]
    i2 = r[:, 1:2]
    lane = lax.broadcasted_iota(jnp.int32, (tm, 128), 1)
    lf = lane.astype(F32)
    oh = jnp.where((lf == i1) | (lf == i2), 1.0, 0.0)
    row = lax.broadcasted_iota(jnp.int32, (tm, 128), 0)
    cs = oh
    s = 1
    while s < tm:
        cs = cs + jnp.where(row >= s, pltpu.roll(cs, s, axis=0), 0.0)
        s *= 2
    excl = cs - oh + carry_ref[0:1, :]
    rank1 = jnp.sum(jnp.where(lf == i1, excl, 0.0), axis=-1, keepdims=True)
    rank2 = jnp.sum(jnp.where(lf == i2, excl, 0.0), axis=-1, keepdims=True)
    rank_ref[...] = jnp.where(lane == 0, rank1, jnp.where(lane == 1, rank2, 0.0))
    carry_ref[0:1, :] = carry_ref[0:1, :] + cs[tm - 1:tm, :]
    cnt_ref[...] = carry_ref[...]


def _route_rank(route):
    T = route.shape[0]
    tm = MOE_TM
    return pl.pallas_call(
        functools.partial(_route_rank_kernel, tm=tm),
        grid=(T // tm,),
        in_specs=[pl.BlockSpec((tm, 128), lambda i: (i, 0))],
        out_specs=[pl.BlockSpec((tm, 128), lambda i: (i, 0)), pl.BlockSpec((8, 128), lambda i: (0, 0))],
        out_shape=[jax.ShapeDtypeStruct((T, 128), F32), jax.ShapeDtypeStruct((8, 128), F32)],
        scratch_shapes=[pltpu.VMEM((8, 128), F32)],
        compiler_params=_cparams(("arbitrary",)),
        name="moe_rank",
    )(route)


def _row_copies(idx_ref, tm, make_copy):
    def body(r, carry):
        for k in range(2):
            make_copy(k, r, idx_ref[k * tm + r]).start(priority=k)
        return carry

    lax.fori_loop(0, tm, body, 0, unroll=8)


def _dispatch_kernel(ends_ref, slots_ref, h_ref, xs_ref, idx_ref, zero_ref, stage_ref, sem_idx, sem_row, *, tm):
    i = pl.program_id(0)

    @pl.when(i == 0)
    def _zero_tiles():
        zero_ref[...] = jnp.zeros(zero_ref.shape, F32)
        zr = zero_ref.shape[0]
        jobs = []
        for e in range(N_EXPERTS):
            start = ends_ref[e - 1] if e else 0
            jobs.append((ends_ref[e] > start, ends_ref[e] - MOE_TILE))
        for tile in range(N_EXPERTS):
            base = xs_ref.shape[0] - (tile + 1) * MOE_TILE
            jobs.append((base >= ends_ref[N_EXPERTS - 1], base))

        def tile_copies(base):
            rows = [base + q * zr for q in range(MOE_TILE // zr)]
            rows = [r if isinstance(r, int) else pl.multiple_of(r, zr) for r in rows]
            return [pltpu.make_async_copy(zero_ref, xs_ref.at[pl.ds(r, zr)], sem_idx) for r in rows]

        for cond, base in jobs:
            @pl.when(cond)
            def _start(base=base):
                for cp in tile_copies(base):
                    cp.start()
        for cond, base in jobs:
            @pl.when(cond)
            def _wait(base=base):
                for cp in tile_copies(base):
                    cp.wait()

    cp = pltpu.make_async_copy(slots_ref.at[pl.ds(i * 2 * tm, 2 * tm)], idx_ref, sem_idx)
    cp.start()
    cp.wait()
    cur = lax.rem(i, 2)

    def wait_tile(b):
        for _ in range(2):
            pltpu.make_async_copy(stage_ref.at[b], xs_ref.at[pl.ds(0, tm)], sem_row.at[b]).wait()

    @pl.when(i > 1)
    def _reuse():
        wait_tile(cur)

    stage_ref[cur] = h_ref[...]
    _row_copies(idx_ref, tm, lambda k, r, s: pltpu.make_async_copy(
        stage_ref.at[cur, pl.ds(r, 1)], xs_ref.at[pl.ds(s, 1)], sem_row.at[cur]))

    @pl.when(i == pl.num_programs(0) - 1)
    def _drain():
        @pl.when(i > 0)
        def _prev():
            wait_tile(1 - cur)
        wait_tile(cur)


def _dispatch(ends, slots, hn, n_slots):
    T = hn.shape[0]
    tm = MOE_TM
    grid_spec = pltpu.PrefetchScalarGridSpec(
        num_scalar_prefetch=1,
        grid=(T // tm,),
        in_specs=[pl.BlockSpec(memory_space=pl.ANY),
                  pl.BlockSpec((tm, D_MODEL), lambda i, ends: (i, 0))],
        out_specs=pl.BlockSpec(memory_space=pl.ANY),
        scratch_shapes=[pltpu.SMEM((2 * tm,), jnp.int32), pltpu.VMEM((256, D_MODEL), F32),
                        pltpu.VMEM((2, tm, D_MODEL), F32),
                        pltpu.SemaphoreType.DMA, pltpu.SemaphoreType.DMA((2,))],
    )
    return pl.pallas_call(
        functools.partial(_dispatch_kernel, tm=tm),
        grid_spec=grid_spec,
        out_shape=jax.ShapeDtypeStruct((n_slots, D_MODEL), F32),
        compiler_params=_cparams(("arbitrary",)),
        name="moe_dispatch",
    )(ends, slots, hn)


def _expert_ffn_kernel(te_ref, nu_ref, x_ref, wg_ref, wu_ref, wd_ref, o_ref, acc_ref):
    del te_ref
    i = pl.program_id(0)
    j = pl.program_id(1)
    last = pl.num_programs(1) - 1
    used = i < nu_ref[0]

    @pl.when(used & (j == 0))
    def _init():
        acc_ref[...] = jnp.zeros(acc_ref.shape, F32)

    @pl.when(used)
    def _step():
        x = x_ref[...]
        g = _dot(x, wg_ref[...])
        u = _dot(x, wu_ref[...])
        acc_ref[...] += _dot(g * _sigmoid(g) * u, wd_ref[...])

    @pl.when(used & (j == last))
    def _fin():
        o_ref[...] = acc_ref[...]

    @pl.when(jnp.logical_not(used) & (j == last))
    def _unused():
        o_ref[...] = jnp.zeros(o_ref.shape, F32)


def _expert_ffn(tile_expert, n_used, xs, wg, wu, wd):
    n_slots = xs.shape[0]
    tm, tf = MOE_TILE, 512
    jw = lambda i, j, nu: jnp.where(i < nu[0], j, 0)
    grid_spec = pltpu.PrefetchScalarGridSpec(
        num_scalar_prefetch=2,
        grid=(n_slots // tm, D_FF // tf),
        in_specs=[pl.BlockSpec((tm, D_MODEL), lambda i, j, te, nu: (jnp.minimum(i, nu[0] - 1), 0)),
                  pl.BlockSpec((None, D_MODEL, tf), lambda i, j, te, nu: (te[i], 0, jw(i, j, nu))),
                  pl.BlockSpec((None, D_MODEL, tf), lambda i, j, te, nu: (te[i], 0, jw(i, j, nu))),
                  pl.BlockSpec((None, tf, D_MODEL), lambda i, j, te, nu: (te[i], jw(i, j, nu), 0))],
        out_specs=pl.BlockSpec((tm, D_MODEL), lambda i, j, te, nu: (i, 0)),
        scratch_shapes=[pltpu.VMEM((tm, D_MODEL), F32)],
    )
    return pl.pallas_call(
        _expert_ffn_kernel,
        grid_spec=grid_spec,
        out_shape=jax.ShapeDtypeStruct((n_slots, D_MODEL), F32),
        compiler_params=_cparams(("parallel", "arbitrary")),
        name="moe_expert_ffn",
    )(tile_expert, n_used, xs, wg, wu, wd)


def _combine_kernel(slots_ref, ys_ref, route_ref, res_ref, ng_ref, o_ref, idx_ref, buf_ref, sem_idx, sem_row,
                    *, tm, final_norm):
    i = pl.program_id(0)
    cur = lax.rem(i, 2)

    def gather(step, b):
        cp = pltpu.make_async_copy(slots_ref.at[pl.ds(step * 2 * tm, 2 * tm)], idx_ref, sem_idx)
        cp.start()
        cp.wait()
        _row_copies(idx_ref, tm, lambda k, r, s: pltpu.make_async_copy(
            ys_ref.at[pl.ds(s, 1)], buf_ref.at[b, k, pl.ds(r, 1)], sem_row.at[b]))

    @pl.when(i == 0)
    def _first():
        gather(0, 0)

    @pl.when(i + 1 < pl.num_programs(0))
    def _next():
        gather(i + 1, 1 - cur)

    for k in range(2):
        pltpu.make_async_copy(ys_ref.at[pl.ds(0, tm)], buf_ref.at[cur, k], sem_row.at[cur]).wait()
    route = route_ref[...]
    out = res_ref[...] + route[:, 2:3] * buf_ref[cur, 0] + route[:, 3:4] * buf_ref[cur, 1]
    if final_norm:
        out = _rms(out, D_MODEL) * ng_ref[...]
    o_ref[...] = out


def _combine(slots, ys, route, res, norm_g, final_norm):
    T = res.shape[0]
    tm = MOE_TM
    return pl.pallas_call(
        functools.partial(_combine_kernel, tm=tm, final_norm=final_norm),
        grid=(T // tm,),
        in_specs=[pl.BlockSpec(memory_space=pl.ANY),
                  pl.BlockSpec(memory_space=pl.ANY),
                  pl.BlockSpec((tm, 128), lambda i: (i, 0)),
                  pl.BlockSpec((tm, D_MODEL), lambda i: (i, 0)),
                  pl.BlockSpec((1, D_MODEL), lambda i: (0, 0))],
        out_specs=pl.BlockSpec((tm, D_MODEL), lambda i: (i, 0)),
        out_shape=jax.ShapeDtypeStruct((T, D_MODEL), F32),
        scratch_shapes=[pltpu.SMEM((2 * tm,), jnp.int32), pltpu.VMEM((2, 2, tm, D_MODEL), F32),
                        pltpu.SemaphoreType.DMA, pltpu.SemaphoreType.DMA((2,))],
        compiler_params=_cparams(("arbitrary",)),
        name="moe_combine",
    )(slots, ys, route, res, norm_g)


def _moe(hn, res, route, wg, wu, wd, norm_g, final_norm):
    T = hn.shape[0]
    n_slots = 2 * T + N_EXPERTS * MOE_TILE
    n_tiles = n_slots // MOE_TILE
    rank, cnt = _route_rank(route)
    counts = cnt[0, :N_EXPERTS].astype(jnp.int32)
    padded = (counts + MOE_TILE - 1) // MOE_TILE * MOE_TILE
    ends = jnp.cumsum(padded)
    starts = ends - padded
    ids = route[:, 0:2].astype(jnp.int32)
    slot = jnp.take(starts, ids) + rank[:, 0:2].astype(jnp.int32)
    slots = slot.reshape(T // MOE_TM, MOE_TM, 2).transpose(0, 2, 1).reshape(-1)
    tile_start = jnp.arange(n_tiles, dtype=jnp.int32) * MOE_TILE
    tile_expert = jnp.minimum(jnp.sum(tile_start[:, None] >= ends[None, :], axis=1), N_EXPERTS - 1).astype(jnp.int32)
    n_used = (ends[-1:] // MOE_TILE).astype(jnp.int32)
    xs = _dispatch(ends.astype(jnp.int32), slots, hn, n_slots)
    ys = _expert_ffn(tile_expert, n_used, xs, wg, wu, wd)
    return _combine(slots, ys, route, res, norm_g, final_norm)


def _final_norm_kernel(x_ref, g_ref, o_ref):
    o_ref[...] = _rms(x_ref[...], D_MODEL) * g_ref[...]


def _final_norm(x2d, g):
    T = x2d.shape[0]
    tm = 1024
    return pl.pallas_call(
        _final_norm_kernel,
        grid=(T // tm,),
        in_specs=[pl.BlockSpec((tm, D_MODEL), lambda i: (i, 0)), pl.BlockSpec(g.shape, lambda i: (0, 0))],
        out_specs=pl.BlockSpec((tm, D_MODEL), lambda i: (i, 0)),
        out_shape=jax.ShapeDtypeStruct((T, D_MODEL), F32),
        compiler_params=_cparams(("parallel",)),
        name="final_norm",
    )(x2d, g)


def _rot_cols(w, half):
    return jnp.concatenate([-w[:, half:2 * half], w[:, 0:half]], axis=1)


def _aligned_w_in(w):
    z = lambda n: jnp.zeros((w.shape[0], n), F32)
    return jnp.concatenate([w[:, 0:O_NSA], z(P_NSA - O_NSA),
                            w[:, O_NSA:O_LRU], z(P_LRU - P_NSA - (O_LRU - O_NSA)),
                            w[:, O_LRU:], z(W_RAW - P_LRU - (w.shape[1] - O_LRU))], axis=1).astype(BF16)


def _rope_tables(positions):
    pos = positions.astype(F32)[:, None]
    S = positions.shape[0]

    def cs(rot_dim):
        inv = ROPE_THETA ** (-jnp.arange(0, rot_dim, 2, dtype=F32) / rot_dim)
        ang = pos * inv[None, :]
        return jnp.cos(ang), jnp.sin(ang)

    cm, sm = cs(MLA_ROPE)
    cn, sn = cs(NSA_ROT)
    hm, hn = MLA_ROPE // 2, NSA_ROT // 2
    base = jnp.concatenate([cm, sm, cn, sn, jnp.ones((S, 1), F32)], axis=1)
    c_cm, c_sm, c_cn, c_sn, c_one = 0, hm, 2 * hm, 2 * hm + hn, 2 * hm + 2 * hn

    place = np.zeros((base.shape[1], 6 * 128), np.float32)

    def put(tile, lane0, col0, n, sign=1.0):
        for j in range(n):
            place[col0 + j, tile * 128 + lane0 + j] = sign

    def ones(tile, lane0, n):
        place[c_one, tile * 128 + lane0:tile * 128 + lane0 + n] = 1.0

    ones(0, 0, 64), put(0, 64, c_cm, hm), put(0, 64 + hm, c_cm, hm), ones(0, 96, 32)
    put(1, 64 + hm, c_sm, hm)
    put(2, 64, c_sm, hm, -1.0)
    for g in (0, 64):
        put(3, g, c_cn, hn), put(3, g + hn, c_cn, hn), ones(3, g + 2 * hn, NSA_HD - 2 * hn)
        put(4, g + hn, c_sn, hn)
        put(5, g, c_sn, hn, -1.0)
    return jnp.dot(base, jnp.asarray(place), precision=lax.Precision.HIGHEST)


def _mla_weights(w_uq, w_ukv, q_norm, kv_norm):
    hw = MLA_NOPE + MLA_ROPE
    pad_r = lambda m: jnp.concatenate([m, jnp.zeros((256 - MLA_Q_RANK, m.shape[1]), F32)], axis=0)
    z = lambda n: jnp.zeros((MLA_Q_RANK, n), F32)
    zk = lambda n: jnp.zeros((MLA_KV_RANK, n), F32)
    w1, w2, wk, wv = [], [], [], []
    for h in range(MLA_HEADS):
        qh = w_uq[:, h * hw:(h + 1) * hw]
        w1 += [qh, z(128 - hw)]
        w2 += [z(MLA_NOPE), _rot_cols(qh[:, MLA_NOPE:], MLA_ROPE // 2), z(128 - hw)]
        kvh = w_ukv[:, h * (MLA_NOPE + MLA_V):(h + 1) * (MLA_NOPE + MLA_V)]
        wk += [kvh[:, :MLA_NOPE], zk(128 - MLA_NOPE)]
        wv += [kvh[:, MLA_NOPE:], zk(128 - MLA_V)]
    qg = jnp.concatenate([q_norm, jnp.zeros((256 - MLA_Q_RANK,), F32)])[None, :]
    return (qg, pad_r(jnp.concatenate(w1, axis=1)).astype(BF16), pad_r(jnp.concatenate(w2, axis=1)).astype(BF16),
            kv_norm[None, :], jnp.concatenate(wk, axis=1).astype(BF16), jnp.concatenate(wv, axis=1).astype(BF16))


def _compress_weights(pe, w_cmp):
    half = CMP_LEN // 2
    wk = w_cmp[0].reshape(CMP_LEN, NSA_HD, NSA_HD)
    wv = w_cmp[1].reshape(CMP_LEN, NSA_HD, NSA_HD)
    z = jnp.zeros((half, NSA_HD, NSA_HD), F32)

    def interleave(ks, vs):
        top = jnp.concatenate([ks, z], axis=-1)
        bot = jnp.concatenate([z, vs], axis=-1)
        return jnp.concatenate([top, bot], axis=1).reshape(half * 2 * NSA_HD, 2 * NSA_HD)

    pad8 = lambda p: jnp.concatenate([p.reshape(1, -1), jnp.zeros((7, CMP_LEN * NSA_HD), F32)], axis=0)
    return (interleave(wk[:half], wv[:half]), interleave(wk[half:], wv[half:]),
            pad8(pe[0]), pad8(pe[1]), w_cmp[0], w_cmp[1])


def _overlap_matrix(S):
    nc_pad = S // CMP_STRIDE
    n = np.arange(nc_pad)[:, None]
    j = np.arange(SEL_LANES)[None, :]
    ov = ((n * CMP_STRIDE <= j * SEL_LEN + SEL_LEN - 1) & (n * CMP_STRIDE + CMP_LEN - 1 >= j * SEL_LEN)
          & (n < nc_pad - 1) & (j < S // SEL_LEN))
    return jnp.asarray(ov.astype(np.float32)).astype(BF16)


def _gate_expand():
    gw = NSA_HEADS * NSA_HD
    ex = np.zeros((128, 3 * gw), np.float32)
    for br in range(3):
        for h in range(NSA_HEADS):
            ex[h * 3 + br, br * gw + h * NSA_HD:br * gw + (h + 1) * NSA_HD] = 1.0
    return jnp.asarray(ex).astype(BF16)


def _lru_gate_weights(w_gate, b_gate):
    wg = jnp.zeros((LRU_W, 2 * LRU_W), F32)
    bw = LRU_W // LRU_BLOCKS
    for g in range(2):
        for n in range(LRU_BLOCKS):
            wg = wg.at[n * bw:(n + 1) * bw, g * LRU_W + n * bw:g * LRU_W + (n + 1) * bw].set(w_gate[g, n])
    return wg.astype(BF16), b_gate.reshape(1, 2 * LRU_W)


def _pad_lanes(v, n=128):
    return jnp.concatenate([v, jnp.zeros((n - v.shape[0],), F32)])[None, :]


def kernel(x, positions, norm_mix, w_in, mla_q_norm, mla_w_uq, mla_kv_norm, mla_w_ukv, nsa_cmp_pe, nsa_w_cmp,
           lru_conv_w, lru_conv_b, lru_w_gate, lru_b_gate, lru_lambda, ssd_conv_w, ssd_conv_b, ssd_dt_bias,
           ssd_a_log, ssd_d, group_norm, w_out, norm_ffn, ffn_w_gate, ffn_w_up, ffn_w_down, moe_router,
           moe_w_gate, moe_w_up, moe_w_down, norm_final):
    B, S, D = x.shape
    T = B * S
    depth = w_in.shape[0]
    assert S // SEL_LEN <= SEL_LANES and S % 512 == 0
    rope_tab = _rope_tables(positions)
    ov = _overlap_matrix(S)
    ex = _gate_expand()
    h_res = x.reshape(T, D)
    for l in range(depth):
        w_raw = _aligned_w_in(w_in[l])
        ua, qn, kvc, ksel, kwin, vsel, vwin, gate, uc, ud = _inproj(
            h_res, norm_mix[l][None, :], w_raw, rope_tab, S)

        q_m, k_m, v_m = _mla_prep(ua, *_mla_weights(mla_w_uq[l], mla_w_ukv[l], mla_q_norm[l], mla_kv_norm[l]),
                                  rope_tab, B, S)
        y_a = _flash(q_m, k_m, v_m, window=None, tq=1024).reshape(T, GROUP_W)

        kvcmp = _compress(kvc.reshape(B, S // CMP_STRIDE, CMP_STRIDE * 128),
                          *_compress_weights(nsa_cmp_pe[l], nsa_w_cmp[l]))
        o_c, q_aug = _cmp_select(qn, kvcmp, ov, B, S)
        o_s = _flash(q_aug, ksel.reshape(B, 1, S, 256), vsel.reshape(B, 1, S, 128), window=None,
                     tq=1024).reshape(T, GROUP_W)
        o_w = _flash(q_aug, kwin.reshape(B, 1, S, 256), vwin.reshape(B, 1, S, 128), window=WINDOW,
                     tq=512).reshape(T, GROUP_W)

        wg_l, bg_l = _lru_gate_weights(lru_w_gate[l], lru_b_gate[l])
        y_c = _lru(uc, lru_conv_w[l], lru_conv_b[l][None, :], wg_l, bg_l,
                   jax.nn.softplus(-lru_lambda[l])[None, :], B, S)

        y_d = _ssd(ud, ssd_conv_w[l], ssd_conv_b[l][None, :], _pad_lanes(ssd_dt_bias[l]),
                   _pad_lanes(-jnp.exp(ssd_a_log[l])), jnp.repeat(ssd_d[l], SSD_HD)[None, :],
                   group_norm[l, 3][None, :], B, S)

        moe_layer = l % 2 == 1
        rw = None
        if moe_layer:
            rw = jnp.concatenate([moe_router[l // 2], jnp.zeros((D, 128 - N_EXPERTS), F32)], axis=1)
            rw = jnp.stack(_split2(rw))
        outs = _outproj(y_a, o_c, o_s, o_w, gate, y_c, y_d, h_res, group_norm[l], ex, w_out[l].astype(BF16),
                        norm_ffn[l][None, :], rw)
        if moe_layer:
            h_res, hn, rg = outs
            normed = l == depth - 1
            h_res = _moe(hn, h_res, rg, moe_w_gate[l // 2], moe_w_up[l // 2], moe_w_down[l // 2],
                         norm_final[None, :], normed)
        else:
            normed = False
            h_res, hn = outs
            h_res = _ffn(hn, h_res, ffn_w_gate[l // 2].astype(BF16), ffn_w_up[l // 2].astype(BF16),
                         ffn_w_down[l // 2].astype(BF16))
    if not normed:
        h_res = _final_norm(h_res, norm_final[None, :])
    return h_res.reshape(B, S, D).astype(x.dtype)
```

```python
import functools

import numpy as np
import jax
import jax.numpy as jnp
from jax import lax
from jax.experimental import pallas as pl
from jax.experimental.pallas import tpu as pltpu

F32 = jnp.float32
BF16 = jnp.bfloat16

D_MODEL = 1024
GROUP_W = 256
ROPE_THETA = 500000.0
NORM_EPS = 1e-6
NEG = -1e30
FORCE = 1e4
MLA_HEADS, MLA_NOPE, MLA_ROPE, MLA_V = 4, 64, 32, 64
MLA_Q_RANK, MLA_KV_RANK = 192, 128
NSA_HEADS, NSA_HD, NSA_ROT = 4, 64, 16
CMP_STRIDE, CMP_LEN, SEL_LEN, SEL_TOPN, WINDOW = 16, 32, 64, 16, 512
LRU_W, LRU_BLOCKS, LRU_C, CONV_W = 256, 4, 8.0, 4
SSD_HEADS, SSD_HD, SSD_GROUPS, SSD_STATE, SSD_CHUNK = 4, 64, 2, 128, 128
SSD_DI = 256
SSD_XBC = SSD_DI + 2 * SSD_GROUPS * SSD_STATE
D_FF = 3584
N_EXPERTS = 8

O_NSA = MLA_Q_RANK + MLA_KV_RANK + MLA_ROPE
O_LRU = O_NSA + NSA_HEADS * NSA_HD + 6 * NSA_HD + 3 * NSA_HEADS
O_SSD = O_LRU + 2 * LRU_W
SSD_IN_W = 1152
P_NSA, P_LRU, W_RAW = 384, 1152, 2816
LOG2E = 1.4426950408889634
SEL_LANES = 128

VMEM_LIMIT = 56 * 1024 * 1024


def _cparams(sem):
    return pltpu.CompilerParams(dimension_semantics=sem, vmem_limit_bytes=VMEM_LIMIT)


def _dot(a, b):
    return jnp.dot(a.astype(BF16), b.astype(BF16), preferred_element_type=F32)


def _dot_nt(a, b):
    return lax.dot_general(a.astype(BF16), b.astype(BF16), (((1,), (1,)), ((), ())),
                           preferred_element_type=F32)


def _dot_f32(a, b):
    return jnp.dot(a, b, precision=lax.Precision.HIGHEST, preferred_element_type=F32)


def _dot_f32_nt(a, b):
    return lax.dot_general(a, b, (((1,), (1,)), ((), ())), precision=lax.Precision.HIGHEST,
                           preferred_element_type=F32)


def _split2(x):
    hi = x.astype(BF16)
    return hi, (x - hi.astype(F32)).astype(BF16)


def _sigmoid(x):
    return 1.0 / (1.0 + jnp.exp(-x))


def _softplus(x):
    return jnp.maximum(x, 0.0) + jnp.log(1.0 + jnp.exp(-jnp.abs(x)))


def _rms(x, width):
    return x * lax.rsqrt(jnp.sum(x * x, axis=-1, keepdims=True) * (1.0 / width) + NORM_EPS)


def _rope_tile(x, tab_ref, slot, half):
    t0 = 3 * 128 * slot
    return (x * tab_ref[:, t0:t0 + 128] + pltpu.roll(x, half, axis=1) * tab_ref[:, t0 + 128:t0 + 256]
            + pltpu.roll(x, 128 - half, axis=1) * tab_ref[:, t0 + 256:t0 + 384])


def _inproj_kernel(x_ref, g_ref, w_ref, tab_ref,
                   ua_ref, qn_ref, kvc_ref, ksel_ref, kwin_ref, vsel_ref, vwin_ref, gate_ref, uc_ref, ud_ref,
                   *, tm, seq_blocks):
    x = x_ref[...]
    h = _rms(x, D_MODEL) * g_ref[...]
    y = _dot(h, w_ref[...])
    lane = lax.broadcasted_iota(jnp.int32, (tm, 128), 1)
    lo64 = lane < 64
    ua_ref[:, 0:128] = y[:, 0:128]
    ua_ref[:, 128:256] = jnp.where(lo64, y[:, 128:256], 0.0)
    ua_ref[:, 256:384] = y[:, MLA_Q_RANK:MLA_Q_RANK + MLA_KV_RANK]
    kpe = jnp.where((lane >= 64) & (lane < 64 + MLA_ROPE), y[:, 256:384], 0.0)
    ua_ref[:, 384:512] = _rope_tile(kpe, tab_ref, 0, MLA_ROPE // 2)
    yn = y[:, P_NSA:P_NSA + 768]
    qn_ref[:, 0:128] = _rope_tile(yn[:, 0:128], tab_ref, 1, NSA_ROT // 2)
    qn_ref[:, 128:256] = _rope_tile(yn[:, 128:256], tab_ref, 1, NSA_ROT // 2)
    kvc_ref[...] = jnp.where(lo64, _rope_tile(yn[:, 256:384], tab_ref, 1, NSA_ROT // 2), yn[:, 256:384])
    s0 = (pl.program_id(0) % seq_blocks) * tm
    pos = s0 + lax.broadcasted_iota(jnp.int32, (tm, SEL_LANES), 0)
    onehot = jnp.where(jnp.right_shift(pos, 6) == lane, 1.0, 0.0)
    for k_ref, v_ref, c0, extra in ((ksel_ref, vsel_ref, 384, onehot), (kwin_ref, vwin_ref, 512, None)):
        kv = yn[:, c0:c0 + 128]
        k_ref[:, 0:128] = jnp.where(lo64, _rope_tile(kv, tab_ref, 1, NSA_ROT // 2), 0.0).astype(BF16)
        k_ref[:, 128:256] = (jnp.zeros((tm, 128), F32) if extra is None else extra).astype(BF16)
        v_ref[...] = jnp.where(lo64, pltpu.roll(kv, 64, axis=1), 1.0).astype(BF16)
    gate_ref[...] = jnp.where(lane < 3 * NSA_HEADS, yn[:, 640:768], 0.0)
    yl = y[:, P_LRU:P_LRU + 512 + SSD_IN_W]
    uc_ref[...] = yl[:, 0:512]
    ud_ref[:, 0:SSD_IN_W - 128] = yl[:, 512:512 + SSD_IN_W - 128]
    ud_ref[:, SSD_IN_W - 128:SSD_IN_W] = jnp.where(lane < SSD_HEADS, yl[:, 512 + SSD_IN_W - 128:512 + SSD_IN_W], 0.0)


def _inproj(x2d, g, w_raw, tab, seq):
    T = x2d.shape[0]
    tm = 512
    seq_blocks = seq // tm
    row = lambda w: pl.BlockSpec((tm, w), lambda i: (i, 0))
    full = lambda a: pl.BlockSpec(a.shape, lambda i: (0,) * a.ndim)
    outs = [(512, F32), (256, F32), (128, F32), (256, BF16), (256, BF16), (128, BF16), (128, BF16), (128, F32),
            (512, F32), (SSD_IN_W, F32)]
    return pl.pallas_call(
        functools.partial(_inproj_kernel, tm=tm, seq_blocks=seq_blocks),
        grid=(T // tm,),
        in_specs=[row(D_MODEL), full(g), full(w_raw), pl.BlockSpec((tm, tab.shape[1]), lambda i: (i % seq_blocks, 0))],
        out_specs=[row(w) for w, _ in outs],
        out_shape=[jax.ShapeDtypeStruct((T, w), dt) for w, dt in outs],
        compiler_params=_cparams(("parallel",)),
        name="inproj",
    )(x2d, g, w_raw, tab)


def _mla_prep_kernel(ua_ref, qg_ref, w1_ref, w2_ref, kvg_ref, wk_ref, wv_ref, tab_ref,
                     q_ref, k_ref, v_ref):
    ua = ua_ref[...]
    cq = _rms(ua[:, 0:256], MLA_Q_RANK) * qg_ref[...]
    y1 = _dot(cq, w1_ref[...])
    y2 = _dot(cq, w2_ref[...])
    ckv = _rms(ua[:, 256:384], MLA_KV_RANK) * kvg_ref[...]
    kn = _dot(ckv, wk_ref[...])
    vv = _dot(ckv, wv_ref[...])
    kpe = ua[:, 384:512]
    cos = tab_ref[:, 0:128]
    sin = tab_ref[:, 128:256] - tab_ref[:, 256:384]
    scale = LOG2E * (MLA_NOPE + MLA_ROPE) ** -0.5
    ones_hi = jnp.where(lax.broadcasted_iota(jnp.int32, cos.shape, 1) >= 64, 1.0, 0.0)
    for h in range(MLA_HEADS):
        sl = slice(h * 128, (h + 1) * 128)
        q_ref[h] = ((y1[:, sl] * cos + y2[:, sl] * sin) * scale).astype(BF16)
        k_ref[h] = (kn[:, sl] + kpe).astype(BF16)
        v_ref[h] = (vv[:, sl] + ones_hi).astype(BF16)


def _mla_prep(ua, qg, w1, w2, kvg, wk, wv, rope_tab, B, S):
    tm = 1024
    nb = S // tm
    full = lambda a: pl.BlockSpec(a.shape, lambda b, i: (0,) * a.ndim)
    tab = pl.BlockSpec((tm, 384), lambda b, i: (i, 0))
    hd = lambda w: pl.BlockSpec((None, MLA_HEADS, tm, w), lambda b, i: (b, 0, i, 0))
    return pl.pallas_call(
        _mla_prep_kernel,
        grid=(B, nb),
        in_specs=[pl.BlockSpec((tm, 512), lambda b, i: (b * nb + i, 0)),
                  full(qg), full(w1), full(w2), full(kvg), full(wk), full(wv), tab],
        out_specs=[hd(128), hd(128), hd(128)],
        out_shape=[jax.ShapeDtypeStruct((B, MLA_HEADS, S, 128), BF16)] * 3,
        compiler_params=_cparams(("parallel", "parallel")),
        name="mla_prep",
    )(ua, qg, w1, w2, kvg, wk, wv, rope_tab)


def _flash_kernel(qi_ref, ki_ref, flag_ref, q_ref, k_ref, v_ref, o_ref, m_ref, acc_ref,
                  *, heads, kv_heads, tq, tk, window):
    p_idx = pl.program_id(1)
    qi = qi_ref[p_idx]
    ki = ki_ref[p_idx]
    flags = flag_ref[p_idx]

    @pl.when(jnp.bitwise_and(flags, 1) != 0)
    def _init():
        m_ref[...] = jnp.full(m_ref.shape, NEG, F32)
        acc_ref[...] = jnp.zeros(acc_ref.shape, F32)

    def step(masked):
        if masked:
            qpos = qi * tq + lax.broadcasted_iota(jnp.int32, (tq, tk), 0)
            kpos = ki * tk + lax.broadcasted_iota(jnp.int32, (tq, tk), 1)
            mask = kpos <= qpos
            if window is not None:
                mask = mask & (kpos > qpos - window)
        nt = (((1,), (1,)), ((), ()))
        shared = kv_heads == 1
        if shared:
            dk = q_ref.shape[-1]
            s_all = lax.dot_general(q_ref[...].reshape(heads * tq, dk), k_ref[0], nt, preferred_element_type=F32)
        ps, alphas = [], []
        for h in range(heads):
            if shared:
                s = s_all[h * tq:(h + 1) * tq]
            else:
                s = lax.dot_general(q_ref[h], k_ref[h], nt, preferred_element_type=F32)
            if masked:
                s = jnp.where(mask, s, NEG)
            m_old = m_ref[h]
            m_new = jnp.maximum(m_old, jnp.max(s, axis=-1, keepdims=True))
            alpha = jnp.exp2(m_old - m_new)
            d = [s[:, c * 128:(c + 1) * 128] - m_new for c in range(tk // 128)]
            if shared:
                p = jnp.concatenate([jnp.exp2(x) for x in d], axis=1).astype(BF16)
            else:
                p = jnp.concatenate([jnp.exp2(x.astype(BF16)) for x in d], axis=1)
            m_ref[h] = m_new
            if shared:
                ps.append(p)
                alphas.append(alpha)
            else:
                acc_ref[h] = alpha * acc_ref[h] + jnp.dot(p, v_ref[h], preferred_element_type=F32)
        if shared:
            pv = jnp.dot(jnp.concatenate(ps, axis=0), v_ref[0], preferred_element_type=F32)
            for h in range(heads):
                acc_ref[h] = alphas[h] * acc_ref[h] + pv[h * tq:(h + 1) * tq]

    @pl.when(jnp.bitwise_and(flags, 4) != 0)
    def _masked():
        step(True)

    @pl.when(jnp.bitwise_and(flags, 4) == 0)
    def _plain():
        step(False)

    @pl.when(jnp.bitwise_and(flags, 2) != 0)
    def _fin():
        for h in range(heads):
            acc = acc_ref[h]
            o_ref[:, h * 64:(h + 1) * 64] = acc[:, 0:64] / acc[:, 64:128]


def _pair_tables(S, tq, tk, window):
    qi, ki, flags = [], [], []
    for i in range(S // tq):
        q_lo, q_hi = i * tq, i * tq + tq - 1
        lo = 0 if window is None else max(0, q_lo - window + 1) // tk
        hi = q_hi // tk
        for j in range(lo, hi + 1):
            k_lo, k_hi = j * tk, j * tk + tk - 1
            masked = k_hi > q_lo or (window is not None and k_lo <= q_hi - window)
            qi.append(i)
            ki.append(j)
            flags.append((1 if j == lo else 0) | (2 if j == hi else 0) | (4 if masked else 0))
    mk = lambda a: jnp.asarray(np.asarray(a, np.int32))
    return mk(qi), mk(ki), mk(flags)


def _flash(q, k, v, *, window, tq, tk=512):
    B, H, S, dk = q.shape
    Hk = k.shape[1]
    tq, tk = min(tq, S), min(tk, S)
    tabs = _pair_tables(S, tq, tk, window)
    npairs = int(tabs[0].shape[0])
    grid_spec = pltpu.PrefetchScalarGridSpec(
        num_scalar_prefetch=3,
        grid=(B, npairs),
        in_specs=[pl.BlockSpec((None, H, tq, dk), lambda b, p, qi, ki, fl: (b, 0, qi[p], 0)),
                  pl.BlockSpec((None, Hk, tk, dk), lambda b, p, qi, ki, fl: (b, 0, ki[p], 0)),
                  pl.BlockSpec((None, Hk, tk, 128), lambda b, p, qi, ki, fl: (b, 0, ki[p], 0))],
        out_specs=pl.BlockSpec((None, tq, H * 64), lambda b, p, qi, ki, fl: (b, qi[p], 0)),
        scratch_shapes=[pltpu.VMEM((H, tq, 128), F32), pltpu.VMEM((H, tq, 128), F32)],
    )
    return pl.pallas_call(
        functools.partial(_flash_kernel, heads=H, kv_heads=Hk, tq=tq, tk=tk, window=window),
        grid_spec=grid_spec,
        out_shape=jax.ShapeDtypeStruct((B, S, H * 64), F32),
        compiler_params=_cparams(("parallel", "arbitrary")),
        name="flash_attn",
    )(*tabs, q, k, v)


def _compress_kernel(x_ref, wa_ref, wb_ref, pek_ref, pev_ref, wkf_ref, wvf_ref, o_ref, *, nc_pad):
    x = x_ref[...]
    a = _dot_f32(x, wa_ref[...])
    b = _dot_f32(x, wb_ref[...])
    ck = _dot_f32(pek_ref[...], wkf_ref[...])[0:1]
    cv = _dot_f32(pev_ref[...], wvf_ref[...])[0:1]
    const = jnp.concatenate([ck, cv], axis=-1)
    b_next = pltpu.roll(b, nc_pad - 1, axis=0)
    row = lax.broadcasted_iota(jnp.int32, (nc_pad, 128), 0)
    o_ref[...] = jnp.where(row < nc_pad - 1, a + b_next + const, 0.0)


def _compress(x, wa, wb, pek, pev, wkf, wvf):
    B, nc_pad, w = x.shape
    full = lambda a: pl.BlockSpec(a.shape, lambda b: (0,) * a.ndim)
    return pl.pallas_call(
        functools.partial(_compress_kernel, nc_pad=nc_pad),
        grid=(B,),
        in_specs=[pl.BlockSpec((None, nc_pad, w), lambda b: (b, 0, 0)),
                  full(wa), full(wb), full(pek), full(pev), full(wkf), full(wvf)],
        out_specs=pl.BlockSpec((None, nc_pad, 128), lambda b: (b, 0, 0)),
        out_shape=jax.ShapeDtypeStruct((B, nc_pad, 128), F32),
        compiler_params=_cparams(("parallel",)),
        name="nsa_compress",
    )(x, wa, wb, pek, pev, wkf, wvf)


def _cmp_select_kernel(q_ref, kvc_ref, ov_ref, oc_ref, qaug_ref, *, tq, nc_pad, n_top):
    i = pl.program_id(1)
    q = q_ref[...]
    kc = kvc_ref[:, 0:64]
    vc = kvc_ref[:, 64:128]
    scale = NSA_HD ** -0.5
    qpos = i * tq + lax.broadcasted_iota(jnp.int32, (tq, 1), 0)
    n_idx = lax.broadcasted_iota(jnp.int32, (1, nc_pad), 1)
    m_c = (n_idx * CMP_STRIDE + (CMP_LEN - 1) <= qpos) & (n_idx < nc_pad - 1)
    kc_hi, kc_lo = _split2(kc)
    kc3 = jnp.concatenate([kc_hi, kc_lo, kc_hi], axis=1)
    psum = jnp.zeros((tq, nc_pad), F32)
    for h in range(NSA_HEADS):
        q_hi, q_lo = _split2(q[:, h * NSA_HD:(h + 1) * NSA_HD])
        q3 = jnp.concatenate([q_hi, q_hi, q_lo], axis=1)
        s = lax.dot_general(q3, kc3, (((1,), (1,)), ((), ())), preferred_element_type=F32)
        s = jnp.where(m_c, s * scale, NEG)
        e = jnp.where(m_c, jnp.exp(s - jnp.max(s, axis=-1, keepdims=True)), 0.0)
        den = jnp.sum(e, axis=-1, keepdims=True)
        p = e / jnp.where(den > 0.0, den, 1.0)
        oc_ref[:, h * NSA_HD:(h + 1) * NSA_HD] = _dot(p, vc)
        psum = psum + p
    ov = ov_ref[...]
    p_hi = psum.astype(BF16)
    p_r = psum - p_hi.astype(F32)
    p_mid = p_r.astype(BF16)
    p_lo = (p_r - p_mid.astype(F32)).astype(BF16)
    imp = (jnp.dot(p_hi, ov, preferred_element_type=F32) + jnp.dot(p_mid, ov, preferred_element_type=F32)
           + jnp.dot(p_lo, ov, preferred_element_type=F32))
    cur = jnp.right_shift(qpos, 6)
    jj = lax.broadcasted_iota(jnp.int32, (1, SEL_LANES), 1)
    forced = (jj == 0) | (jj == cur) | (jj == cur - 1)
    bias = jnp.where(forced, 0.0, NEG)
    imp = jnp.where((jj <= cur) & jnp.logical_not(forced), imp, NEG)
    jf = jj.astype(F32)
    for _ in range(n_top - 3):
        mx = jnp.max(imp, axis=-1, keepdims=True)
        idx = jnp.min(jnp.where(imp == mx, jf, float(SEL_LANES)), axis=-1, keepdims=True)
        hit = jf == idx
        bias = jnp.where(hit & (mx > 0.5 * NEG), 0.0, bias)
        imp = jnp.where(hit, -3e38, imp)
    bias = bias.astype(BF16)
    qs = (q * (scale * LOG2E)).astype(BF16)
    for h in range(NSA_HEADS):
        qaug_ref[h, :, 0:64] = qs[:, h * NSA_HD:(h + 1) * NSA_HD]
        qaug_ref[h, :, 64:128] = jnp.zeros((tq, 64), BF16)
        qaug_ref[h, :, 128:256] = bias


def _cmp_select(qn, kvcmp, ov, B, S):
    tq = min(1024, S)
    nb = S // tq
    nc_pad = kvcmp.shape[1]
    n_top = min(SEL_TOPN, S // SEL_LEN)
    assert n_top >= 3
    return pl.pallas_call(
        functools.partial(_cmp_select_kernel, tq=tq, nc_pad=nc_pad, n_top=n_top),
        grid=(B, nb),
        in_specs=[pl.BlockSpec((tq, 256), lambda b, i: (b * nb + i, 0)),
                  pl.BlockSpec((None, nc_pad, 128), lambda b, i: (b, 0, 0)),
                  pl.BlockSpec(ov.shape, lambda b, i: (0, 0))],
        out_specs=[pl.BlockSpec((tq, 256), lambda b, i: (b * nb + i, 0)),
                   pl.BlockSpec((None, NSA_HEADS, tq, 256), lambda b, i: (b, 0, i, 0))],
        out_shape=[jax.ShapeDtypeStruct((B * S, 256), F32),
                   jax.ShapeDtypeStruct((B, NSA_HEADS, S, 256), BF16)],
        compiler_params=_cparams(("parallel", "parallel")),
        name="nsa_cmp_select",
    )(qn, kvcmp, ov)


def _shift_scan(a, b, t, width):
    row = lax.broadcasted_iota(jnp.int32, (t, width), 0)
    s = 1
    while s < t:
        keep = row >= s
        a_sh = jnp.where(keep, pltpu.roll(a, s, axis=0), 1.0)
        b_sh = jnp.where(keep, pltpu.roll(b, s, axis=0), 0.0)
        b = a * b_sh + b
        a = a * a_sh
        s *= 2
    return a, b


def _causal_conv(xpad_ref, x, cw_ref, cb_ref, t):
    xpad_ref[8:8 + t, :] = x
    y = cb_ref[...] + cw_ref[CONV_W - 1:CONV_W, :] * x
    for k in range(CONV_W - 1):
        off = 8 - (CONV_W - 1) + k
        y = y + cw_ref[k:k + 1, :] * xpad_ref[off:off + t, :]
    xpad_ref[0:8, :] = x[t - 8:t, :]
    return y


def _lru_kernel(u_ref, cw_ref, cb_ref, wg_ref, bg_ref, sp_ref, y_ref, xpad_ref, h_ref, *, t):
    @pl.when(pl.program_id(1) == 0)
    def _init():
        xpad_ref[0:8, :] = jnp.zeros((8, LRU_W), F32)
        h_ref[...] = jnp.zeros(h_ref.shape, F32)

    x = u_ref[:, 0:LRU_W]
    gbr = u_ref[:, LRU_W:2 * LRU_W]
    xb = _causal_conv(xpad_ref, x, cw_ref, cb_ref, t)
    g = _dot(xb, wg_ref[...]) + bg_ref[...]
    r = _sigmoid(g[:, 0:LRU_W])
    ig = _sigmoid(g[:, LRU_W:2 * LRU_W])
    log_a = -LRU_C * r * sp_ref[...]
    a = jnp.exp(log_a)
    one_m = -jnp.tanh(log_a) * (a * a + 1.0)
    b = jnp.sqrt(jnp.maximum(one_m, 0.0)) * (ig * xb)
    a_cum, h = _shift_scan(a, b, t, LRU_W)
    h = h + a_cum * h_ref[0:1, :]
    h_ref[0:1, :] = h[t - 1:t, :]
    gelu = 0.5 * gbr * (1.0 + jnp.tanh(0.7978845608028654 * (gbr + 0.044715 * gbr * gbr * gbr)))
    y_ref[...] = h * gelu


def _lru(uc, cw, cb, wg, bg, sp, B, S):
    t = 512
    nb = S // t
    full = lambda a: pl.BlockSpec(a.shape, lambda b, i: (0,) * a.ndim)
    return pl.pallas_call(
        functools.partial(_lru_kernel, t=t),
        grid=(B, nb),
        in_specs=[pl.BlockSpec((t, 2 * LRU_W), lambda b, i: (b * nb + i, 0)),
                  full(cw), full(cb), full(wg), full(bg), full(sp)],
        out_specs=pl.BlockSpec((t, LRU_W), lambda b, i: (b * nb + i, 0)),
        out_shape=jax.ShapeDtypeStruct((B * S, LRU_W), F32),
        scratch_shapes=[pltpu.VMEM((t + 8, LRU_W), F32), pltpu.VMEM((8, LRU_W), F32)],
        compiler_params=_cparams(("parallel", "arbitrary")),
        name="rglru",
    )(uc, cw, cb, wg, bg, sp)


def _ssd_kernel(u_ref, cw_ref, cb_ref, dtb_ref, a_ref, d_ref, ng_ref, y_ref, xpad_ref, st_ref, *, t):
    @pl.when(pl.program_id(1) == 0)
    def _init():
        xpad_ref[0:8, :] = jnp.zeros((8, SSD_XBC), F32)
        st_ref[...] = jnp.zeros(st_ref.shape, F32)

    L = SSD_CHUNK
    z = u_ref[:, 0:SSD_DI]
    conv = _causal_conv(xpad_ref, u_ref[:, SSD_DI:SSD_DI + SSD_XBC], cw_ref, cb_ref, t)
    xbc = conv * _sigmoid(conv)
    dt = _softplus(u_ref[:, SSD_DI + SSD_XBC:SSD_IN_W] + dtb_ref[...])
    a = dt * a_ref[...]
    row = lax.broadcasted_iota(jnp.int32, (t, 128), 0)
    rin = jnp.bitwise_and(row, L - 1)
    cs = a
    s = 1
    while s < L:
        cs = cs + jnp.where(rin >= s, pltpu.roll(cs, s, axis=0), 0.0)
        s *= 2
    tril = lax.broadcasted_iota(jnp.int32, (L, L), 0) >= lax.broadcasted_iota(jnp.int32, (L, L), 1)
    gn = SSD_GROUPS * SSD_STATE
    rep = SSD_HEADS // SSD_GROUPS
    for c in range(t // L):
        rs = slice(c * L, (c + 1) * L)
        cs_c = cs[rs]
        cs_t = cs_c.T
        cs_last = cs_c[L - 1:L, :]
        ys = []
        for g in range(SSD_GROUPS):
            bg = xbc[rs, SSD_DI + g * SSD_STATE:SSD_DI + (g + 1) * SSD_STATE]
            cg = xbc[rs, SSD_DI + gn + g * SSD_STATE:SSD_DI + gn + (g + 1) * SSD_STATE]
            gmat = _dot_nt(cg, bg)
            bg_t = bg.T
            for hh in range(rep):
                h = g * rep + hh
                cs_col = cs_c[:, h:h + 1]
                lm = jnp.exp(jnp.where(tril, cs_col - cs_t[h:h + 1, :], NEG))
                xh = xbc[rs, h * SSD_HD:(h + 1) * SSD_HD]
                xdt = xh * dt[rs, h:h + 1]
                st_old = st_ref[h]
                y_h = _dot(gmat * lm, xdt) + _dot(cg, st_old) * jnp.exp(cs_col)
                dec = jnp.exp(cs_last[:, h:h + 1] - cs_col)
                st_ref[h] = jnp.exp(cs_last[:, h:h + 1]) * st_old + _dot(bg_t, xdt * dec)
                ys.append(y_h)
        y = jnp.concatenate(ys, axis=-1) + xbc[rs, 0:SSD_DI] * d_ref[...]
        zc = z[rs]
        y = y * (zc * _sigmoid(zc))
        y_ref[rs, :] = _rms(y, SSD_DI) * ng_ref[...]


def _ssd(ud, cw, cb, dtb, a_neg, d_vec, ng, B, S):
    t = 512
    nb = S // t
    full = lambda a: pl.BlockSpec(a.shape, lambda b, i: (0,) * a.ndim)
    return pl.pallas_call(
        functools.partial(_ssd_kernel, t=t),
        grid=(B, nb),
        in_specs=[pl.BlockSpec((t, SSD_IN_W), lambda b, i: (b * nb + i, 0)),
                  full(cw), full(cb), full(dtb), full(a_neg), full(d_vec), full(ng)],
        out_specs=pl.BlockSpec((t, SSD_DI), lambda b, i: (b * nb + i, 0)),
        out_shape=jax.ShapeDtypeStruct((B * S, SSD_DI), F32),
        scratch_shapes=[pltpu.VMEM((t + 8, SSD_XBC), F32), pltpu.VMEM((SSD_HEADS, SSD_STATE, SSD_HD), F32)],
        compiler_params=_cparams(("parallel", "arbitrary")),
        name="ssd",
    )(ud, cw, cb, dtb, a_neg, d_vec, ng)


def _outproj_kernel(*refs, with_router):
    if with_router:
        (ya_ref, oc_ref, os_ref, ow_ref, gate_ref, yc_ref, yd_ref, res_ref, gn_ref, ex_ref, w_ref, nf_ref,
         rw_ref, hres_ref, hn_ref, rg_ref) = refs
    else:
        (ya_ref, oc_ref, os_ref, ow_ref, gate_ref, yc_ref, yd_ref, res_ref, gn_ref, ex_ref, w_ref, nf_ref,
         hres_ref, hn_ref) = refs
    sg_hi, sg_lo = _split2(_sigmoid(gate_ref[...]))
    ex = ex_ref[...]
    gx = jnp.dot(sg_hi, ex, preferred_element_type=F32) + jnp.dot(sg_lo, ex, preferred_element_type=F32)
    yb = gx[:, 0:256] * oc_ref[...] + gx[:, 256:512] * os_ref[...] + gx[:, 512:768] * ow_ref[...]
    y = jnp.concatenate([_rms(ya_ref[...], GROUP_W) * gn_ref[0:1, :],
                         _rms(yb, GROUP_W) * gn_ref[1:2, :],
                         _rms(yc_ref[...], GROUP_W) * gn_ref[2:3, :],
                         yd_ref[...]], axis=-1)
    hres = res_ref[...] + _dot(y, w_ref[...])
    hres_ref[...] = hres
    hn = _rms(hres, D_MODEL) * nf_ref[...]
    hn_ref[...] = hn.astype(hn_ref.dtype)
    if with_router:
        h_hi, h_lo = _split2(hn)
        logits = (jnp.dot(h_hi, rw_ref[0], preferred_element_type=F32)
                  + jnp.dot(h_hi, rw_ref[1], preferred_element_type=F32)
                  + jnp.dot(h_lo, rw_ref[0], preferred_element_type=F32))
        lane = lax.broadcasted_iota(jnp.int32, logits.shape, 1)
        lf = lane.astype(F32)
        logits = jnp.where(lane < N_EXPERTS, logits, NEG)
        m1 = jnp.max(logits, axis=-1, keepdims=True)
        i1 = jnp.min(jnp.where(logits == m1, lf, 128.0), axis=-1, keepdims=True)
        rest = jnp.where(lf == i1, NEG, logits)
        m2 = jnp.max(rest, axis=-1, keepdims=True)
        i2 = jnp.min(jnp.where(rest == m2, lf, 128.0), axis=-1, keepdims=True)
        e2 = jnp.exp(m2 - m1)
        den = 1.0 + e2
        rg_ref[...] = jnp.where(lane == 0, i1, jnp.where(lane == 1, i2, jnp.where(
            lane == 2, 1.0 / den, jnp.where(lane == 3, e2 / den, 0.0))))


def _outproj(ya, oc, osel, ow, gate, yc, yd, res, gn, ex, w, nf, rw):
    T = ya.shape[0]
    tm = 512
    with_router = rw is not None
    row = lambda wd: pl.BlockSpec((tm, wd), lambda i: (i, 0))
    full = lambda a: pl.BlockSpec(a.shape, lambda i: (0,) * a.ndim)
    ins = [ya, oc, osel, ow, gate, yc, yd, res, gn, ex, w, nf]
    in_specs = [row(256), row(256), row(256), row(256), row(128), row(256), row(256), row(D_MODEL),
                full(gn), full(ex), full(w), full(nf)]
    out_specs = [row(D_MODEL), row(D_MODEL)]
    out_shape = [jax.ShapeDtypeStruct((T, D_MODEL), F32),
                 jax.ShapeDtypeStruct((T, D_MODEL), F32 if with_router else BF16)]
    if with_router:
        ins.append(rw)
        in_specs.append(full(rw))
        out_specs.append(row(128))
        out_shape.append(jax.ShapeDtypeStruct((T, 128), F32))
    return pl.pallas_call(
        functools.partial(_outproj_kernel, with_router=with_router),
        grid=(T // tm,),
        in_specs=in_specs,
        out_specs=out_specs,
        out_shape=out_shape,
        compiler_params=_cparams(("parallel",)),
        name="outproj",
    )(*ins)


def _ffn_kernel(h_ref, res_ref, wg_ref, wu_ref, wd_ref, o_ref):
    j = pl.program_id(1)

    @pl.when(j == 0)
    def _init():
        o_ref[...] = res_ref[...]

    h = h_ref[...]
    g = jnp.dot(h, wg_ref[...], preferred_element_type=F32)
    u = jnp.dot(h, wu_ref[...], preferred_element_type=F32)
    o_ref[...] += _dot(g * _sigmoid(g) * u, wd_ref[...])


def _ffn(hn, res, wg, wu, wd):
    T = hn.shape[0]
    tm, tf = 1024, 512
    return pl.pallas_call(
        _ffn_kernel,
        grid=(T // tm, D_FF // tf),
        in_specs=[pl.BlockSpec((tm, D_MODEL), lambda i, j: (i, 0)),
                  pl.BlockSpec((tm, D_MODEL), lambda i, j: (i, 0)),
                  pl.BlockSpec((D_MODEL, tf), lambda i, j: (0, j)),
                  pl.BlockSpec((D_MODEL, tf), lambda i, j: (0, j)),
                  pl.BlockSpec((tf, D_MODEL), lambda i, j: (j, 0))],
        out_specs=pl.BlockSpec((tm, D_MODEL), lambda i, j: (i, 0)),
        out_shape=jax.ShapeDtypeStruct((T, D_MODEL), F32),
        compiler_params=_cparams(("parallel", "arbitrary")),
        name="ffn_dense",
    )(hn, res, wg, wu, wd)


MOE_TILE = 1024
MOE_TM = 512


def _route_rank_kernel(route_ref, rank_ref, cnt_ref, carry_ref, *, tm):
    @pl.when(pl.program_id(0) == 0)
    def _init():
        carry_ref[...] = jnp.zeros(carry_ref.shape, F32)

    r = route_ref[...]
    i1 = r[:, 0:1]
    i2 = r[:, 1:2]
    lane = lax.broadcasted_iota(jnp.int32, (tm, 128), 1)
    lf = lane.astype(F32)
    oh = jnp.where((lf == i1) | (lf == i2), 1.0, 0.0)
    row = lax.broadcasted_iota(jnp.int32, (tm, 128), 0)
    cs = oh
    s = 1
    while s < tm:
        cs = cs + jnp.where(row >= s, pltpu.roll(cs, s, axis=0), 0.0)
        s *= 2
    excl = cs - oh + carry_ref[0:1, :]
    rank1 = jnp.sum(jnp.where(lf == i1, excl, 0.0), axis=-1, keepdims=True)
    rank2 = jnp.sum(jnp.where(lf == i2, excl, 0.0), axis=-1, keepdims=True)
    rank_ref[...] = jnp.where(lane == 0, rank1, jnp.where(lane == 1, rank2, 0.0))
    carry_ref[0:1, :] = carry_ref[0:1, :] + cs[tm - 1:tm, :]
    cnt_ref[...] = carry_ref[...]


def _route_rank(route):
    T = route.shape[0]
    tm = MOE_TM
    return pl.pallas_call(
        functools.partial(_route_rank_kernel, tm=tm),
        grid=(T // tm,),
        in_specs=[pl.BlockSpec((tm, 128), lambda i: (i, 0))],
        out_specs=[pl.BlockSpec((tm, 128), lambda i: (i, 0)), pl.BlockSpec((8, 128), lambda i: (0, 0))],
        out_shape=[jax.ShapeDtypeStruct((T, 128), F32), jax.ShapeDtypeStruct((8, 128), F32)],
        scratch_shapes=[pltpu.VMEM((8, 128), F32)],
        compiler_params=_cparams(("arbitrary",)),
        name="moe_rank",
    )(route)


def _row_copies(idx_ref, tm, make_copy):
    def body(r, carry):
        for k in range(2):
            make_copy(k, r, idx_ref[k * tm + r]).start(priority=k)
        return carry

    lax.fori_loop(0, tm, body, 0, unroll=8)


def _dispatch_kernel(ends_ref, slots_ref, h_ref, xs_ref, idx_ref, zero_ref, stage_ref, sem_idx, sem_row, *, tm):
    i = pl.program_id(0)

    @pl.when(i == 0)
    def _zero_tiles():
        zero_ref[...] = jnp.zeros(zero_ref.shape, F32)
        zr = zero_ref.shape[0]
        jobs = []
        for e in range(N_EXPERTS):
            start = ends_ref[e - 1] if e else 0
            jobs.append((ends_ref[e] > start, ends_ref[e] - MOE_TILE))
        for tile in range(N_EXPERTS):
            base = xs_ref.shape[0] - (tile + 1) * MOE_TILE
            jobs.append((base >= ends_ref[N_EXPERTS - 1], base))

        def tile_copies(base):
            rows = [base + q * zr for q in range(MOE_TILE // zr)]
            rows = [r if isinstance(r, int) else pl.multiple_of(r, zr) for r in rows]
            return [pltpu.make_async_copy(zero_ref, xs_ref.at[pl.ds(r, zr)], sem_idx) for r in rows]

        for cond, base in jobs:
            @pl.when(cond)
            def _start(base=base):
                for cp in tile_copies(base):
                    cp.start()
        for cond, base in jobs:
            @pl.when(cond)
            def _wait(base=base):
                for cp in tile_copies(base):
                    cp.wait()

    cp = pltpu.make_async_copy(slots_ref.at[pl.ds(i * 2 * tm, 2 * tm)], idx_ref, sem_idx)
    cp.start()
    cp.wait()
    cur = lax.rem(i, 2)

    def wait_tile(b):
        for _ in range(2):
            pltpu.make_async_copy(stage_ref.at[b], xs_ref.at[pl.ds(0, tm)], sem_row.at[b]).wait()

    @pl.when(i > 1)
    def _reuse():
        wait_tile(cur)

    stage_ref[cur] = h_ref[...]
    _row_copies(idx_ref, tm, lambda k, r, s: pltpu.make_async_copy(
        stage_ref.at[cur, pl.ds(r, 1)], xs_ref.at[pl.ds(s, 1)], sem_row.at[cur]))

    @pl.when(i == pl.num_programs(0) - 1)
    def _drain():
        @pl.when(i > 0)
        def _prev():
            wait_tile(1 - cur)
        wait_tile(cur)


def _dispatch(ends, slots, hn, n_slots):
    T = hn.shape[0]
    tm = MOE_TM
    grid_spec = pltpu.PrefetchScalarGridSpec(
        num_scalar_prefetch=1,
        grid=(T // tm,),
        in_specs=[pl.BlockSpec(memory_space=pl.ANY),
                  pl.BlockSpec((tm, D_MODEL), lambda i, ends: (i, 0))],
        out_specs=pl.BlockSpec(memory_space=pl.ANY),
        scratch_shapes=[pltpu.SMEM((2 * tm,), jnp.int32), pltpu.VMEM((256, D_MODEL), F32),
                        pltpu.VMEM((2, tm, D_MODEL), F32),
                        pltpu.SemaphoreType.DMA, pltpu.SemaphoreType.DMA((2,))],
    )
    return pl.pallas_call(
        functools.partial(_dispatch_kernel, tm=tm),
        grid_spec=grid_spec,
        out_shape=jax.ShapeDtypeStruct((n_slots, D_MODEL), F32),
        compiler_params=_cparams(("arbitrary",)),
        name="moe_dispatch",
    )(ends, slots, hn)


def _expert_ffn_kernel(te_ref, nu_ref, x_ref, wg_ref, wu_ref, wd_ref, o_ref, xb_ref):
    del te_ref
    i = pl.program_id(0)
    j = pl.program_id(1)
    used = i < nu_ref[0]

    @pl.when(j == 0)
    def _init():
        o_ref[...] = jnp.zeros(o_ref.shape, F32)
        xb_ref[...] = x_ref[...].astype(BF16)

    @pl.when(used)
    def _step():
        x = xb_ref[...]
        g = _dot(x, wg_ref[...])
        u = _dot(x, wu_ref[...])
        o_ref[...] += _dot(g * _sigmoid(g) * u, wd_ref[...])


def _expert_ffn(tile_expert, n_used, xs, wg, wu, wd):
    n_slots = xs.shape[0]
    tm, tf = MOE_TILE, 512
    jw = lambda i, j, nu: jnp.where(i < nu[0], j, 0)
    grid_spec = pltpu.PrefetchScalarGridSpec(
        num_scalar_prefetch=2,
        grid=(n_slots // tm, D_FF // tf),
        in_specs=[pl.BlockSpec((tm, D_MODEL), lambda i, j, te, nu: (jnp.minimum(i, nu[0] - 1), 0)),
                  pl.BlockSpec((None, D_MODEL, tf), lambda i, j, te, nu: (te[i], 0, jw(i, j, nu))),
                  pl.BlockSpec((None, D_MODEL, tf), lambda i, j, te, nu: (te[i], 0, jw(i, j, nu))),
                  pl.BlockSpec((None, tf, D_MODEL), lambda i, j, te, nu: (te[i], jw(i, j, nu), 0))],
        out_specs=pl.BlockSpec((tm, D_MODEL), lambda i, j, te, nu: (i, 0)),
        scratch_shapes=[pltpu.VMEM((tm, D_MODEL), BF16)],
    )
    return pl.pallas_call(
        _expert_ffn_kernel,
        grid_spec=grid_spec,
        out_shape=jax.ShapeDtypeStruct((n_slots, D_MODEL), F32),
        compiler_params=_cparams(("parallel", "arbitrary")),
        name="moe_expert_ffn",
    )(tile_expert, n_used, xs, wg, wu, wd)


def _combine_kernel(slots_ref, ys_ref, route_ref, res_ref, ng_ref, o_ref, idx_ref, buf_ref, sem_idx, sem_row,
                    *, tm, final_norm):
    i = pl.program_id(0)
    cur = lax.rem(i, 2)

    def gather(step, b):
        cp = pltpu.make_async_copy(slots_ref.at[pl.ds(step * 2 * tm, 2 * tm)], idx_ref, sem_idx)
        cp.start()
        cp.wait()
        _row_copies(idx_ref, tm, lambda k, r, s: pltpu.make_async_copy(
            ys_ref.at[pl.ds(s, 1)], buf_ref.at[b, k, pl.ds(r, 1)], sem_row.at[b]))

    @pl.when(i == 0)
    def _first():
        gather(0, 0)

    @pl.when(i + 1 < pl.num_programs(0))
    def _next():
        gather(i + 1, 1 - cur)

    for k in range(2):
        pltpu.make_async_copy(ys_ref.at[pl.ds(0, tm)], buf_ref.at[cur, k], sem_row.at[cur]).wait()
    route = route_ref[...]
    out = res_ref[...] + route[:, 2:3] * buf_ref[cur, 0] + route[:, 3:4] * buf_ref[cur, 1]
    if final_norm:
        out = _rms(out, D_MODEL) * ng_ref[...]
    o_ref[...] = out


def _combine(slots, ys, route, res, norm_g, final_norm):
    T = res.shape[0]
    tm = MOE_TM
    return pl.pallas_call(
        functools.partial(_combine_kernel, tm=tm, final_norm=final_norm),
        grid=(T // tm,),
        in_specs=[pl.BlockSpec(memory_space=pl.ANY),
                  pl.BlockSpec(memory_space=pl.ANY),
                  pl.BlockSpec((tm, 128), lambda i: (i, 0)),
                  pl.BlockSpec((tm, D_MODEL), lambda i: (i, 0)),
                  pl.BlockSpec((1, D_MODEL), lambda i: (0, 0))],
        out_specs=pl.BlockSpec((tm, D_MODEL), lambda i: (i, 0)),
        out_shape=jax.ShapeDtypeStruct((T, D_MODEL), F32),
        scratch_shapes=[pltpu.SMEM((2 * tm,), jnp.int32), pltpu.VMEM((2, 2, tm, D_MODEL), F32),
                        pltpu.SemaphoreType.DMA, pltpu.SemaphoreType.DMA((2,))],
        compiler_params=_cparams(("arbitrary",)),
        name="moe_combine",
    )(slots, ys, route, res, norm_g)


def _moe(hn, res, route, wg, wu, wd, norm_g, final_norm):
    T = hn.shape[0]
    n_slots = 2 * T + N_EXPERTS * MOE_TILE
    n_tiles = n_slots // MOE_TILE
    rank, cnt = _route_rank(route)
    counts = cnt[0, :N_EXPERTS].astype(jnp.int32)
    padded = (counts + MOE_TILE - 1) // MOE_TILE * MOE_TILE
    ends = jnp.cumsum(padded)
    starts = ends - padded
    ids = route[:, 0:2].astype(jnp.int32)
    slot = jnp.take(starts, ids) + rank[:, 0:2].astype(jnp.int32)
    slots = slot.reshape(T // MOE_TM, MOE_TM, 2).transpose(0, 2, 1).reshape(-1)
    tile_start = jnp.arange(n_tiles, dtype=jnp.int32) * MOE_TILE
    tile_expert = jnp.minimum(jnp.sum(tile_start[:, None] >= ends[None, :], axis=1), N_EXPERTS - 1).astype(jnp.int32)
    n_used = (ends[-1:] // MOE_TILE).astype(jnp.int32)
    xs = _dispatch(ends.astype(jnp.int32), slots, hn, n_slots)
    ys = _expert_ffn(tile_expert, n_used, xs, wg, wu, wd)
    return _combine(slots, ys, route, res, norm_g, final_norm)


def _final_norm_kernel(x_ref, g_ref, o_ref):
    o_ref[...] = _rms(x_ref[...], D_MODEL) * g_ref[...]


def _final_norm(x2d, g):
    T = x2d.shape[0]
    tm = 1024
    return pl.pallas_call(
        _final_norm_kernel,
        grid=(T // tm,),
        in_specs=[pl.BlockSpec((tm, D_MODEL), lambda i: (i, 0)), pl.BlockSpec(g.shape, lambda i: (0, 0))],
        out_specs=pl.BlockSpec((tm, D_MODEL), lambda i: (i, 0)),
        out_shape=jax.ShapeDtypeStruct((T, D_MODEL), F32),
        compiler_params=_cparams(("parallel",)),
        name="final_norm",
    )(x2d, g)


def _rot_cols(w, half):
    return jnp.concatenate([-w[:, half:2 * half], w[:, 0:half]], axis=1)


def _aligned_w_in(w):
    z = lambda n: jnp.zeros((w.shape[0], n), F32)
    return jnp.concatenate([w[:, 0:O_NSA], z(P_NSA - O_NSA),
                            w[:, O_NSA:O_LRU], z(P_LRU - P_NSA - (O_LRU - O_NSA)),
                            w[:, O_LRU:], z(W_RAW - P_LRU - (w.shape[1] - O_LRU))], axis=1).astype(BF16)


def _rope_tables(positions):
    pos = positions.astype(F32)[:, None]
    S = positions.shape[0]

    def cs(rot_dim):
        inv = ROPE_THETA ** (-jnp.arange(0, rot_dim, 2, dtype=F32) / rot_dim)
        ang = pos * inv[None, :]
        return jnp.cos(ang), jnp.sin(ang)

    cm, sm = cs(MLA_ROPE)
    cn, sn = cs(NSA_ROT)
    hm, hn = MLA_ROPE // 2, NSA_ROT // 2
    base = jnp.concatenate([cm, sm, cn, sn, jnp.ones((S, 1), F32)], axis=1)
    c_cm, c_sm, c_cn, c_sn, c_one = 0, hm, 2 * hm, 2 * hm + hn, 2 * hm + 2 * hn

    place = np.zeros((base.shape[1], 6 * 128), np.float32)

    def put(tile, lane0, col0, n, sign=1.0):
        for j in range(n):
            place[col0 + j, tile * 128 + lane0 + j] = sign

    def ones(tile, lane0, n):
        place[c_one, tile * 128 + lane0:tile * 128 + lane0 + n] = 1.0

    ones(0, 0, 64), put(0, 64, c_cm, hm), put(0, 64 + hm, c_cm, hm), ones(0, 96, 32)
    put(1, 64 + hm, c_sm, hm)
    put(2, 64, c_sm, hm, -1.0)
    for g in (0, 64):
        put(3, g, c_cn, hn), put(3, g + hn, c_cn, hn), ones(3, g + 2 * hn, NSA_HD - 2 * hn)
        put(4, g + hn, c_sn, hn)
        put(5, g, c_sn, hn, -1.0)
    return jnp.dot(base, jnp.asarray(place), precision=lax.Precision.HIGHEST)


def _mla_weights(w_uq, w_ukv, q_norm, kv_norm):
    hw = MLA_NOPE + MLA_ROPE
    pad_r = lambda m: jnp.concatenate([m, jnp.zeros((256 - MLA_Q_RANK, m.shape[1]), F32)], axis=0)
    z = lambda n: jnp.zeros((MLA_Q_RANK, n), F32)
    zk = lambda n: jnp.zeros((MLA_KV_RANK, n), F32)
    w1, w2, wk, wv = [], [], [], []
    for h in range(MLA_HEADS):
        qh = w_uq[:, h * hw:(h + 1) * hw]
        w1 += [qh, z(128 - hw)]
        w2 += [z(MLA_NOPE), _rot_cols(qh[:, MLA_NOPE:], MLA_ROPE // 2), z(128 - hw)]
        kvh = w_ukv[:, h * (MLA_NOPE + MLA_V):(h + 1) * (MLA_NOPE + MLA_V)]
        wk += [kvh[:, :MLA_NOPE], zk(128 - MLA_NOPE)]
        wv += [kvh[:, MLA_NOPE:], zk(128 - MLA_V)]
    qg = jnp.concatenate([q_norm, jnp.zeros((256 - MLA_Q_RANK,), F32)])[None, :]
    return (qg, pad_r(jnp.concatenate(w1, axis=1)).astype(BF16), pad_r(jnp.concatenate(w2, axis=1)).astype(BF16),
            kv_norm[None, :], jnp.concatenate(wk, axis=1).astype(BF16), jnp.concatenate(wv, axis=1).astype(BF16))


def _compress_weights(pe, w_cmp):
    half = CMP_LEN // 2
    wk = w_cmp[0].reshape(CMP_LEN, NSA_HD, NSA_HD)
    wv = w_cmp[1].reshape(CMP_LEN, NSA_HD, NSA_HD)
    z = jnp.zeros((half, NSA_HD, NSA_HD), F32)

    def interleave(ks, vs):
        top = jnp.concatenate([ks, z], axis=-1)
        bot = jnp.concatenate([z, vs], axis=-1)
        return jnp.concatenate([top, bot], axis=1).reshape(half * 2 * NSA_HD, 2 * NSA_HD)

    pad8 = lambda p: jnp.concatenate([p.reshape(1, -1), jnp.zeros((7, CMP_LEN * NSA_HD), F32)], axis=0)
    return (interleave(wk[:half], wv[:half]), interleave(wk[half:], wv[half:]),
            pad8(pe[0]), pad8(pe[1]), w_cmp[0], w_cmp[1])


def _overlap_matrix(S):
    nc_pad = S // CMP_STRIDE
    n = np.arange(nc_pad)[:, None]
    j = np.arange(SEL_LANES)[None, :]
    ov = ((n * CMP_STRIDE <= j * SEL_LEN + SEL_LEN - 1) & (n * CMP_STRIDE + CMP_LEN - 1 >= j * SEL_LEN)
          & (n < nc_pad - 1) & (j < S // SEL_LEN))
    return jnp.asarray(ov.astype(np.float32)).astype(BF16)


def _gate_expand():
    gw = NSA_HEADS * NSA_HD
    ex = np.zeros((128, 3 * gw), np.float32)
    for br in range(3):
        for h in range(NSA_HEADS):
            ex[h * 3 + br, br * gw + h * NSA_HD:br * gw + (h + 1) * NSA_HD] = 1.0
    return jnp.asarray(ex).astype(BF16)


def _lru_gate_weights(w_gate, b_gate):
    wg = jnp.zeros((LRU_W, 2 * LRU_W), F32)
    bw = LRU_W // LRU_BLOCKS
    for g in range(2):
        for n in range(LRU_BLOCKS):
            wg = wg.at[n * bw:(n + 1) * bw, g * LRU_W + n * bw:g * LRU_W + (n + 1) * bw].set(w_gate[g, n])
    return wg.astype(BF16), b_gate.reshape(1, 2 * LRU_W)


def _pad_lanes(v, n=128):
    return jnp.concatenate([v, jnp.zeros((n - v.shape[0],), F32)])[None, :]


def kernel(x, positions, norm_mix, w_in, mla_q_norm, mla_w_uq, mla_kv_norm, mla_w_ukv, nsa_cmp_pe, nsa_w_cmp,
           lru_conv_w, lru_conv_b, lru_w_gate, lru_b_gate, lru_lambda, ssd_conv_w, ssd_conv_b, ssd_dt_bias,
           ssd_a_log, ssd_d, group_norm, w_out, norm_ffn, ffn_w_gate, ffn_w_up, ffn_w_down, moe_router,
           moe_w_gate, moe_w_up, moe_w_down, norm_final):
    B, S, D = x.shape
    T = B * S
    depth = w_in.shape[0]
    assert S // SEL_LEN <= SEL_LANES and S % 512 == 0
    rope_tab = _rope_tables(positions)
    ov = _overlap_matrix(S)
    ex = _gate_expand()
    h_res = x.reshape(T, D)
    for l in range(depth):
        w_raw = _aligned_w_in(w_in[l])
        ua, qn, kvc, ksel, kwin, vsel, vwin, gate, uc, ud = _inproj(
            h_res, norm_mix[l][None, :], w_raw, rope_tab, S)

        q_m, k_m, v_m = _mla_prep(ua, *_mla_weights(mla_w_uq[l], mla_w_ukv[l], mla_q_norm[l], mla_kv_norm[l]),
                                  rope_tab, B, S)
        y_a = _flash(q_m, k_m, v_m, window=None, tq=1024).reshape(T, GROUP_W)

        kvcmp = _compress(kvc.reshape(B, S // CMP_STRIDE, CMP_STRIDE * 128),
                          *_compress_weights(nsa_cmp_pe[l], nsa_w_cmp[l]))
        o_c, q_aug = _cmp_select(qn, kvcmp, ov, B, S)
        o_s = _flash(q_aug, ksel.reshape(B, 1, S, 256), vsel.reshape(B, 1, S, 128), window=None,
                     tq=1024).reshape(T, GROUP_W)
        o_w = _flash(q_aug, kwin.reshape(B, 1, S, 256), vwin.reshape(B, 1, S, 128), window=WINDOW,
                     tq=512).reshape(T, GROUP_W)

        wg_l, bg_l = _lru_gate_weights(lru_w_gate[l], lru_b_gate[l])
        y_c = _lru(uc, lru_conv_w[l], lru_conv_b[l][None, :], wg_l, bg_l,
                   jax.nn.softplus(-lru_lambda[l])[None, :], B, S)

        y_d = _ssd(ud, ssd_conv_w[l], ssd_conv_b[l][None, :], _pad_lanes(ssd_dt_bias[l]),
                   _pad_lanes(-jnp.exp(ssd_a_log[l])), jnp.repeat(ssd_d[l], SSD_HD)[None, :],
                   group_norm[l, 3][None, :], B, S)

        moe_layer = l % 2 == 1
        rw = None
        if moe_layer:
            rw = jnp.concatenate([moe_router[l // 2], jnp.zeros((D, 128 - N_EXPERTS), F32)], axis=1)
            rw = jnp.stack(_split2(rw))
        outs = _outproj(y_a, o_c, o_s, o_w, gate, y_c, y_d, h_res, group_norm[l], ex, w_out[l].astype(BF16),
                        norm_ffn[l][None, :], rw)
        if moe_layer:
            h_res, hn, rg = outs
            normed = l == depth - 1
            h_res = _moe(hn, h_res, rg, moe_w_gate[l // 2], moe_w_up[l // 2], moe_w_down[l // 2],
                         norm_final[None, :], normed)
        else:
            normed = False
            h_res, hn = outs
            h_res = _ffn(hn, h_res, ffn_w_gate[l // 2].astype(BF16), ffn_w_up[l // 2].astype(BF16),
                         ffn_w_down[l // 2].astype(BF16))
    if not normed:
        h_res = _final_norm(h_res, norm_final[None, :])
    return h_res.reshape(B, S, D).astype(x.dtype)
```

```python
import functools

import numpy as np
import jax
import jax.numpy as jnp
from jax import lax
from jax.experimental import pallas as pl
from jax.experimental.pallas import tpu as pltpu

F32 = jnp.float32
BF16 = jnp.bfloat16

D_MODEL = 1024
GROUP_W = 256
ROPE_THETA = 500000.0
NORM_EPS = 1e-6
NEG = -1e30
FORCE = 1e4
MLA_HEADS, MLA_NOPE, MLA_ROPE, MLA_V = 4, 64, 32, 64
MLA_Q_RANK, MLA_KV_RANK = 192, 128
NSA_HEADS, NSA_HD, NSA_ROT = 4, 64, 16
CMP_STRIDE, CMP_LEN, SEL_LEN, SEL_TOPN, WINDOW = 16, 32, 64, 16, 512
LRU_W, LRU_BLOCKS, LRU_C, CONV_W = 256, 4, 8.0, 4
SSD_HEADS, SSD_HD, SSD_GROUPS, SSD_STATE, SSD_CHUNK = 4, 64, 2, 128, 128
SSD_DI = 256
SSD_XBC = SSD_DI + 2 * SSD_GROUPS * SSD_STATE
D_FF = 3584
N_EXPERTS = 8

O_NSA = MLA_Q_RANK + MLA_KV_RANK + MLA_ROPE
O_LRU = O_NSA + NSA_HEADS * NSA_HD + 6 * NSA_HD + 3 * NSA_HEADS
O_SSD = O_LRU + 2 * LRU_W
SSD_IN_W = 1152
P_NSA, P_LRU, W_RAW = 384, 1152, 2816
LOG2E = 1.4426950408889634
SEL_LANES = 128

VMEM_LIMIT = 56 * 1024 * 1024


def _cparams(sem):
    return pltpu.CompilerParams(dimension_semantics=sem, vmem_limit_bytes=VMEM_LIMIT)


def _dot(a, b):
    return jnp.dot(a.astype(BF16), b.astype(BF16), preferred_element_type=F32)


def _dot_nt(a, b):
    return lax.dot_general(a.astype(BF16), b.astype(BF16), (((1,), (1,)), ((), ())),
                           preferred_element_type=F32)


def _dot_f32(a, b):
    return jnp.dot(a, b, precision=lax.Precision.HIGHEST, preferred_element_type=F32)


def _dot_f32_nt(a, b):
    return lax.dot_general(a, b, (((1,), (1,)), ((), ())), precision=lax.Precision.HIGHEST,
                           preferred_element_type=F32)


def _split2(x):
    hi = x.astype(BF16)
    return hi, (x - hi.astype(F32)).astype(BF16)


def _sigmoid(x):
    return 1.0 / (1.0 + jnp.exp(-x))


def _softplus(x):
    return jnp.maximum(x, 0.0) + jnp.log(1.0 + jnp.exp(-jnp.abs(x)))


def _rms(x, width):
    return x * lax.rsqrt(jnp.sum(x * x, axis=-1, keepdims=True) * (1.0 / width) + NORM_EPS)


def _rope_tile(x, tab_ref, slot, half):
    t0 = 3 * 128 * slot
    return (x * tab_ref[:, t0:t0 + 128] + pltpu.roll(x, half, axis=1) * tab_ref[:, t0 + 128:t0 + 256]
            + pltpu.roll(x, 128 - half, axis=1) * tab_ref[:, t0 + 256:t0 + 384])


def _inproj_kernel(x_ref, g_ref, w_ref, tab_ref,
                   ua_ref, qn_ref, kvc_ref, ksel_ref, kwin_ref, vsel_ref, vwin_ref, gate_ref, uc_ref, ud_ref,
                   *, tm, seq_blocks):
    x = x_ref[...]
    h = _rms(x, D_MODEL) * g_ref[...]
    y = _dot(h, w_ref[...])
    lane = lax.broadcasted_iota(jnp.int32, (tm, 128), 1)
    lo64 = lane < 64
    ua_ref[:, 0:128] = y[:, 0:128]
    ua_ref[:, 128:256] = jnp.where(lo64, y[:, 128:256], 0.0)
    ua_ref[:, 256:384] = y[:, MLA_Q_RANK:MLA_Q_RANK + MLA_KV_RANK]
    kpe = jnp.where((lane >= 64) & (lane < 64 + MLA_ROPE), y[:, 256:384], 0.0)
    ua_ref[:, 384:512] = _rope_tile(kpe, tab_ref, 0, MLA_ROPE // 2)
    yn = y[:, P_NSA:P_NSA + 768]
    qn_ref[:, 0:128] = _rope_tile(yn[:, 0:128], tab_ref, 1, NSA_ROT // 2)
    qn_ref[:, 128:256] = _rope_tile(yn[:, 128:256], tab_ref, 1, NSA_ROT // 2)
    kvc_ref[...] = jnp.where(lo64, _rope_tile(yn[:, 256:384], tab_ref, 1, NSA_ROT // 2), yn[:, 256:384])
    s0 = (pl.program_id(0) % seq_blocks) * tm
    pos = s0 + lax.broadcasted_iota(jnp.int32, (tm, SEL_LANES), 0)
    onehot = jnp.where(jnp.right_shift(pos, 6) == lane, 1.0, 0.0)
    for k_ref, v_ref, c0, extra in ((ksel_ref, vsel_ref, 384, onehot), (kwin_ref, vwin_ref, 512, None)):
        kv = yn[:, c0:c0 + 128]
        k_ref[:, 0:128] = jnp.where(lo64, _rope_tile(kv, tab_ref, 1, NSA_ROT // 2), 0.0).astype(BF16)
        k_ref[:, 128:256] = (jnp.zeros((tm, 128), F32) if extra is None else extra).astype(BF16)
        v_ref[...] = jnp.where(lo64, pltpu.roll(kv, 64, axis=1), 1.0).astype(BF16)
    gate_ref[...] = jnp.where(lane < 3 * NSA_HEADS, yn[:, 640:768], 0.0)
    yl = y[:, P_LRU:P_LRU + 512 + SSD_IN_W]
    uc_ref[...] = yl[:, 0:512]
    ud_ref[:, 0:SSD_IN_W - 128] = yl[:, 512:512 + SSD_IN_W - 128]
    ud_ref[:, SSD_IN_W - 128:SSD_IN_W] = jnp.where(lane < SSD_HEADS, yl[:, 512 + SSD_IN_W - 128:512 + SSD_IN_W], 0.0)


def _inproj(x2d, g, w_raw, tab, seq):
    T = x2d.shape[0]
    tm = 512
    seq_blocks = seq // tm
    row = lambda w: pl.BlockSpec((tm, w), lambda i: (i, 0))
    full = lambda a: pl.BlockSpec(a.shape, lambda i: (0,) * a.ndim)
    outs = [(512, F32), (256, F32), (128, F32), (256, BF16), (256, BF16), (128, BF16), (128, BF16), (128, F32),
            (512, F32), (SSD_IN_W, F32)]
    return pl.pallas_call(
        functools.partial(_inproj_kernel, tm=tm, seq_blocks=seq_blocks),
        grid=(T // tm,),
        in_specs=[row(D_MODEL), full(g), full(w_raw), pl.BlockSpec((tm, tab.shape[1]), lambda i: (i % seq_blocks, 0))],
        out_specs=[row(w) for w, _ in outs],
        out_shape=[jax.ShapeDtypeStruct((T, w), dt) for w, dt in outs],
        compiler_params=_cparams(("parallel",)),
        name="inproj",
    )(x2d, g, w_raw, tab)


def _mla_prep_kernel(ua_ref, qg_ref, w1_ref, w2_ref, kvg_ref, wk_ref, wv_ref, tab_ref,
                     q_ref, k_ref, v_ref):
    ua = ua_ref[...]
    cq = _rms(ua[:, 0:256], MLA_Q_RANK) * qg_ref[...]
    y1 = _dot(cq, w1_ref[...])
    y2 = _dot(cq, w2_ref[...])
    ckv = _rms(ua[:, 256:384], MLA_KV_RANK) * kvg_ref[...]
    kn = _dot(ckv, wk_ref[...])
    vv = _dot(ckv, wv_ref[...])
    kpe = ua[:, 384:512]
    cos = tab_ref[:, 0:128]
    sin = tab_ref[:, 128:256] - tab_ref[:, 256:384]
    scale = LOG2E * (MLA_NOPE + MLA_ROPE) ** -0.5
    ones_hi = jnp.where(lax.broadcasted_iota(jnp.int32, cos.shape, 1) >= 64, 1.0, 0.0)
    for h in range(MLA_HEADS):
        sl = slice(h * 128, (h + 1) * 128)
        q_ref[h] = ((y1[:, sl] * cos + y2[:, sl] * sin) * scale).astype(BF16)
        k_ref[h] = (kn[:, sl] + kpe).astype(BF16)
        v_ref[h] = (vv[:, sl] + ones_hi).astype(BF16)


def _mla_prep(ua, qg, w1, w2, kvg, wk, wv, rope_tab, B, S):
    tm = 1024
    nb = S // tm
    full = lambda a: pl.BlockSpec(a.shape, lambda b, i: (0,) * a.ndim)
    tab = pl.BlockSpec((tm, 384), lambda b, i: (i, 0))
    hd = lambda w: pl.BlockSpec((None, MLA_HEADS, tm, w), lambda b, i: (b, 0, i, 0))
    return pl.pallas_call(
        _mla_prep_kernel,
        grid=(B, nb),
        in_specs=[pl.BlockSpec((tm, 512), lambda b, i: (b * nb + i, 0)),
                  full(qg), full(w1), full(w2), full(kvg), full(wk), full(wv), tab],
        out_specs=[hd(128), hd(128), hd(128)],
        out_shape=[jax.ShapeDtypeStruct((B, MLA_HEADS, S, 128), BF16)] * 3,
        compiler_params=_cparams(("parallel", "parallel")),
        name="mla_prep",
    )(ua, qg, w1, w2, kvg, wk, wv, rope_tab)


def _flash_kernel(qi_ref, ki_ref, flag_ref, q_ref, k_ref, v_ref, o_ref, m_ref, acc_ref,
                  *, heads, kv_heads, tq, tk, window):
    p_idx = pl.program_id(1)
    qi = qi_ref[p_idx]
    ki = ki_ref[p_idx]
    flags = flag_ref[p_idx]

    @pl.when(jnp.bitwise_and(flags, 1) != 0)
    def _init():
        m_ref[...] = jnp.full(m_ref.shape, NEG, F32)
        acc_ref[...] = jnp.zeros(acc_ref.shape, F32)

    def step(masked):
        if masked:
            qpos = qi * tq + lax.broadcasted_iota(jnp.int32, (tq, tk), 0)
            kpos = ki * tk + lax.broadcasted_iota(jnp.int32, (tq, tk), 1)
            mask = kpos <= qpos
            if window is not None:
                mask = mask & (kpos > qpos - window)
        nt = (((1,), (1,)), ((), ()))
        shared = kv_heads == 1
        if shared:
            dk = q_ref.shape[-1]
            s_all = lax.dot_general(q_ref[...].reshape(heads * tq, dk), k_ref[0], nt, preferred_element_type=F32)
        ps, alphas = [], []
        for h in range(heads):
            if shared:
                s = s_all[h * tq:(h + 1) * tq]
            else:
                s = lax.dot_general(q_ref[h], k_ref[h], nt, preferred_element_type=F32)
            if masked:
                s = jnp.where(mask, s, NEG)
            m_old = m_ref[h]
            m_new = jnp.maximum(m_old, jnp.max(s, axis=-1, keepdims=True))
            alpha = jnp.exp2(m_old - m_new)
            d = [s[:, c * 128:(c + 1) * 128] - m_new for c in range(tk // 128)]
            if shared:
                p = jnp.concatenate([jnp.exp2(x) for x in d], axis=1).astype(BF16)
            else:
                p = jnp.concatenate([jnp.exp2(x.astype(BF16)) for x in d], axis=1)
            m_ref[h] = m_new
            if shared:
                ps.append(p)
                alphas.append(alpha)
            else:
                acc_ref[h] = alpha * acc_ref[h] + jnp.dot(p, v_ref[h], preferred_element_type=F32)
        if shared:
            pv = jnp.dot(jnp.concatenate(ps, axis=0), v_ref[0], preferred_element_type=F32)
            for h in range(heads):
                acc_ref[h] = alphas[h] * acc_ref[h] + pv[h * tq:(h + 1) * tq]

    @pl.when(jnp.bitwise_and(flags, 4) != 0)
    def _masked():
        step(True)

    @pl.when(jnp.bitwise_and(flags, 4) == 0)
    def _plain():
        step(False)

    @pl.when(jnp.bitwise_and(flags, 2) != 0)
    def _fin():
        for h in range(heads):
            acc = acc_ref[h]
            o_ref[:, h * 64:(h + 1) * 64] = acc[:, 0:64] / acc[:, 64:128]


def _pair_tables(S, tq, tk, window):
    qi, ki, flags = [], [], []
    for i in range(S // tq):
        q_lo, q_hi = i * tq, i * tq + tq - 1
        lo = 0 if window is None else max(0, q_lo - window + 1) // tk
        hi = q_hi // tk
        for j in range(lo, hi + 1):
            k_lo, k_hi = j * tk, j * tk + tk - 1
            masked = k_hi > q_lo or (window is not None and k_lo <= q_hi - window)
            qi.append(i)
            ki.append(j)
            flags.append((1 if j == lo else 0) | (2 if j == hi else 0) | (4 if masked else 0))
    mk = lambda a: jnp.asarray(np.asarray(a, np.int32))
    return mk(qi), mk(ki), mk(flags)


def _flash(q, k, v, *, window, tq, tk=512):
    B, H, S, dk = q.shape
    Hk = k.shape[1]
    tq, tk = min(tq, S), min(tk, S)
    tabs = _pair_tables(S, tq, tk, window)
    npairs = int(tabs[0].shape[0])
    grid_spec = pltpu.PrefetchScalarGridSpec(
        num_scalar_prefetch=3,
        grid=(B, npairs),
        in_specs=[pl.BlockSpec((None, H, tq, dk), lambda b, p, qi, ki, fl: (b, 0, qi[p], 0)),
                  pl.BlockSpec((None, Hk, tk, dk), lambda b, p, qi, ki, fl: (b, 0, ki[p], 0)),
                  pl.BlockSpec((None, Hk, tk, 128), lambda b, p, qi, ki, fl: (b, 0, ki[p], 0))],
        out_specs=pl.BlockSpec((None, tq, H * 64), lambda b, p, qi, ki, fl: (b, qi[p], 0)),
        scratch_shapes=[pltpu.VMEM((H, tq, 128), F32), pltpu.VMEM((H, tq, 128), F32)],
    )
    return pl.pallas_call(
        functools.partial(_flash_kernel, heads=H, kv_heads=Hk, tq=tq, tk=tk, window=window),
        grid_spec=grid_spec,
        out_shape=jax.ShapeDtypeStruct((B, S, H * 64), F32),
        compiler_params=_cparams(("parallel", "arbitrary")),
        name="flash_attn",
    )(*tabs, q, k, v)


def _compress_kernel(x_ref, wa_ref, wb_ref, pek_ref, pev_ref, wkf_ref, wvf_ref, o_ref, *, nc_pad):
    x = x_ref[...]
    a = _dot_f32(x, wa_ref[...])
    b = _dot_f32(x, wb_ref[...])
    ck = _dot_f32(pek_ref[...], wkf_ref[...])[0:1]
    cv = _dot_f32(pev_ref[...], wvf_ref[...])[0:1]
    const = jnp.concatenate([ck, cv], axis=-1)
    b_next = pltpu.roll(b, nc_pad - 1, axis=0)
    row = lax.broadcasted_iota(jnp.int32, (nc_pad, 128), 0)
    o_ref[...] = jnp.where(row < nc_pad - 1, a + b_next + const, 0.0)


def _compress(x, wa, wb, pek, pev, wkf, wvf):
    B, nc_pad, w = x.shape
    full = lambda a: pl.BlockSpec(a.shape, lambda b: (0,) * a.ndim)
    return pl.pallas_call(
        functools.partial(_compress_kernel, nc_pad=nc_pad),
        grid=(B,),
        in_specs=[pl.BlockSpec((None, nc_pad, w), lambda b: (b, 0, 0)),
                  full(wa), full(wb), full(pek), full(pev), full(wkf), full(wvf)],
        out_specs=pl.BlockSpec((None, nc_pad, 128), lambda b: (b, 0, 0)),
        out_shape=jax.ShapeDtypeStruct((B, nc_pad, 128), F32),
        compiler_params=_cparams(("parallel",)),
        name="nsa_compress",
    )(x, wa, wb, pek, pev, wkf, wvf)


def _cmp_select_kernel(q_ref, kvc_ref, ov_ref, oc_ref, qaug_ref, *, tq, nc_pad, n_top):
    i = pl.program_id(1)
    q = q_ref[...]
    kc = kvc_ref[:, 0:64]
    vc = kvc_ref[:, 64:128]
    scale = NSA_HD ** -0.5
    qpos = i * tq + lax.broadcasted_iota(jnp.int32, (tq, 1), 0)
    n_idx = lax.broadcasted_iota(jnp.int32, (1, nc_pad), 1)
    m_c = (n_idx * CMP_STRIDE + (CMP_LEN - 1) <= qpos) & (n_idx < nc_pad - 1)
    kc_hi, kc_lo = _split2(kc)
    kc3 = jnp.concatenate([kc_hi, kc_lo, kc_hi], axis=1)
    psum = jnp.zeros((tq, nc_pad), F32)
    for h in range(NSA_HEADS):
        q_hi, q_lo = _split2(q[:, h * NSA_HD:(h + 1) * NSA_HD])
        q3 = jnp.concatenate([q_hi, q_hi, q_lo], axis=1)
        s = lax.dot_general(q3, kc3, (((1,), (1,)), ((), ())), preferred_element_type=F32)
        s = jnp.where(m_c, s * scale, NEG)
        e = jnp.where(m_c, jnp.exp(s - jnp.max(s, axis=-1, keepdims=True)), 0.0)
        den = jnp.sum(e, axis=-1, keepdims=True)
        p = e / jnp.where(den > 0.0, den, 1.0)
        oc_ref[:, h * NSA_HD:(h + 1) * NSA_HD] = _dot(p, vc)
        psum = psum + p
    ov = ov_ref[...]
    p_hi = psum.astype(BF16)
    p_r = psum - p_hi.astype(F32)
    p_mid = p_r.astype(BF16)
    p_lo = (p_r - p_mid.astype(F32)).astype(BF16)
    imp = (jnp.dot(p_hi, ov, preferred_element_type=F32) + jnp.dot(p_mid, ov, preferred_element_type=F32)
           + jnp.dot(p_lo, ov, preferred_element_type=F32))
    cur = jnp.right_shift(qpos, 6)
    jj = lax.broadcasted_iota(jnp.int32, (1, SEL_LANES), 1)
    forced = (jj == 0) | (jj == cur) | (jj == cur - 1)
    bias = jnp.where(forced, 0.0, NEG)
    imp = jnp.where((jj <= cur) & jnp.logical_not(forced), imp, NEG)
    jf = jj.astype(F32)
    for _ in range(n_top - 3):
        mx = jnp.max(imp, axis=-1, keepdims=True)
        idx = jnp.min(jnp.where(imp == mx, jf, float(SEL_LANES)), axis=-1, keepdims=True)
        hit = jf == idx
        bias = jnp.where(hit & (mx > 0.5 * NEG), 0.0, bias)
        imp = jnp.where(hit, -3e38, imp)
    bias = bias.astype(BF16)
    qs = (q * (scale * LOG2E)).astype(BF16)
    for h in range(NSA_HEADS):
        qaug_ref[h, :, 0:64] = qs[:, h * NSA_HD:(h + 1) * NSA_HD]
        qaug_ref[h, :, 64:128] = jnp.zeros((tq, 64), BF16)
        qaug_ref[h, :, 128:256] = bias


def _cmp_select(qn, kvcmp, ov, B, S):
    tq = min(1024, S)
    nb = S // tq
    nc_pad = kvcmp.shape[1]
    n_top = min(SEL_TOPN, S // SEL_LEN)
    assert n_top >= 3
    return pl.pallas_call(
        functools.partial(_cmp_select_kernel, tq=tq, nc_pad=nc_pad, n_top=n_top),
        grid=(B, nb),
        in_specs=[pl.BlockSpec((tq, 256), lambda b, i: (b * nb + i, 0)),
                  pl.BlockSpec((None, nc_pad, 128), lambda b, i: (b, 0, 0)),
                  pl.BlockSpec(ov.shape, lambda b, i: (0, 0))],
        out_specs=[pl.BlockSpec((tq, 256), lambda b, i: (b * nb + i, 0)),
                   pl.BlockSpec((None, NSA_HEADS, tq, 256), lambda b, i: (b, 0, i, 0))],
        out_shape=[jax.ShapeDtypeStruct((B * S, 256), F32),
                   jax.ShapeDtypeStruct((B, NSA_HEADS, S, 256), BF16)],
        compiler_params=_cparams(("parallel", "parallel")),
        name="nsa_cmp_select",
    )(qn, kvcmp, ov)


def _shift_scan(a, b, t, width):
    row = lax.broadcasted_iota(jnp.int32, (t, width), 0)
    s = 1
    while s < t:
        keep = row >= s
        a_sh = jnp.where(keep, pltpu.roll(a, s, axis=0), 1.0)
        b_sh = jnp.where(keep, pltpu.roll(b, s, axis=0), 0.0)
        b = a * b_sh + b
        a = a * a_sh
        s *= 2
    return a, b


def _causal_conv(xpad_ref, x, cw_ref, cb_ref, t):
    xpad_ref[8:8 + t, :] = x
    y = cb_ref[...] + cw_ref[CONV_W - 1:CONV_W, :] * x
    for k in range(CONV_W - 1):
        off = 8 - (CONV_W - 1) + k
        y = y + cw_ref[k:k + 1, :] * xpad_ref[off:off + t, :]
    xpad_ref[0:8, :] = x[t - 8:t, :]
    return y


def _lru_kernel(u_ref, cw_ref, cb_ref, wg_ref, bg_ref, sp_ref, y_ref, xpad_ref, h_ref, *, t):
    @pl.when(pl.program_id(1) == 0)
    def _init():
        xpad_ref[0:8, :] = jnp.zeros((8, LRU_W), F32)
        h_ref[...] = jnp.zeros(h_ref.shape, F32)

    x = u_ref[:, 0:LRU_W]
    gbr = u_ref[:, LRU_W:2 * LRU_W]
    xb = _causal_conv(xpad_ref, x, cw_ref, cb_ref, t)
    g = _dot(xb, wg_ref[...]) + bg_ref[...]
    r = _sigmoid(g[:, 0:LRU_W])
    ig = _sigmoid(g[:, LRU_W:2 * LRU_W])
    log_a = -LRU_C * r * sp_ref[...]
    a = jnp.exp(log_a)
    one_m = -jnp.tanh(log_a) * (a * a + 1.0)
    b = jnp.sqrt(jnp.maximum(one_m, 0.0)) * (ig * xb)
    a_cum, h = _shift_scan(a, b, t, LRU_W)
    h = h + a_cum * h_ref[0:1, :]
    h_ref[0:1, :] = h[t - 1:t, :]
    gelu = 0.5 * gbr * (1.0 + jnp.tanh(0.7978845608028654 * (gbr + 0.044715 * gbr * gbr * gbr)))
    y_ref[...] = h * gelu


def _lru(uc, cw, cb, wg, bg, sp, B, S):
    t = 512
    nb = S // t
    full = lambda a: pl.BlockSpec(a.shape, lambda b, i: (0,) * a.ndim)
    return pl.pallas_call(
        functools.partial(_lru_kernel, t=t),
        grid=(B, nb),
        in_specs=[pl.BlockSpec((t, 2 * LRU_W), lambda b, i: (b * nb + i, 0)),
                  full(cw), full(cb), full(wg), full(bg), full(sp)],
        out_specs=pl.BlockSpec((t, LRU_W), lambda b, i: (b * nb + i, 0)),
        out_shape=jax.ShapeDtypeStruct((B * S, LRU_W), F32),
        scratch_shapes=[pltpu.VMEM((t + 8, LRU_W), F32), pltpu.VMEM((8, LRU_W), F32)],
        compiler_params=_cparams(("parallel", "arbitrary")),
        name="rglru",
    )(uc, cw, cb, wg, bg, sp)


def _ssd_kernel(u_ref, cw_ref, cb_ref, dtb_ref, a_ref, d_ref, ng_ref, y_ref, xpad_ref, st_ref, *, t):
    @pl.when(pl.program_id(1) == 0)
    def _init():
        xpad_ref[0:8, :] = jnp.zeros((8, SSD_XBC), F32)
        st_ref[...] = jnp.zeros(st_ref.shape, F32)

    L = SSD_CHUNK
    z = u_ref[:, 0:SSD_DI]
    conv = _causal_conv(xpad_ref, u_ref[:, SSD_DI:SSD_DI + SSD_XBC], cw_ref, cb_ref, t)
    xbc = conv * _sigmoid(conv)
    dt = _softplus(u_ref[:, SSD_DI + SSD_XBC:SSD_IN_W] + dtb_ref[...])
    a = dt * a_ref[...]
    row = lax.broadcasted_iota(jnp.int32, (t, 128), 0)
    rin = jnp.bitwise_and(row, L - 1)
    cs = a
    s = 1
    while s < L:
        cs = cs + jnp.where(rin >= s, pltpu.roll(cs, s, axis=0), 0.0)
        s *= 2
    tril = lax.broadcasted_iota(jnp.int32, (L, L), 0) >= lax.broadcasted_iota(jnp.int32, (L, L), 1)
    gn = SSD_GROUPS * SSD_STATE
    rep = SSD_HEADS // SSD_GROUPS
    for c in range(t // L):
        rs = slice(c * L, (c + 1) * L)
        cs_c = cs[rs]
        cs_t = cs_c.T
        cs_last = cs_c[L - 1:L, :]
        ys = []
        for g in range(SSD_GROUPS):
            bg = xbc[rs, SSD_DI + g * SSD_STATE:SSD_DI + (g + 1) * SSD_STATE]
            cg = xbc[rs, SSD_DI + gn + g * SSD_STATE:SSD_DI + gn + (g + 1) * SSD_STATE]
            gmat = _dot_nt(cg, bg)
            bg_t = bg.T
            for hh in range(rep):
                h = g * rep + hh
                cs_col = cs_c[:, h:h + 1]
                lm = jnp.exp(jnp.where(tril, cs_col - cs_t[h:h + 1, :], NEG))
                xh = xbc[rs, h * SSD_HD:(h + 1) * SSD_HD]
                xdt = xh * dt[rs, h:h + 1]
                st_old = st_ref[h]
                y_h = _dot(gmat * lm, xdt) + _dot(cg, st_old) * jnp.exp(cs_col)
                dec = jnp.exp(cs_last[:, h:h + 1] - cs_col)
                st_ref[h] = jnp.exp(cs_last[:, h:h + 1]) * st_old + _dot(bg_t, xdt * dec)
                ys.append(y_h)
        y = jnp.concatenate(ys, axis=-1) + xbc[rs, 0:SSD_DI] * d_ref[...]
        zc = z[rs]
        y = y * (zc * _sigmoid(zc))
        y_ref[rs, :] = _rms(y, SSD_DI) * ng_ref[...]


def _ssd(ud, cw, cb, dtb, a_neg, d_vec, ng, B, S):
    t = 512
    nb = S // t
    full = lambda a: pl.BlockSpec(a.shape, lambda b, i: (0,) * a.ndim)
    return pl.pallas_call(
        functools.partial(_ssd_kernel, t=t),
        grid=(B, nb),
        in_specs=[pl.BlockSpec((t, SSD_IN_W), lambda b, i: (b * nb + i, 0)),
                  full(cw), full(cb), full(dtb), full(a_neg), full(d_vec), full(ng)],
        out_specs=pl.BlockSpec((t, SSD_DI), lambda b, i: (b * nb + i, 0)),
        out_shape=jax.ShapeDtypeStruct((B * S, SSD_DI), F32),
        scratch_shapes=[pltpu.VMEM((t + 8, SSD_XBC), F32), pltpu.VMEM((SSD_HEADS, SSD_STATE, SSD_HD), F32)],
        compiler_params=_cparams(("parallel", "arbitrary")),
        name="ssd",
    )(ud, cw, cb, dtb, a_neg, d_vec, ng)


def _outproj_kernel(*refs, with_router):
    if with_router:
        (ya_ref, oc_ref, os_ref, ow_ref, gate_ref, yc_ref, yd_ref, res_ref, gn_ref, ex_ref, w_ref, nf_ref,
         rw_ref, hres_ref, hn_ref, rg_ref, cnt_ref, carry_ref) = refs
    else:
        (ya_ref, oc_ref, os_ref, ow_ref, gate_ref, yc_ref, yd_ref, res_ref, gn_ref, ex_ref, w_ref, nf_ref,
         hres_ref, hn_ref) = refs
    sg_hi, sg_lo = _split2(_sigmoid(gate_ref[...]))
    ex = ex_ref[...]
    gx = jnp.dot(sg_hi, ex, preferred_element_type=F32) + jnp.dot(sg_lo, ex, preferred_element_type=F32)
    yb = gx[:, 0:256] * oc_ref[...] + gx[:, 256:512] * os_ref[...] + gx[:, 512:768] * ow_ref[...]
    y = jnp.concatenate([_rms(ya_ref[...], GROUP_W) * gn_ref[0:1, :],
                         _rms(yb, GROUP_W) * gn_ref[1:2, :],
                         _rms(yc_ref[...], GROUP_W) * gn_ref[2:3, :],
                         yd_ref[...]], axis=-1)
    hres = res_ref[...] + _dot(y, w_ref[...])
    hres_ref[...] = hres
    hn = _rms(hres, D_MODEL) * nf_ref[...]
    hn_ref[...] = hn.astype(hn_ref.dtype)
    if with_router:
        h_hi, h_lo = _split2(hn)
        logits = (jnp.dot(h_hi, rw_ref[0], preferred_element_type=F32)
                  + jnp.dot(h_hi, rw_ref[1], preferred_element_type=F32)
                  + jnp.dot(h_lo, rw_ref[0], preferred_element_type=F32))
        lane = lax.broadcasted_iota(jnp.int32, logits.shape, 1)
        lf = lane.astype(F32)
        logits = jnp.where(lane < N_EXPERTS, logits, NEG)
        m1 = jnp.max(logits, axis=-1, keepdims=True)
        i1 = jnp.min(jnp.where(logits == m1, lf, 128.0), axis=-1, keepdims=True)
        rest = jnp.where(lf == i1, NEG, logits)
        m2 = jnp.max(rest, axis=-1, keepdims=True)
        i2 = jnp.min(jnp.where(rest == m2, lf, 128.0), axis=-1, keepdims=True)
        e2 = jnp.exp(m2 - m1)
        den = 1.0 + e2
        @pl.when(pl.program_id(0) == 0)
        def _init():
            carry_ref[...] = jnp.zeros(carry_ref.shape, F32)

        tm = logits.shape[0]
        oh = jnp.where((lf == i1) | (lf == i2), 1.0, 0.0)
        row = lax.broadcasted_iota(jnp.int32, oh.shape, 0)
        cs = oh
        s = 1
        while s < tm:
            cs = cs + jnp.where(row >= s, pltpu.roll(cs, s, axis=0), 0.0)
            s *= 2
        excl = cs - oh + carry_ref[0:1, :]
        rank1 = jnp.sum(jnp.where(lf == i1, excl, 0.0), axis=-1, keepdims=True)
        rank2 = jnp.sum(jnp.where(lf == i2, excl, 0.0), axis=-1, keepdims=True)
        carry_ref[0:1, :] = carry_ref[0:1, :] + cs[tm - 1:tm, :]
        cnt_ref[...] = carry_ref[...]
        rg_ref[...] = jnp.where(lane == 0, i1, jnp.where(lane == 1, i2, jnp.where(
            lane == 2, 1.0 / den, jnp.where(lane == 3, e2 / den, jnp.where(
                lane == 4, rank1, jnp.where(lane == 5, rank2, 0.0))))))


def _outproj(ya, oc, osel, ow, gate, yc, yd, res, gn, ex, w, nf, rw):
    T = ya.shape[0]
    tm = 512
    with_router = rw is not None
    row = lambda wd: pl.BlockSpec((tm, wd), lambda i: (i, 0))
    full = lambda a: pl.BlockSpec(a.shape, lambda i: (0,) * a.ndim)
    ins = [ya, oc, osel, ow, gate, yc, yd, res, gn, ex, w, nf]
    in_specs = [row(256), row(256), row(256), row(256), row(128), row(256), row(256), row(D_MODEL),
                full(gn), full(ex), full(w), full(nf)]
    out_specs = [row(D_MODEL), row(D_MODEL)]
    out_shape = [jax.ShapeDtypeStruct((T, D_MODEL), F32),
                 jax.ShapeDtypeStruct((T, D_MODEL), F32 if with_router else BF16)]
    if with_router:
        ins.append(rw)
        in_specs.append(full(rw))
        out_specs += [row(128), pl.BlockSpec((8, 128), lambda i: (0, 0))]
        out_shape += [jax.ShapeDtypeStruct((T, 128), F32), jax.ShapeDtypeStruct((8, 128), F32)]
    return pl.pallas_call(
        functools.partial(_outproj_kernel, with_router=with_router),
        grid=(T // tm,),
        in_specs=in_specs,
        out_specs=out_specs,
        out_shape=out_shape,
        scratch_shapes=[pltpu.VMEM((8, 128), F32)] if with_router else [],
        compiler_params=_cparams(("arbitrary" if with_router else "parallel",)),
        name="outproj",
    )(*ins)


def _ffn_kernel(h_ref, res_ref, wg_ref, wu_ref, wd_ref, o_ref):
    j = pl.program_id(1)

    @pl.when(j == 0)
    def _init():
        o_ref[...] = res_ref[...]

    h = h_ref[...]
    g = jnp.dot(h, wg_ref[...], preferred_element_type=F32)
    u = jnp.dot(h, wu_ref[...], preferred_element_type=F32)
    o_ref[...] += _dot(g * _sigmoid(g) * u, wd_ref[...])


def _ffn(hn, res, wg, wu, wd):
    T = hn.shape[0]
    tm, tf = 1024, 512
    return pl.pallas_call(
        _ffn_kernel,
        grid=(T // tm, D_FF // tf),
        in_specs=[pl.BlockSpec((tm, D_MODEL), lambda i, j: (i, 0)),
                  pl.BlockSpec((tm, D_MODEL), lambda i, j: (i, 0)),
                  pl.BlockSpec((D_MODEL, tf), lambda i, j: (0, j)),
                  pl.BlockSpec((D_MODEL, tf), lambda i, j: (0, j)),
                  pl.BlockSpec((tf, D_MODEL), lambda i, j: (j, 0))],
        out_specs=pl.BlockSpec((tm, D_MODEL), lambda i, j: (i, 0)),
        out_shape=jax.ShapeDtypeStruct((T, D_MODEL), F32),
        compiler_params=_cparams(("parallel", "arbitrary")),
        name="ffn_dense",
    )(hn, res, wg, wu, wd)


MOE_TILE = 1024
MOE_TM = 512


def _row_copies(idx_ref, tm, make_copy):
    def body(r, carry):
        for k in range(2):
            make_copy(k, r, idx_ref[k * tm + r]).start(priority=k)
        return carry

    lax.fori_loop(0, tm, body, 0, unroll=8)


def _dispatch_kernel(ends_ref, slots_ref, h_ref, xs_ref, idx_ref, zero_ref, stage_ref, sem_idx, sem_row, *, tm):
    i = pl.program_id(0)

    @pl.when(i == 0)
    def _zero_tiles():
        zero_ref[...] = jnp.zeros(zero_ref.shape, F32)
        zr = zero_ref.shape[0]
        jobs = []
        for e in range(N_EXPERTS):
            start = ends_ref[e - 1] if e else 0
            jobs.append((ends_ref[e] > start, ends_ref[e] - MOE_TILE))
        for tile in range(N_EXPERTS):
            base = xs_ref.shape[0] - (tile + 1) * MOE_TILE
            jobs.append((base >= ends_ref[N_EXPERTS - 1], base))

        def tile_copies(base):
            rows = [base + q * zr for q in range(MOE_TILE // zr)]
            rows = [r if isinstance(r, int) else pl.multiple_of(r, zr) for r in rows]
            return [pltpu.make_async_copy(zero_ref, xs_ref.at[pl.ds(r, zr)], sem_idx) for r in rows]

        for cond, base in jobs:
            @pl.when(cond)
            def _start(base=base):
                for cp in tile_copies(base):
                    cp.start()
        for cond, base in jobs:
            @pl.when(cond)
            def _wait(base=base):
                for cp in tile_copies(base):
                    cp.wait()

    cp = pltpu.make_async_copy(slots_ref.at[pl.ds(i * 2 * tm, 2 * tm)], idx_ref, sem_idx)
    cp.start()
    cp.wait()
    cur = lax.rem(i, 2)

    def wait_tile(b):
        for _ in range(2):
            pltpu.make_async_copy(stage_ref.at[b], xs_ref.at[pl.ds(0, tm)], sem_row.at[b]).wait()

    @pl.when(i > 1)
    def _reuse():
        wait_tile(cur)

    stage_ref[cur] = h_ref[...]
    _row_copies(idx_ref, tm, lambda k, r, s: pltpu.make_async_copy(
        stage_ref.at[cur, pl.ds(r, 1)], xs_ref.at[pl.ds(s, 1)], sem_row.at[cur]))

    @pl.when(i == pl.num_programs(0) - 1)
    def _drain():
        @pl.when(i > 0)
        def _prev():
            wait_tile(1 - cur)
        wait_tile(cur)


def _dispatch(ends, slots, hn, n_slots):
    T = hn.shape[0]
    tm = MOE_TM
    grid_spec = pltpu.PrefetchScalarGridSpec(
        num_scalar_prefetch=1,
        grid=(T // tm,),
        in_specs=[pl.BlockSpec(memory_space=pl.ANY),
                  pl.BlockSpec((tm, D_MODEL), lambda i, ends: (i, 0))],
        out_specs=pl.BlockSpec(memory_space=pl.ANY),
        scratch_shapes=[pltpu.SMEM((2 * tm,), jnp.int32), pltpu.VMEM((256, D_MODEL), F32),
                        pltpu.VMEM((2, tm, D_MODEL), F32),
                        pltpu.SemaphoreType.DMA, pltpu.SemaphoreType.DMA((2,))],
    )
    return pl.pallas_call(
        functools.partial(_dispatch_kernel, tm=tm),
        grid_spec=grid_spec,
        out_shape=jax.ShapeDtypeStruct((n_slots, D_MODEL), F32),
        compiler_params=_cparams(("arbitrary",)),
        name="moe_dispatch",
    )(ends, slots, hn)


def _expert_ffn_kernel(te_ref, nu_ref, x_ref, wg_ref, wu_ref, wd_ref, o_ref, xb_ref):
    del te_ref
    i = pl.program_id(0)
    j = pl.program_id(1)
    used = i < nu_ref[0]

    @pl.when(j == 0)
    def _init():
        o_ref[...] = jnp.zeros(o_ref.shape, F32)
        xb_ref[...] = x_ref[...].astype(BF16)

    @pl.when(used)
    def _step():
        x = xb_ref[...]
        g = _dot(x, wg_ref[...])
        u = _dot(x, wu_ref[...])
        o_ref[...] += _dot(g * _sigmoid(g) * u, wd_ref[...])


def _expert_ffn(tile_expert, n_used, xs, wg, wu, wd):
    n_slots = xs.shape[0]
    tm, tf = MOE_TILE, 512
    jw = lambda i, j, nu: jnp.where(i < nu[0], j, 0)
    grid_spec = pltpu.PrefetchScalarGridSpec(
        num_scalar_prefetch=2,
        grid=(n_slots // tm, D_FF // tf),
        in_specs=[pl.BlockSpec((tm, D_MODEL), lambda i, j, te, nu: (jnp.minimum(i, nu[0] - 1), 0)),
                  pl.BlockSpec((None, D_MODEL, tf), lambda i, j, te, nu: (te[i], 0, jw(i, j, nu))),
                  pl.BlockSpec((None, D_MODEL, tf), lambda i, j, te, nu: (te[i], 0, jw(i, j, nu))),
                  pl.BlockSpec((None, tf, D_MODEL), lambda i, j, te, nu: (te[i], jw(i, j, nu), 0))],
        out_specs=pl.BlockSpec((tm, D_MODEL), lambda i, j, te, nu: (i, 0)),
        scratch_shapes=[pltpu.VMEM((tm, D_MODEL), BF16)],
    )
    return pl.pallas_call(
        _expert_ffn_kernel,
        grid_spec=grid_spec,
        out_shape=jax.ShapeDtypeStruct((n_slots, D_MODEL), F32),
        compiler_params=_cparams(("parallel", "arbitrary")),
        name="moe_expert_ffn",
    )(tile_expert, n_used, xs, wg, wu, wd)


def _combine_kernel(slots_ref, ys_ref, route_ref, res_ref, ng_ref, o_ref, idx_ref, buf_ref, sem_idx, sem_row,
                    *, tm, final_norm):
    i = pl.program_id(0)
    cur = lax.rem(i, 2)

    def gather(step, b):
        cp = pltpu.make_async_copy(slots_ref.at[pl.ds(step * 2 * tm, 2 * tm)], idx_ref, sem_idx)
        cp.start()
        cp.wait()
        _row_copies(idx_ref, tm, lambda k, r, s: pltpu.make_async_copy(
            ys_ref.at[pl.ds(s, 1)], buf_ref.at[b, k, pl.ds(r, 1)], sem_row.at[b]))

    @pl.when(i == 0)
    def _first():
        gather(0, 0)

    @pl.when(i + 1 < pl.num_programs(0))
    def _next():
        gather(i + 1, 1 - cur)

    for k in range(2):
        pltpu.make_async_copy(ys_ref.at[pl.ds(0, tm)], buf_ref.at[cur, k], sem_row.at[cur]).wait()
    route = route_ref[...]
    out = res_ref[...] + route[:, 2:3] * buf_ref[cur, 0] + route[:, 3:4] * buf_ref[cur, 1]
    if final_norm:
        out = _rms(out, D_MODEL) * ng_ref[...]
    o_ref[...] = out


def _combine(slots, ys, route, res, norm_g, final_norm):
    T = res.shape[0]
    tm = MOE_TM
    return pl.pallas_call(
        functools.partial(_combine_kernel, tm=tm, final_norm=final_norm),
        grid=(T // tm,),
        in_specs=[pl.BlockSpec(memory_space=pl.ANY),
                  pl.BlockSpec(memory_space=pl.ANY),
                  pl.BlockSpec((tm, 128), lambda i: (i, 0)),
                  pl.BlockSpec((tm, D_MODEL), lambda i: (i, 0)),
                  pl.BlockSpec((1, D_MODEL), lambda i: (0, 0))],
        out_specs=pl.BlockSpec((tm, D_MODEL), lambda i: (i, 0)),
        out_shape=jax.ShapeDtypeStruct((T, D_MODEL), F32),
        scratch_shapes=[pltpu.SMEM((2 * tm,), jnp.int32), pltpu.VMEM((2, 2, tm, D_MODEL), F32),
                        pltpu.SemaphoreType.DMA, pltpu.SemaphoreType.DMA((2,))],
        compiler_params=_cparams(("arbitrary",)),
        name="moe_combine",
    )(slots, ys, route, res, norm_g)


def _moe(hn, res, route, cnt, wg, wu, wd, norm_g, final_norm):
    T = hn.shape[0]
    n_slots = 2 * T + N_EXPERTS * MOE_TILE
    n_tiles = n_slots // MOE_TILE
    rank = route[:, 4:6]
    counts = cnt[0, :N_EXPERTS].astype(jnp.int32)
    padded = (counts + MOE_TILE - 1) // MOE_TILE * MOE_TILE
    ends = jnp.cumsum(padded)
    starts = ends - padded
    ids = route[:, 0:2].astype(jnp.int32)
    slot = jnp.take(starts, ids) + rank[:, 0:2].astype(jnp.int32)
    slots = slot.reshape(T // MOE_TM, MOE_TM, 2).transpose(0, 2, 1).reshape(-1)
    tile_start = jnp.arange(n_tiles, dtype=jnp.int32) * MOE_TILE
    tile_expert = jnp.minimum(jnp.sum(tile_start[:, None] >= ends[None, :], axis=1), N_EXPERTS - 1).astype(jnp.int32)
    n_used = (ends[-1:] // MOE_TILE).astype(jnp.int32)
    xs = _dispatch(ends.astype(jnp.int32), slots, hn, n_slots)
    ys = _expert_ffn(tile_expert, n_used, xs, wg, wu, wd)
    return _combine(slots, ys, route, res, norm_g, final_norm)


def _final_norm_kernel(x_ref, g_ref, o_ref):
    o_ref[...] = _rms(x_ref[...], D_MODEL) * g_ref[...]


def _final_norm(x2d, g):
    T = x2d.shape[0]
    tm = 1024
    return pl.pallas_call(
        _final_norm_kernel,
        grid=(T // tm,),
        in_specs=[pl.BlockSpec((tm, D_MODEL), lambda i: (i, 0)), pl.BlockSpec(g.shape, lambda i: (0, 0))],
        out_specs=pl.BlockSpec((tm, D_MODEL), lambda i: (i, 0)),
        out_shape=jax.ShapeDtypeStruct((T, D_MODEL), F32),
        compiler_params=_cparams(("parallel",)),
        name="final_norm",
    )(x2d, g)


def _rot_cols(w, half):
    return jnp.concatenate([-w[:, half:2 * half], w[:, 0:half]], axis=1)


def _aligned_w_in(w):
    z = lambda n: jnp.zeros((w.shape[0], n), F32)
    return jnp.concatenate([w[:, 0:O_NSA], z(P_NSA - O_NSA),
                            w[:, O_NSA:O_LRU], z(P_LRU - P_NSA - (O_LRU - O_NSA)),
                            w[:, O_LRU:], z(W_RAW - P_LRU - (w.shape[1] - O_LRU))], axis=1).astype(BF16)


def _rope_tables(positions):
    pos = positions.astype(F32)[:, None]
    S = positions.shape[0]

    def cs(rot_dim):
        inv = ROPE_THETA ** (-jnp.arange(0, rot_dim, 2, dtype=F32) / rot_dim)
        ang = pos * inv[None, :]
        return jnp.cos(ang), jnp.sin(ang)

    cm, sm = cs(MLA_ROPE)
    cn, sn = cs(NSA_ROT)
    hm, hn = MLA_ROPE // 2, NSA_ROT // 2
    base = jnp.concatenate([cm, sm, cn, sn, jnp.ones((S, 1), F32)], axis=1)
    c_cm, c_sm, c_cn, c_sn, c_one = 0, hm, 2 * hm, 2 * hm + hn, 2 * hm + 2 * hn

    place = np.zeros((base.shape[1], 6 * 128), np.float32)

    def put(tile, lane0, col0, n, sign=1.0):
        for j in range(n):
            place[col0 + j, tile * 128 + lane0 + j] = sign

    def ones(tile, lane0, n):
        place[c_one, tile * 128 + lane0:tile * 128 + lane0 + n] = 1.0

    ones(0, 0, 64), put(0, 64, c_cm, hm), put(0, 64 + hm, c_cm, hm), ones(0, 96, 32)
    put(1, 64 + hm, c_sm, hm)
    put(2, 64, c_sm, hm, -1.0)
    for g in (0, 64):
        put(3, g, c_cn, hn), put(3, g + hn, c_cn, hn), ones(3, g + 2 * hn, NSA_HD - 2 * hn)
        put(4, g + hn, c_sn, hn)
        put(5, g, c_sn, hn, -1.0)
    return jnp.dot(base, jnp.asarray(place), precision=lax.Precision.HIGHEST)


def _mla_weights(w_uq, w_ukv, q_norm, kv_norm):
    hw = MLA_NOPE + MLA_ROPE
    pad_r = lambda m: jnp.concatenate([m, jnp.zeros((256 - MLA_Q_RANK, m.shape[1]), F32)], axis=0)
    z = lambda n: jnp.zeros((MLA_Q_RANK, n), F32)
    zk = lambda n: jnp.zeros((MLA_KV_RANK, n), F32)
    w1, w2, wk, wv = [], [], [], []
    for h in range(MLA_HEADS):
        qh = w_uq[:, h * hw:(h + 1) * hw]
        w1 += [qh, z(128 - hw)]
        w2 += [z(MLA_NOPE), _rot_cols(qh[:, MLA_NOPE:], MLA_ROPE // 2), z(128 - hw)]
        kvh = w_ukv[:, h * (MLA_NOPE + MLA_V):(h + 1) * (MLA_NOPE + MLA_V)]
        wk += [kvh[:, :MLA_NOPE], zk(128 - MLA_NOPE)]
        wv += [kvh[:, MLA_NOPE:], zk(128 - MLA_V)]
    qg = jnp.concatenate([q_norm, jnp.zeros((256 - MLA_Q_RANK,), F32)])[None, :]
    return (qg, pad_r(jnp.concatenate(w1, axis=1)).astype(BF16), pad_r(jnp.concatenate(w2, axis=1)).astype(BF16),
            kv_norm[None, :], jnp.concatenate(wk, axis=1).astype(BF16), jnp.concatenate(wv, axis=1).astype(BF16))


def _compress_weights(pe, w_cmp):
    half = CMP_LEN // 2
    wk = w_cmp[0].reshape(CMP_LEN, NSA_HD, NSA_HD)
    wv = w_cmp[1].reshape(CMP_LEN, NSA_HD, NSA_HD)
    z = jnp.zeros((half, NSA_HD, NSA_HD), F32)

    def interleave(ks, vs):
        top = jnp.concatenate([ks, z], axis=-1)
        bot = jnp.concatenate([z, vs], axis=-1)
        return jnp.concatenate([top, bot], axis=1).reshape(half * 2 * NSA_HD, 2 * NSA_HD)

    pad8 = lambda p: jnp.concatenate([p.reshape(1, -1), jnp.zeros((7, CMP_LEN * NSA_HD), F32)], axis=0)
    return (interleave(wk[:half], wv[:half]), interleave(wk[half:], wv[half:]),
            pad8(pe[0]), pad8(pe[1]), w_cmp[0], w_cmp[1])


def _overlap_matrix(S):
    nc_pad = S // CMP_STRIDE
    n = np.arange(nc_pad)[:, None]
    j = np.arange(SEL_LANES)[None, :]
    ov = ((n * CMP_STRIDE <= j * SEL_LEN + SEL_LEN - 1) & (n * CMP_STRIDE + CMP_LEN - 1 >= j * SEL_LEN)
          & (n < nc_pad - 1) & (j < S // SEL_LEN))
    return jnp.asarray(ov.astype(np.float32)).astype(BF16)


def _gate_expand():
    gw = NSA_HEADS * NSA_HD
    ex = np.zeros((128, 3 * gw), np.float32)
    for br in range(3):
        for h in range(NSA_HEADS):
            ex[h * 3 + br, br * gw + h * NSA_HD:br * gw + (h + 1) * NSA_HD] = 1.0
    return jnp.asarray(ex).astype(BF16)


def _lru_gate_weights(w_gate, b_gate):
    wg = jnp.zeros((LRU_W, 2 * LRU_W), F32)
    bw = LRU_W // LRU_BLOCKS
    for g in range(2):
        for n in range(LRU_BLOCKS):
            wg = wg.at[n * bw:(n + 1) * bw, g * LRU_W + n * bw:g * LRU_W + (n + 1) * bw].set(w_gate[g, n])
    return wg.astype(BF16), b_gate.reshape(1, 2 * LRU_W)


def _pad_lanes(v, n=128):
    return jnp.concatenate([v, jnp.zeros((n - v.shape[0],), F32)])[None, :]


def kernel(x, positions, norm_mix, w_in, mla_q_norm, mla_w_uq, mla_kv_norm, mla_w_ukv, nsa_cmp_pe, nsa_w_cmp,
           lru_conv_w, lru_conv_b, lru_w_gate, lru_b_gate, lru_lambda, ssd_conv_w, ssd_conv_b, ssd_dt_bias,
           ssd_a_log, ssd_d, group_norm, w_out, norm_ffn, ffn_w_gate, ffn_w_up, ffn_w_down, moe_router,
           moe_w_gate, moe_w_up, moe_w_down, norm_final):
    B, S, D = x.shape
    T = B * S
    depth = w_in.shape[0]
    assert S // SEL_LEN <= SEL_LANES and S % 512 == 0
    rope_tab = _rope_tables(positions)
    ov = _overlap_matrix(S)
    ex = _gate_expand()
    h_res = x.reshape(T, D)
    for l in range(depth):
        w_raw = _aligned_w_in(w_in[l])
        ua, qn, kvc, ksel, kwin, vsel, vwin, gate, uc, ud = _inproj(
            h_res, norm_mix[l][None, :], w_raw, rope_tab, S)

        q_m, k_m, v_m = _mla_prep(ua, *_mla_weights(mla_w_uq[l], mla_w_ukv[l], mla_q_norm[l], mla_kv_norm[l]),
                                  rope_tab, B, S)
        y_a = _flash(q_m, k_m, v_m, window=None, tq=1024).reshape(T, GROUP_W)

        kvcmp = _compress(kvc.reshape(B, S // CMP_STRIDE, CMP_STRIDE * 128),
                          *_compress_weights(nsa_cmp_pe[l], nsa_w_cmp[l]))
        o_c, q_aug = _cmp_select(qn, kvcmp, ov, B, S)
        o_s = _flash(q_aug, ksel.reshape(B, 1, S, 256), vsel.reshape(B, 1, S, 128), window=None,
                     tq=1024).reshape(T, GROUP_W)
        o_w = _flash(q_aug, kwin.reshape(B, 1, S, 256), vwin.reshape(B, 1, S, 128), window=WINDOW,
                     tq=512).reshape(T, GROUP_W)

        wg_l, bg_l = _lru_gate_weights(lru_w_gate[l], lru_b_gate[l])
        y_c = _lru(uc, lru_conv_w[l], lru_conv_b[l][None, :], wg_l, bg_l,
                   jax.nn.softplus(-lru_lambda[l])[None, :], B, S)

        y_d = _ssd(ud, ssd_conv_w[l], ssd_conv_b[l][None, :], _pad_lanes(ssd_dt_bias[l]),
                   _pad_lanes(-jnp.exp(ssd_a_log[l])), jnp.repeat(ssd_d[l], SSD_HD)[None, :],
                   group_norm[l, 3][None, :], B, S)

        moe_layer = l % 2 == 1
        rw = None
        if moe_layer:
            rw = jnp.concatenate([moe_router[l // 2], jnp.zeros((D, 128 - N_EXPERTS), F32)], axis=1)
            rw = jnp.stack(_split2(rw))
        outs = _outproj(y_a, o_c, o_s, o_w, gate, y_c, y_d, h_res, group_norm[l], ex, w_out[l].astype(BF16),
                        norm_ffn[l][None, :], rw)
        if moe_layer:
            h_res, hn, rg, cnt = outs
            normed = l == depth - 1
            h_res = _moe(hn, h_res, rg, cnt, moe_w_gate[l // 2], moe_w_up[l // 2], moe_w_down[l // 2],
                         norm_final[None, :], normed)
        else:
            normed = False
            h_res, hn = outs
            h_res = _ffn(hn, h_res, ffn_w_gate[l // 2].astype(BF16), ffn_w_up[l // 2].astype(BF16),
                         ffn_w_down[l // 2].astype(BF16))
    if not normed:
        h_res = _final_norm(h_res, norm_final[None, :])
    return h_res.reshape(B, S, D).astype(x.dtype)
```
